```python
import jax
import jax.numpy as jnp
from jax import lax
import numpy as np

D_MODEL = 4096
BATCH = 1
SEQ = 16384
DEPTH = 4

GRID_W = 64
CTX_LEN = 256
ROPE_DIM = 64
ROPE_THETA = 10000.0
NORM_EPS = 1e-6
NEG_INF = -1e30

ADA_RANK = 256
N_MOD = 6

A_HEADS = 16
A_HEAD_DIM = 64
A_WIDTH = A_HEADS * A_HEAD_DIM
A_DECAY_RANK = 128
A_ICLR_RANK = 128
A_GATE_RANK = 480
A_LNX_EPS = 64e-5
N_A = 3 * A_WIDTH + 2 * A_DECAY_RANK + 2 * A_ICLR_RANK + A_GATE_RANK

B_Q_HEADS = 16
B_KV_HEADS = 4
B_HEAD_DIM = ROPE_DIM
B_WIDTH = B_Q_HEADS * B_HEAD_DIM
WINDOW = 128
BLOCK = 128
N_B = (B_Q_HEADS + 2 * B_KV_HEADS) * B_HEAD_DIM

C_HEADS = 8
C_Q_RANK = 1024
C_KV_RANK = 512
C_NOPE_DIM = 128
C_ROPE_DIM = ROPE_DIM
C_V_DIM = 128
C_WIDTH = C_HEADS * C_V_DIM
N_C = C_Q_RANK + C_KV_RANK + C_ROPE_DIM

N_BRANCH = 3
BRANCH_WIDTH = 1024
N_G = N_BRANCH * D_MODEL
N_IN = N_A + N_B + N_C + N_G

N_EXPERTS = 16
CAPACITY_FACTOR = 2
D_EXPERT = 512

kernel_name = 'hybrid_rwkv7_swa_mla_ecmoe_dit'


def _split(z, sizes):
    return jnp.split(z, [int(s) for s in np.cumsum(sizes)[:-1]], axis=-1)


def rms_norm(x, g):
    xf = x.astype(jnp.float32)
    y = xf * lax.rsqrt(jnp.mean(xf * xf, axis=-1, keepdims=True) + NORM_EPS)
    return (y * g.astype(jnp.float32)).astype(x.dtype)


def adaln(cond, down, up, bias):
    d = cond.shape[-1]
    m = (jax.nn.silu(cond) @ down) @ up + bias
    m = m.reshape(cond.shape[0], N_MOD, 1, d)
    return tuple(m[:, j] for j in range(N_MOD))


def modulate(h, shift, scale):
    return h * (1 + scale) + shift


def axial_rope_tables(length):
    rows = length // GRID_W
    row = jnp.repeat(jnp.arange(rows), GRID_W).astype(jnp.float32)
    col = (jnp.arange(length) % GRID_W).astype(jnp.float32)
    n_freq = ROPE_DIM // 4
    inv = jnp.power(ROPE_THETA, -jnp.arange(n_freq, dtype=jnp.float32) / n_freq)
    ang_r = row[:, None] * inv[None]
    ang_c = col[:, None] * inv[None]
    return (jnp.cos(ang_r), jnp.sin(ang_r), jnp.cos(ang_c), jnp.sin(ang_c))


def _rotate_half(x, cos, sin):
    cos = cos[None, :, None, :].astype(x.dtype)
    sin = sin[None, :, None, :].astype(x.dtype)
    x1, x2 = jnp.split(x, 2, axis=-1)
    return jnp.concatenate([x1 * cos - x2 * sin, x1 * sin + x2 * cos], axis=-1)


def apply_rope2d(x, rope):
    cos_r, sin_r, cos_c, sin_c = rope
    x_row, x_col = jnp.split(x, 2, axis=-1)
    return jnp.concatenate([_rotate_half(x_row, cos_r, sin_r), _rotate_half(x_col, cos_c, sin_c)], axis=-1)


def centred_shift(z, mu_prev, mu_next):
    zp = jnp.pad(z, ((0, 0), (1, 1), (0, 0)))
    return z + mu_prev * (zp[:, :-2] - z) + mu_next * (zp[:, 2:] - z)


def _heads_a(t):
    return t.reshape(t.shape[:-1] + (A_HEADS, A_HEAD_DIM))


def rwkv_features(z, mu, w0, w_up, a0, a_up, g_up, k_k, k_a):
    z = centred_shift(z.astype(jnp.float32), mu[0], mu[1])
    r, k, v, wd, ad, gd = _split(z, (A_WIDTH, A_WIDTH, A_WIDTH, 2 * A_DECAY_RANK, 2 * A_ICLR_RANK, A_GATE_RANK))
    bsz, t = z.shape[:2]
    wd = wd.reshape(bsz, t, 2, A_DECAY_RANK)
    ad = ad.reshape(bsz, t, 2, A_ICLR_RANK)
    w = w0 + jnp.einsum('btnr,nrc->btnc', jnp.tanh(wd), w_up)
    log_w = -jax.nn.softplus(-w.astype(jnp.float32)) - 0.5
    decay = jnp.exp(-jnp.exp(log_w))
    a = jax.nn.sigmoid(a0 + jnp.einsum('btnr,nrc->btnc', ad, a_up)).astype(jnp.float32)
    g = (jax.nn.sigmoid(gd) @ g_up).astype(jnp.float32)
    kk = _heads_a(k * k_k)
    kk = kk / jnp.maximum(jnp.sqrt(jnp.sum(kk * kk, axis=-1, keepdims=True)), 1e-12)
    k_dir = k[:, :, None, :] * (1 + (a - 1) * k_a)
    return (_heads_a(r), _heads_a(v), kk, _heads_a(k_dir), _heads_a(decay), _heads_a(a), g)


def wkv_scan(state0, r, w, k, v, kk, a, reverse):
    def step(s, inp):
        r_t, w_t, k_t, v_t, kk_t, a_t = inp
        sa = jnp.einsum('bhvk,bhk->bhv', s, -kk_t)
        s = s * w_t[:, :, None, :] + sa[..., None] * (kk_t * a_t)[:, :, None, :] + v_t[..., None] * k_t[:, :, None, :]
        return s, jnp.einsum('bhvk,bhk->bhv', s, r_t)
    xs = tuple(jnp.moveaxis(t_, 1, 0) for t_ in (r, w, k, v, kk, a))
    s_fin, y = lax.scan(step, state0, xs, reverse=reverse)
    return s_fin, jnp.moveaxis(y, 0, 1)


def rwkv_output(y, feats, r_k, lnx_w, lnx_b):
    r, v, kk, k_dir, decay, a, g = feats
    bsz, t = y.shape[:2]
    mean = jnp.mean(y, axis=-1, keepdims=True)
    var = jnp.mean(jnp.square(y - mean), axis=-1, keepdims=True)
    yn = ((y - mean) * lax.rsqrt(var + A_LNX_EPS)).reshape(bsz, t, A_WIDTH) * lnx_w + lnx_b
    bonus = jnp.sum(r[:, :, None] * k_dir * r_k, axis=-1, keepdims=True) * v[:, :, None]
    bonus = jnp.sum(bonus, axis=2).reshape(bsz, t, A_WIDTH)
    return (yn + bonus) * g


def rwkv_mixer(zx, zu, mu, w0, w_up, a0, a_up, g_up, k_k, k_a, r_k, lnx_w, lnx_b, with_ctx):
    fx = rwkv_features(zx, mu, w0, w_up, a0, a_up, g_up, k_k, k_a)
    fu = rwkv_features(zu, mu, w0, w_up, a0, a_up, g_up, k_k, k_a)
    s0 = jnp.zeros((zx.shape[0], A_HEADS, A_HEAD_DIM, A_HEAD_DIM), jnp.float32)

    def run(f, d, state, reverse):
        r, v, kk, k_dir, decay, a, _ = f
        return wkv_scan(state, r, decay[:, :, d], k_dir[:, :, d], v, kk, a[:, :, d], reverse)

    s_uf, y_uf = run(fu, 0, s0, False)
    s_ub, y_ub = run(fu, 1, s0, True)
    _, y_xf = run(fx, 0, s_uf, False)
    _, y_xb = run(fx, 1, s_ub, True)
    out_x = rwkv_output(y_xf + y_xb, fx, r_k, lnx_w, lnx_b).astype(zx.dtype)
    out_u = rwkv_output(y_uf + y_ub, fu, r_k, lnx_w, lnx_b).astype(zu.dtype) if with_ctx else None
    return out_x, out_u


def gqa_heads(z, q_norm, k_norm):
    q, k, v = _split(z, (B_Q_HEADS * B_HEAD_DIM, B_KV_HEADS * B_HEAD_DIM, B_KV_HEADS * B_HEAD_DIM))
    bsz, t = z.shape[:2]
    q = rms_norm(q.reshape(bsz, t, B_Q_HEADS, B_HEAD_DIM), q_norm)
    k = rms_norm(k.reshape(bsz, t, B_KV_HEADS, B_HEAD_DIM), k_norm)
    v = v.reshape(bsz, t, B_KV_HEADS, B_HEAD_DIM)
    return q, k, v


def sink_attention(q, k, v, sink, mask):
    s = jnp.einsum('bqhgd,bkhd->bhgqk', q, k).astype(jnp.float32) * (B_HEAD_DIM ** -0.5)
    if mask is not None:
        s = jnp.where(mask, s, NEG_INF)
    s = jnp.concatenate([jnp.broadcast_to(sink, s.shape[:-1] + (1,)), s], axis=-1)
    p = jax.nn.softmax(s, axis=-1)[..., 1:].astype(v.dtype)
    return jnp.einsum('bhgqk,bkhd->bqhgd', p, v)


def window_gqa_mixer(zx, zu, q_norm, k_norm, sink, rope, with_ctx):
    qx, kx, vx = gqa_heads(zx, q_norm, k_norm)
    qu, ku, vu = gqa_heads(zu, q_norm, k_norm)
    qx = apply_rope2d(qx, rope)
    kx = apply_rope2d(kx, rope)
    bsz, seq_len = zx.shape[:2]
    ctx_len = zu.shape[1]
    n_blk = seq_len // BLOCK
    grp = B_Q_HEADS // B_KV_HEADS
    sink_hg = sink.reshape(B_KV_HEADS, grp)[None, :, :, None, None].astype(jnp.float32)

    def bands(t):
        tb = jnp.pad(t, ((0, 0), (BLOCK, BLOCK), (0, 0), (0, 0))).reshape(bsz, n_blk + 2, BLOCK, B_KV_HEADS, B_HEAD_DIM)
        return jnp.moveaxis(jnp.concatenate([tb[:, :-2], tb[:, 1:-1], tb[:, 2:]], axis=2), 1, 0)

    q_blk = jnp.moveaxis(qx.reshape(bsz, n_blk, BLOCK, B_KV_HEADS, grp, B_HEAD_DIM), 1, 0)
    blk = jnp.arange(n_blk)[:, None, None]
    q_pos = blk * BLOCK + jnp.arange(BLOCK)[None, :, None]
    k_pos = (blk - 1) * BLOCK + jnp.arange(3 * BLOCK)[None, None, :]
    band_mask = (jnp.abs(k_pos - q_pos) <= WINDOW) & (k_pos >= 0) & (k_pos < seq_len)
    mask = jnp.concatenate([jnp.ones((n_blk, BLOCK, ctx_len), bool), band_mask], axis=-1)

    def one_block(args):
        qb, kb, vb, mb = args
        return sink_attention(qb, jnp.concatenate([ku, kb], axis=1), jnp.concatenate([vu, vb], axis=1), sink_hg, mb)

    out = lax.map(one_block, (q_blk, bands(kx), bands(vx), mask))
    y_x = jnp.moveaxis(out, 0, 1).reshape(bsz, seq_len, B_WIDTH)
    y_u = None
    if with_ctx:
        y_u = sink_attention(qu.reshape(bsz, ctx_len, B_KV_HEADS, grp, B_HEAD_DIM), ku, vu, sink_hg, None)
        y_u = y_u.reshape(bsz, ctx_len, B_WIDTH)
    return y_x, y_u


def mla_heads(z, q_a_norm, q_up, kv_a_norm, kv_up, nope_norm, rope_norm):
    cq, ckv, k_rope = _split(z, (C_Q_RANK, C_KV_RANK, C_ROPE_DIM))
    bsz, t = z.shape[:2]
    q = (rms_norm(cq, q_a_norm) @ q_up).reshape(bsz, t, C_HEADS, C_NOPE_DIM + C_ROPE_DIM)
    kv = (rms_norm(ckv, kv_a_norm) @ kv_up).reshape(bsz, t, C_HEADS, C_NOPE_DIM + C_V_DIM)
    q_nope, q_rope = q[..., :C_NOPE_DIM], q[..., C_NOPE_DIM:]
    k_nope, v = kv[..., :C_NOPE_DIM], kv[..., C_NOPE_DIM:]
    q_nope = rms_norm(q_nope, nope_norm[0])
    k_nope = rms_norm(k_nope, nope_norm[1])
    q_rope = rms_norm(q_rope, rope_norm[0])
    k_rope = rms_norm(k_rope, rope_norm[1])
    return q_nope, q_rope, k_nope, k_rope, v


def mla_attention(q_nope, q_rope, k_nope, k_rope, v):
    s = jnp.einsum('bqhd,bkhd->bhqk', q_nope, k_nope) + jnp.einsum('bqhr,bkr->bhqk', q_rope, k_rope)
    p = jax.nn.softmax(s.astype(jnp.float32) * ((C_NOPE_DIM + C_ROPE_DIM) ** -0.5), axis=-1).astype(v.dtype)
    return jnp.einsum('bhqk,bkhd->bqhd', p, v)


def mla_mixer(zx, zu, q_a_norm, q_up, kv_a_norm, kv_up, nope_norm, rope_norm, rope, with_ctx):
    qn_x, qr_x, kn_x, kr_x, v_x = mla_heads(zx, q_a_norm, q_up, kv_a_norm, kv_up, nope_norm, rope_norm)
    qn_u, qr_u, kn_u, kr_u, v_u = mla_heads(zu, q_a_norm, q_up, kv_a_norm, kv_up, nope_norm, rope_norm)
    qr_x = apply_rope2d(qr_x, rope)
    kr_x = apply_rope2d(kr_x[:, :, None, :], rope)[:, :, 0]
    bsz, seq_len = zx.shape[:2]
    n_blk = seq_len // BLOCK
    kn_all = jnp.concatenate([kn_u, kn_x], axis=1)
    kr_all = jnp.concatenate([kr_u, kr_x], axis=1)
    v_all = jnp.concatenate([v_u, v_x], axis=1)

    def to_blocks(t):
        return jnp.moveaxis(t.reshape((bsz, n_blk, BLOCK) + t.shape[2:]), 1, 0)

    out = lax.map(lambda qs: mla_attention(qs[0], qs[1], kn_all, kr_all, v_all), (to_blocks(qn_x), to_blocks(qr_x)))
    y_x = jnp.moveaxis(out, 0, 1).reshape(bsz, seq_len, C_WIDTH)
    y_u = mla_attention(qn_u, qr_u, kn_u, kr_u, v_u).reshape(bsz, zu.shape[1], C_WIDTH) if with_ctx else None
    return y_x, y_u


def merge_branches(ys, zg, w_branch, w_out):
    bsz, t = zg.shape[:2]
    gates = jax.nn.sigmoid(zg.reshape(bsz, t, N_BRANCH, -1))
    merged = gates[:, :, 0] * (ys[0] @ w_branch[0])
    for b in range(1, N_BRANCH):
        merged = merged + gates[:, :, b] * (ys[b] @ w_branch[b])
    return merged @ w_out


def expert_choice_ffn(h, w_router, w_gate, w_up, w_down):
    bsz, n, d = h.shape
    cap = (CAPACITY_FACTOR * n) // N_EXPERTS
    aff = jax.nn.softmax(jnp.einsum('bnd,de->ben', h, w_router).astype(jnp.float32), axis=1)
    g, idx = lax.top_k(aff, cap)
    xe = jax.vmap(lambda hb, ib: hb[ib])(h, idx)
    hid = jax.nn.silu(jnp.einsum('becd,edf->becf', xe, w_gate)) * jnp.einsum('becd,edf->becf', xe, w_up)
    ye = jnp.einsum('becf,efd->becd', hid, w_down) * g[..., None].astype(h.dtype)
    return jax.vmap(lambda ib, yb: jnp.zeros((n, d), h.dtype).at[ib.reshape(-1)].add(yb.reshape(-1, d)))(idx, ye)


def setup_inputs(seed: int = 0) -> dict:
    key = jax.random.key(seed)
    ks = iter(list(jax.random.split(key, 40)))
    f32 = jnp.float32
    nl = DEPTH

    def nrm(shape, scale):
        return jax.random.normal(next(ks), shape, f32) * scale

    def gain(shape):
        return 1.0 + nrm(shape, 0.02)

    return {
        'x': nrm((BATCH, SEQ, D_MODEL), 1.0),
        'c': nrm((BATCH, D_MODEL), 1.0),
        'ctx': nrm((BATCH, CTX_LEN, D_MODEL), 1.0),
        'c_ctx': nrm((D_MODEL,), 1.0),
        'norm_mix': gain((nl, D_MODEL)),
        'norm_ffn': gain((nl, D_MODEL)),
        'ada_down': nrm((nl, D_MODEL, ADA_RANK), D_MODEL ** -0.5),
        'ada_up': nrm((nl, ADA_RANK, N_MOD * D_MODEL), 0.3 * ADA_RANK ** -0.5),
        'ada_bias': nrm((nl, N_MOD * D_MODEL), 0.02),
        'w_in': nrm((nl, D_MODEL, N_IN), D_MODEL ** -0.5),
        'rwkv_mu': jax.random.uniform(next(ks), (nl, 2, N_A), f32, 0.0, 0.5),
        'rwkv_w0': jax.random.uniform(next(ks), (nl, 2, A_WIDTH), f32, -4.0, 0.0),
        'rwkv_w_up': nrm((nl, 2, A_DECAY_RANK, A_WIDTH), 0.3 * A_DECAY_RANK ** -0.5),
        'rwkv_a0': nrm((nl, 2, A_WIDTH), 0.1),
        'rwkv_a_up': nrm((nl, 2, A_ICLR_RANK, A_WIDTH), 0.3 * A_ICLR_RANK ** -0.5),
        'rwkv_g_up': nrm((nl, A_GATE_RANK, A_WIDTH), A_GATE_RANK ** -0.5),
        'rwkv_k_k': 1.0 + nrm((nl, A_WIDTH), 0.05),
        'rwkv_k_a': 1.0 + nrm((nl, A_WIDTH), 0.05),
        'rwkv_r_k': nrm((nl, A_HEADS, A_HEAD_DIM), 0.1),
        'rwkv_lnx_w': gain((nl, A_WIDTH)),
        'rwkv_lnx_b': nrm((nl, A_WIDTH), 0.02),
        'gqa_q_norm': gain((nl, B_HEAD_DIM)),
        'gqa_k_norm': gain((nl, B_HEAD_DIM)),
        'gqa_sink': nrm((nl, B_Q_HEADS), 0.5),
        'mla_q_a_norm': gain((nl, C_Q_RANK)),
        'mla_q_up': nrm((nl, C_Q_RANK, C_HEADS * (C_NOPE_DIM + C_ROPE_DIM)), C_Q_RANK ** -0.5),
        'mla_kv_a_norm': gain((nl, C_KV_RANK)),
        'mla_kv_up': nrm((nl, C_KV_RANK, C_HEADS * (C_NOPE_DIM + C_V_DIM)), C_KV_RANK ** -0.5),
        'mla_nope_norm': gain((nl, 2, C_NOPE_DIM)),
        'mla_rope_norm': gain((nl, 2, C_ROPE_DIM)),
        'w_branch': nrm((nl, N_BRANCH, BRANCH_WIDTH, D_MODEL), BRANCH_WIDTH ** -0.5),
        'w_out': nrm((nl, D_MODEL, D_MODEL), D_MODEL ** -0.5),
        'moe_router': nrm((nl, D_MODEL, N_EXPERTS), D_MODEL ** -0.5),
        'moe_w_gate': nrm((nl, N_EXPERTS, D_MODEL, D_EXPERT), D_MODEL ** -0.5),
        'moe_w_up': nrm((nl, N_EXPERTS, D_MODEL, D_EXPERT), D_MODEL ** -0.5),
        'moe_w_down': nrm((nl, N_EXPERTS, D_EXPERT, D_MODEL), D_EXPERT ** -0.5),
    }


def reference(x, c, ctx, c_ctx, norm_mix, norm_ffn, ada_down, ada_up, ada_bias, w_in,
              rwkv_mu, rwkv_w0, rwkv_w_up, rwkv_a0, rwkv_a_up, rwkv_g_up, rwkv_k_k, rwkv_k_a,
              rwkv_r_k, rwkv_lnx_w, rwkv_lnx_b, gqa_q_norm, gqa_k_norm, gqa_sink,
              mla_q_a_norm, mla_q_up, mla_kv_a_norm, mla_kv_up, mla_nope_norm, mla_rope_norm,
              w_branch, w_out, moe_router, moe_w_gate, moe_w_up, moe_w_down):
    rope = axial_rope_tables(x.shape[1])
    u = ctx
    c_u = jnp.broadcast_to(c_ctx[None, :], c.shape)
    for i in range(DEPTH):
        with_ctx = i < DEPTH - 1
        mod_x = adaln(c, ada_down[i], ada_up[i], ada_bias[i])
        mod_u = adaln(c_u, ada_down[i], ada_up[i], ada_bias[i])
        zx = modulate(rms_norm(x, norm_mix[i]), mod_x[0], mod_x[1]) @ w_in[i]
        zu = modulate(rms_norm(u, norm_mix[i]), mod_u[0], mod_u[1]) @ w_in[i]
        ax, bx, cx, gx = _split(zx, (N_A, N_B, N_C, N_G))
        au, bu, cu, gu = _split(zu, (N_A, N_B, N_C, N_G))
        ya_x, ya_u = rwkv_mixer(ax, au, rwkv_mu[i], rwkv_w0[i], rwkv_w_up[i], rwkv_a0[i], rwkv_a_up[i],
                                rwkv_g_up[i], rwkv_k_k[i], rwkv_k_a[i], rwkv_r_k[i], rwkv_lnx_w[i],
                                rwkv_lnx_b[i], with_ctx)
        yb_x, yb_u = window_gqa_mixer(bx, bu, gqa_q_norm[i], gqa_k_norm[i], gqa_sink[i], rope, with_ctx)
        yc_x, yc_u = mla_mixer(cx, cu, mla_q_a_norm[i], mla_q_up[i], mla_kv_a_norm[i], mla_kv_up[i],
                               mla_nope_norm[i], mla_rope_norm[i], rope, with_ctx)
        x = x + mod_x[2] * merge_branches((ya_x, yb_x, yc_x), gx, w_branch[i], w_out[i])
        hx = modulate(rms_norm(x, norm_ffn[i]), mod_x[3], mod_x[4])
        x = x + mod_x[5] * expert_choice_ffn(hx, moe_router[i], moe_w_gate[i], moe_w_up[i], moe_w_down[i])
        if with_ctx:
            u = u + mod_u[2] * merge_branches((ya_u, yb_u, yc_u), gu, w_branch[i], w_out[i])
            hu = modulate(rms_norm(u, norm_ffn[i]), mod_u[3], mod_u[4])
            u = u + mod_u[5] * expert_choice_ffn(hu, moe_router[i], moe_w_gate[i], moe_w_up[i], moe_w_down[i])
    return x
```

```python
import functools
import math

import jax
import jax.numpy as jnp
import numpy as np
from jax import lax
from jax.experimental import pallas as pl
from jax.experimental.pallas import tpu as pltpu

F32 = jnp.float32
BF16 = jnp.bfloat16
I32 = jnp.int32

GRID_W = 64
ROPE_DIM = 64
ROPE_THETA = 10000.0
NORM_EPS = 1e-6
NEG_INF = -1e30
N_MOD = 6
A_HEADS = 16
A_HEAD_DIM = 64
A_LNX_EPS = 64e-5
B_Q_HEADS = 16
B_KV_HEADS = 4
B_HEAD_DIM = 64
WINDOW = 128
BLOCK = 128
C_HEADS = 8
C_NOPE_DIM = 128
C_ROPE_DIM = 64
C_V_DIM = 128
N_BRANCH = 3
CAPACITY_FACTOR = 2

LANES = 128
CHUNK = 64
SEQ_GROUP = 4
ROW_TILE = 256
MOE_TILE = 128
MOE_ALIGN = 16
MOE_WIN = MOE_TILE + MOE_ALIGN
SEL_ROWS = 128
RWKV_PASSES = 3
VMEM_MB = 56


def _cp(n_grid, vmem_mb=VMEM_MB):
    return pltpu.CompilerParams(dimension_semantics=("arbitrary",) * n_grid,
                                vmem_limit_bytes=vmem_mb * 1024 * 1024)


def _pick(n, cands):
    for c in cands:
        if n % c == 0:
            return c
    raise ValueError(f"no tile for {n} in {cands}")


def _dg(a, b, ca=1, cb=0):
    return lax.dot_general(a, b, (((ca,), (cb,)), ((), ())), preferred_element_type=F32)


def _split2(x):
    hi = x.astype(BF16)
    lo = (x - hi.astype(F32)).astype(BF16)
    return hi, lo


def _mm(a, b, passes=1, nt=False):
    cb = 1 if nt else 0
    if passes == 1:
        return _dg(a.astype(BF16), b.astype(BF16), 1, cb)
    ah, al = _split2(a)
    bh, bl = _split2(b)
    return _dg(ah, bh, 1, cb) + (_dg(ah, bl, 1, cb) + _dg(al, bh, 1, cb))


def _mm_exact_lhs(m_bf16, x):
    x1 = x.astype(BF16)
    r1 = x - x1.astype(F32)
    x2 = r1.astype(BF16)
    x3 = (r1 - x2.astype(F32)).astype(BF16)
    return _dg(m_bf16, x1) + (_dg(m_bf16, x2) + _dg(m_bf16, x3))


def _mm_exact_rhs(x, m_bf16):
    x1 = x.astype(BF16)
    r1 = x - x1.astype(F32)
    x2 = r1.astype(BF16)
    x3 = (r1 - x2.astype(F32)).astype(BF16)
    return _dg(x1, m_bf16) + (_dg(x2, m_bf16) + _dg(x3, m_bf16))


def _sigmoid(x):
    return 1.0 / (1.0 + jnp.exp(-x))


def _iota(shape, dim):
    return lax.broadcasted_iota(I32, shape, dim)


def _block_ones(n, blk):
    i = _iota((n, n), 0) // blk
    j = _iota((n, n), 1) // blk
    return jnp.where(i == j, 1.0, 0.0).astype(BF16)


def _seg_sum(x, ones_bf16):
    return _mm_exact_rhs(x, ones_bf16)


def _swap16(x):
    n = x.shape[-1]
    lane = _iota(x.shape, x.ndim - 1)
    fwd = pltpu.roll(x, n - 16, x.ndim - 1)
    bwd = pltpu.roll(x, 16, x.ndim - 1)
    return jnp.where((lane % 32) < 16, fwd, bwd)


def _adaln_kernel(c_ref, dn_ref, up_ref, b_ref, o_ref):
    c = c_ref[...]
    t = _mm(c * _sigmoid(c), dn_ref[...], 3)
    o_ref[...] = _mm(t, up_ref[...], 3) + b_ref[...]


def _adaln(cond8, down, up, bias):
    d, r = down.shape
    n = up.shape[1]
    tn = _pick(n, (4096, 2048, 1024, 512, 256, 128))
    return pl.pallas_call(
        _adaln_kernel,
        grid=(n // tn,),
        in_specs=[pl.BlockSpec((8, d), lambda j: (0, 0)),
                  pl.BlockSpec((d, r), lambda j: (0, 0)),
                  pl.BlockSpec((r, tn), lambda j: (0, j)),
                  pl.BlockSpec((1, tn), lambda j: (0, j))],
        out_specs=pl.BlockSpec((8, tn), lambda j: (0, j)),
        out_shape=jax.ShapeDtypeStruct((8, n), F32),
        compiler_params=_cp(1),
        name="adaln",
    )(cond8, down, up, bias.reshape(1, n))


def _rows_are_ctx(tile_rows, row0, nu):
    return (row0 + _iota((tile_rows, 1), 0)) < nu


def _norm_mod_kernel(x_ref, g_ref, sh_ref, sc_ref, *rest, nu, tm, router):
    if router:
        wr_ref, h_ref, aff_ref = rest
    else:
        (h_ref,) = rest
    x = x_ref[...]
    is_u = _rows_are_ctx(tm, pl.program_id(0) * tm, nu)
    y = x * lax.rsqrt(jnp.mean(x * x, axis=-1, keepdims=True) + NORM_EPS) * g_ref[...]
    sh = jnp.where(is_u, sh_ref[0:1, :], sh_ref[1:2, :])
    sc = jnp.where(is_u, sc_ref[0:1, :], sc_ref[1:2, :])
    h = y * (1.0 + sc) + sh
    h_ref[...] = h.astype(h_ref.dtype)
    if router:
        logits = _mm(wr_ref[...], h, 3, nt=True)
        m = jnp.max(logits, axis=0, keepdims=True)
        p = jnp.exp(logits - m)
        aff_ref[...] = p / jnp.sum(p, axis=0, keepdims=True)


def _norm_mod(x_all, gain, shift2, scale2, nu, router_t=None):
    t, d = x_all.shape
    tm = ROW_TILE
    router = router_t is not None
    in_specs = [pl.BlockSpec((tm, d), lambda i: (i, 0)),
                pl.BlockSpec((1, d), lambda i: (0, 0)),
                pl.BlockSpec((2, d), lambda i: (0, 0)),
                pl.BlockSpec((2, d), lambda i: (0, 0))]
    args = [x_all, gain.reshape(1, d), shift2, scale2]
    out_specs = pl.BlockSpec((tm, d), lambda i: (i, 0))
    out_shape = jax.ShapeDtypeStruct((t, d), BF16)
    if router:
        e = router_t.shape[0]
        in_specs.append(pl.BlockSpec((e, d), lambda i: (0, 0)))
        args.append(router_t)
        out_specs = (out_specs, pl.BlockSpec((e, tm), lambda i: (0, i)))
        out_shape = (out_shape, jax.ShapeDtypeStruct((e, t), F32))
    return pl.pallas_call(
        functools.partial(_norm_mod_kernel, nu=nu, tm=tm, router=router),
        grid=(t // tm,), in_specs=in_specs, out_specs=out_specs, out_shape=out_shape,
        compiler_params=_cp(1), name="norm_mod_router" if router else "norm_mod",
    )(*args)


def _matmul_kernel(*refs, nu, tm, act, has_rms, has_resid):
    it = iter(refs)
    a_ref = next(it)
    w_ref = next(it)
    g_ref = next(it) if has_rms else None
    x_ref = next(it) if has_resid else None
    gate_ref = next(it) if has_resid else None
    o_ref = next(it)
    a = a_ref[...]
    if has_rms:
        af = a.astype(F32)
        a = af * lax.rsqrt(jnp.mean(af * af, axis=-1, keepdims=True) + NORM_EPS) * g_ref[...]
    acc = _dg(a.astype(BF16), w_ref[...])
    if act == "sigmoid":
        acc = _sigmoid(acc)
    if has_resid:
        is_u = _rows_are_ctx(tm, pl.program_id(1) * tm, nu)
        gate = jnp.where(is_u, gate_ref[0:1, :], gate_ref[1:2, :])
        acc = x_ref[...] + gate * acc
    o_ref[...] = acc.astype(o_ref.dtype)


def _matmul(a, w, out_dtype, *, act=None, rms_gain=None, resid=None, gate2=None, nu=0, name="matmul"):
    m, k = a.shape
    n = w.shape[1]
    tm = _pick(m, (640, 256, 128))
    tn = n if n <= 2048 else _pick(n, (1024, 768, 512, 384, 256, 128))
    has_rms = rms_gain is not None
    has_resid = resid is not None
    in_specs = [pl.BlockSpec((tm, k), lambda j, i: (i, 0)),
                pl.BlockSpec((k, tn), lambda j, i: (0, j))]
    args = [a, w]
    if has_rms:
        in_specs.append(pl.BlockSpec((1, k), lambda j, i: (0, 0)))
        args.append(rms_gain.reshape(1, k))
    if has_resid:
        in_specs += [pl.BlockSpec((tm, tn), lambda j, i: (i, j)),
                     pl.BlockSpec((2, tn), lambda j, i: (0, j))]
        args += [resid, gate2]
    return pl.pallas_call(
        functools.partial(_matmul_kernel, nu=nu, tm=tm, act=act, has_rms=has_rms, has_resid=has_resid),
        grid=(n // tn, m // tm), in_specs=in_specs,
        out_specs=pl.BlockSpec((tm, tn), lambda j, i: (i, j)),
        out_shape=jax.ShapeDtypeStruct((m, n), out_dtype),
        compiler_params=_cp(2), name=name,
    )(*args)


def _rwkv_feat_kernel(z_ref, zp_ref, zn_ref, mu_ref, w0_ref, wup_ref, a0_ref, aup_ref, gup_ref,
                      r_ref, k_ref, v_ref, e_ref, a_ref, g_ref, *, nu, t_all, tm, aw, dr):
    z = z_ref[...]
    row = pl.program_id(0) * tm + _iota((tm, 1), 0)
    ri = _iota((tm, 1), 0)
    up1 = pltpu.roll(z, 1, 0)
    dn1 = pltpu.roll(z, tm - 1, 0)
    zp = jnp.where(ri == 0, zp_ref[7:8, :], up1)
    zn = jnp.where(ri == tm - 1, zn_ref[0:1, :], dn1)
    zp = jnp.where((row == 0) | (row == nu), 0.0, zp)
    zn = jnp.where((row == nu - 1) | (row == t_all - 1), 0.0, zn)
    zs = z + mu_ref[0:1, :] * (zp - z) + mu_ref[1:2, :] * (zn - z)
    r_ref[...] = zs[:, 0:aw]
    k_ref[...] = zs[:, aw:2 * aw]
    v_ref[...] = zs[:, 2 * aw:3 * aw]
    o = 3 * aw
    for n in range(2):
        wd = jnp.tanh(zs[:, o + n * dr:o + (n + 1) * dr])
        w = w0_ref[n:n + 1, :] + _mm(wd, wup_ref[n], 3)
        sp = jnp.maximum(-w, 0.0) + jnp.log(1.0 + jnp.exp(-jnp.abs(w)))
        e_ref[n] = jnp.exp(-sp - 0.5)
    o += 2 * dr
    for n in range(2):
        ad = zs[:, o + n * dr:o + (n + 1) * dr]
        a_ref[n] = _sigmoid(a0_ref[n:n + 1, :] + _mm(ad, aup_ref[n], 3))
    o += 2 * dr
    g_ref[...] = _mm(_sigmoid(zs[:, o:]), gup_ref[...], 3)


def _rwkv_feat(za, mu_p, w0, w_up, a0, a_up, g_up_p, nu):
    t, na = za.shape
    aw = w0.shape[1]
    dr = w_up.shape[1]
    tm = ROW_TILE
    nb8 = tm // 8
    last8 = t // 8 - 1
    row_spec = lambda w: pl.BlockSpec((tm, w), lambda i: (i, 0))
    full = lambda a: pl.BlockSpec(a.shape, lambda i: (0,) * a.ndim)
    out_rows = jax.ShapeDtypeStruct((t, aw), F32)
    out_dir = jax.ShapeDtypeStruct((2, t, aw), F32)
    dir_spec = pl.BlockSpec((2, tm, aw), lambda i: (0, i, 0))
    return pl.pallas_call(
        functools.partial(_rwkv_feat_kernel, nu=nu, t_all=t, tm=tm, aw=aw, dr=dr),
        grid=(t // tm,),
        in_specs=[row_spec(na),
                  pl.BlockSpec((8, na), lambda i: (jnp.maximum(i * nb8 - 1, 0), 0)),
                  pl.BlockSpec((8, na), lambda i: (jnp.minimum((i + 1) * nb8, last8), 0)),
                  full(mu_p), full(w0), full(w_up), full(a0), full(a_up), full(g_up_p)],
        out_specs=(row_spec(aw), row_spec(aw), row_spec(aw), dir_spec, dir_spec, row_spec(aw)),
        out_shape=(out_rows, out_rows, out_rows, out_dir, out_dir, out_rows),
        compiler_params=_cp(1), name="rwkv_feat",
    )(za, za, za, mu_p, w0, w_up, a0, a_up, g_up_p)


def _stack_pair(x, m_a, m_b):
    return jnp.concatenate([x * m_a, x * m_b], axis=0)


def _compact_pair(x):
    half = x.shape[0] // 2
    return x[:half] + x[half:]


def _rwkv_chunk_math(r, k, v, e, a, kk_gain, ka_gain, reverse, passes):
    c = r.shape[0]
    hd = A_HEAD_DIM
    mm = functools.partial(_mm, passes=passes)
    lane = _iota((1, LANES), 1)
    m_a = jnp.where(lane < hd, 1.0, 0.0)
    m_b = 1.0 - m_a
    ones_seg = _block_ones(LANES, hd)
    ti = _iota((c, c), 0)
    tj = _iota((c, c), 1)
    tri = jnp.where((tj >= ti) if reverse else (tj <= ti), 1.0, 0.0).astype(BF16)

    kk0 = k * kk_gain
    kk = kk0 / jnp.maximum(jnp.sqrt(_seg_sum(kk0 * kk0, ones_seg)), 1e-12)
    kt = k * (1.0 + (a - 1.0) * ka_gain)
    b = kk * a
    cl = _mm_exact_lhs(tri, e)
    last = 0 if reverse else c - 1
    ctot = cl[last:last + 1, :]
    g_in = jnp.exp(-cl)
    g_ex = jnp.exp(e - cl)
    g_inv = jnp.exp(cl)
    g_end = jnp.exp(cl - ctot)
    st = lambda x: _stack_pair(x, m_a, m_b)
    kk2 = st(kk * g_ex)
    r2 = st(r * g_in)
    b2 = st(b * g_inv)
    k2 = st(kt * g_inv)
    v2 = st(v)
    bg2 = st(b * g_end)
    kg2 = st(kt * g_end)

    s = mm(jnp.concatenate([kk2, r2], axis=0), jnp.concatenate([b2, k2], axis=0), nt=True)
    n2 = 2 * c
    i2 = _iota((n2, n2), 0)
    j2 = _iota((n2, n2), 1)
    il = i2 % c
    jl = j2 % c
    strict = (jl > il) if reverse else (jl < il)
    incl = (jl >= il) if reverse else (jl <= il)
    a_b = jnp.where(strict, s[:n2, :n2], 0.0)
    a_k = jnp.where(strict, s[:n2, n2:], 0.0)
    l_b = jnp.where(incl, s[n2:, :n2], 0.0)
    l_k = jnp.where(incl, s[n2:, n2:], 0.0)
    eye = jnp.where(i2 == j2, 1.0, 0.0)

    same = lambda m: (i2 // m) == (j2 // m)
    a_d = jnp.where(same(8), a_b, 0.0)
    a_d2 = mm(a_d, a_d)
    a_d4 = mm(a_d2, a_d2)
    t_inv = mm(mm(eye - a_d, eye + a_d2), eye + a_d4)
    m = 8
    while m < c:
        a_off = jnp.where(same(2 * m) & jnp.logical_not(same(m)), a_b, 0.0)
        t_inv = t_inv - mm(mm(t_inv, a_off), t_inv)
        m *= 2

    av = mm(a_k, v2)
    z12 = mm(t_inv, jnp.concatenate([kk2, av], axis=1))
    z1 = z12[:, :LANES]
    z2 = z12[:, LANES:]
    vz = jnp.concatenate([v2, z2], axis=0)
    p_bd = eye * jnp.exp(-ctot) - mm(bg2.T, z1)
    q_bd = mm(jnp.concatenate([kg2, -bg2], axis=0).T, vz)
    r2s = r2 - mm(l_b, z1)
    y0s = mm(jnp.concatenate([l_k, -l_b], axis=1), vz)
    return _compact_pair(p_bd), _compact_pair(q_bd), _compact_pair(r2s), _compact_pair(y0s)


def _rwkv_chunk_kernel(r_ref, k_ref, v_ref, e_ref, a_ref, kkg_ref, kag_ref,
                       p_ref, q_ref, r2_ref, y0_ref, *, reverse, groups):
    c = CHUNK
    for g in range(groups):
        sl = slice(g * c, (g + 1) * c)
        p, q, r2, y0 = _rwkv_chunk_math(r_ref[sl, :], k_ref[sl, :], v_ref[sl, :], e_ref[0, sl, :], a_ref[0, sl, :],
                                        kkg_ref[...], kag_ref[...], reverse, RWKV_PASSES)
        p_ref[0, g] = p
        q_ref[0, g] = q
        r2_ref[sl, :] = r2
        y0_ref[sl, :] = y0


def _rwkv_chunks(r, k, v, e, a, k_k, k_a, direction):
    t, aw = r.shape
    npair = aw // LANES
    groups = SEQ_GROUP
    rows = groups * CHUNK
    nc = t // CHUNK
    row_spec = pl.BlockSpec((rows, LANES), lambda p, i: (i, p))
    dir_spec = pl.BlockSpec((1, rows, LANES), lambda p, i: (direction, i, p))
    par_spec = pl.BlockSpec((1, LANES), lambda p, i: (0, p))
    pq_spec = pl.BlockSpec((1, groups, A_HEAD_DIM, LANES), lambda p, i: (p, i, 0, 0))
    pq_shape = jax.ShapeDtypeStruct((npair, nc, A_HEAD_DIM, LANES), F32)
    ry_shape = jax.ShapeDtypeStruct((t, aw), F32)
    return pl.pallas_call(
        functools.partial(_rwkv_chunk_kernel, reverse=bool(direction), groups=groups),
        grid=(npair, t // rows),
        in_specs=[row_spec, row_spec, row_spec, dir_spec, dir_spec, par_spec, par_spec],
        out_specs=(pq_spec, pq_spec, row_spec, row_spec),
        out_shape=(pq_shape, pq_shape, ry_shape, ry_shape),
        compiler_params=_cp(2), name="rwkv_chunks_bwd" if direction else "rwkv_chunks_fwd",
    )(r, k, v, e, a, k_k.reshape(1, aw), k_a.reshape(1, aw))


def _pair_block_diag(x):
    lane = _iota(x.shape, 1)
    return jnp.concatenate([jnp.where(lane < A_HEAD_DIM, x, 0.0), jnp.where(lane >= A_HEAD_DIM, x, 0.0)], axis=0)


def _rwkv_seq_kernel(p_ref, q_ref, r2_ref, y0_ref, y_ref, h_ref, *, reverse, groups, npair):
    c = CHUNK

    @pl.when(pl.program_id(0) == 0)
    def _():
        h_ref[...] = jnp.zeros_like(h_ref)

    order = range(groups - 1, -1, -1) if reverse else range(groups)
    for g in order:
        sl = slice(g * c, (g + 1) * c)
        for p in range(npair):
            ls = slice(p * LANES, (p + 1) * LANES)
            h = h_ref[p]
            y_ref[sl, ls] = _mm(r2_ref[sl, ls], h, RWKV_PASSES) + y0_ref[sl, ls]
            h_ref[p] = _mm(_pair_block_diag(p_ref[p, g]), h, RWKV_PASSES) + _pair_block_diag(q_ref[p, g])


def _rwkv_seq(p, q, r2, y0, direction, nu):
    npair, nc = p.shape[:2]
    t, aw = r2.shape
    groups = SEQ_GROUP
    rows = groups * CHUNK
    nb = t // rows
    nbu = nu // rows
    assert nu % rows == 0
    if direction:
        blk = lambda s: jnp.where(s < nbu, nbu - 1 - s, nb + nbu - 1 - s)
    else:
        blk = lambda s: s
    pq_spec = pl.BlockSpec((npair, groups, A_HEAD_DIM, LANES), lambda s: (0, blk(s), 0, 0))
    row_spec = pl.BlockSpec((rows, aw), lambda s: (blk(s), 0))
    return pl.pallas_call(
        functools.partial(_rwkv_seq_kernel, reverse=bool(direction), groups=groups, npair=npair),
        grid=(nb,),
        in_specs=[pq_spec, pq_spec, row_spec, row_spec],
        out_specs=row_spec,
        out_shape=jax.ShapeDtypeStruct((t, aw), F32),
        scratch_shapes=[pltpu.VMEM((npair, LANES, LANES), F32)],
        compiler_params=_cp(1), name="rwkv_seq_bwd" if direction else "rwkv_seq_fwd",
    )(p, q, r2, y0)


def _rwkv_out_kernel(yf_ref, yb_ref, r_ref, k_ref, v_ref, a_ref, g_ref, lw_ref, lb_ref, rk_ref, ka_ref, o_ref, *, aw):
    hd = A_HEAD_DIM
    ones_seg = _block_ones(LANES, hd)
    for p in range(aw // LANES):
        ls = slice(p * LANES, (p + 1) * LANES)
        y = yf_ref[:, ls] + yb_ref[:, ls]
        mean = _seg_sum(y, ones_seg) * (1.0 / hd)
        yc = y - mean
        var = _seg_sum(yc * yc, ones_seg) * (1.0 / hd)
        yn = yc * lax.rsqrt(var + A_LNX_EPS) * lw_ref[:, ls] + lb_ref[:, ls]
        r = r_ref[:, ls]
        k = k_ref[:, ls]
        v = v_ref[:, ls]
        bonus = jnp.zeros_like(y)
        for n in range(2):
            kt = k * (1.0 + (a_ref[n, :, ls] - 1.0) * ka_ref[:, ls])
            bonus = bonus + _seg_sum(r * kt * rk_ref[:, ls], ones_seg) * v
        o_ref[:, ls] = ((yn + bonus) * g_ref[:, ls]).astype(o_ref.dtype)


def _rwkv_out(yf, yb, r, k, v, a, g, lnx_w, lnx_b, r_k, k_a):
    t, aw = r.shape
    tm = ROW_TILE
    row_spec = pl.BlockSpec((tm, aw), lambda i: (i, 0))
    par_spec = pl.BlockSpec((1, aw), lambda i: (0, 0))
    return pl.pallas_call(
        functools.partial(_rwkv_out_kernel, aw=aw),
        grid=(t // tm,),
        in_specs=[row_spec] * 5 + [pl.BlockSpec((2, tm, aw), lambda i: (0, i, 0)), row_spec] + [par_spec] * 4,
        out_specs=row_spec,
        out_shape=jax.ShapeDtypeStruct((t, aw), BF16),
        compiler_params=_cp(1), name="rwkv_out",
    )(yf, yb, r, k, v, a, g, lnx_w.reshape(1, aw), lnx_b.reshape(1, aw), r_k.reshape(1, aw), k_a.reshape(1, aw))


def _rms_rope_slab(x, gain, cos, sin, ones_seg):
    ms = _seg_sum(x * x, ones_seg) * (1.0 / B_HEAD_DIM)
    y = x * lax.rsqrt(ms + NORM_EPS) * gain
    return y * cos + _swap16(y) * sin


def _gqa_prep_kernel(z_ref, cos_ref, sin_ref, qg_ref, kg_ref, o_ref, *, qw, kw):
    ones_seg = _block_ones(LANES, B_HEAD_DIM)
    cos = cos_ref[...]
    sin = sin_ref[...]
    scale = B_HEAD_DIM ** -0.5
    for s in range(qw // LANES):
        ls = slice(s * LANES, (s + 1) * LANES)
        o_ref[:, ls] = (_rms_rope_slab(z_ref[:, ls], qg_ref[...], cos, sin, ones_seg) * scale).astype(o_ref.dtype)
    for s in range(kw // LANES):
        ls = slice(qw + s * LANES, qw + (s + 1) * LANES)
        o_ref[:, ls] = _rms_rope_slab(z_ref[:, ls], kg_ref[...], cos, sin, ones_seg).astype(o_ref.dtype)
    o_ref[:, qw + kw:] = z_ref[:, qw + kw:].astype(o_ref.dtype)


def _gqa_prep(zb, cos, sin, q_norm, k_norm):
    t, nb = zb.shape
    tm = ROW_TILE
    qw = B_Q_HEADS * B_HEAD_DIM
    kw = B_KV_HEADS * B_HEAD_DIM
    tile2 = lambda g: jnp.tile(g.reshape(1, B_HEAD_DIM), (1, LANES // B_HEAD_DIM))
    return pl.pallas_call(
        functools.partial(_gqa_prep_kernel, qw=qw, kw=kw),
        grid=(t // tm,),
        in_specs=[pl.BlockSpec((tm, nb), lambda i: (i, 0)),
                  pl.BlockSpec((tm, LANES), lambda i: (i, 0)),
                  pl.BlockSpec((tm, LANES), lambda i: (i, 0)),
                  pl.BlockSpec((1, LANES), lambda i: (0, 0)),
                  pl.BlockSpec((1, LANES), lambda i: (0, 0))],
        out_specs=pl.BlockSpec((tm, nb), lambda i: (i, 0)),
        out_shape=jax.ShapeDtypeStruct((t, nb), BF16),
        compiler_params=_cp(1), name="gqa_prep",
    )(zb, cos, sin, tile2(q_norm), tile2(k_norm))


def _gqa_attn_kernel(sink_ref, q_ref, c_ref, kp_ref, ko_ref, kn_ref, o_ref, *, nu, seq, qw, kw):
    hd = B_HEAD_DIM
    grp = B_Q_HEADS // B_KV_HEADS
    blk = BLOCK
    j = pl.program_id(0)
    jb = j - nu // blk
    nkeys = nu + 3 * blk
    rows = grp * blk
    qi = _iota((rows, nkeys), 0) % blk
    kc = _iota((rows, nkeys), 1)
    q_pos = jb * blk + qi
    k_pos = (jb - 1) * blk + (kc - nu)
    band_ok = (jnp.abs(k_pos - q_pos) <= WINDOW) & (k_pos >= 0) & (k_pos < seq) & (jb >= 0)
    valid = (kc < nu) | band_ok
    rg = _iota((rows, 1), 0) // blk
    outs = []
    for h in range(B_KV_HEADS):
        ks = slice(qw + h * hd, qw + (h + 1) * hd)
        vs = slice(qw + kw + h * hd, qw + kw + (h + 1) * hd)
        k_all = jnp.concatenate([c_ref[:, ks], kp_ref[:, ks], ko_ref[:, ks], kn_ref[:, ks]], axis=0)
        v_all = jnp.concatenate([c_ref[:, vs], kp_ref[:, vs], ko_ref[:, vs], kn_ref[:, vs]], axis=0)
        q4 = jnp.concatenate([q_ref[:, (h * grp + g) * hd:(h * grp + g + 1) * hd] for g in range(grp)], axis=0)
        s = jnp.where(valid, _dg(q4, k_all, 1, 1), NEG_INF)
        sink = jnp.zeros((rows, 1), F32)
        for g in range(grp):
            sink = jnp.where(rg == g, sink_ref[h * grp + g], sink)
        m = jnp.maximum(jnp.max(s, axis=1, keepdims=True), sink)
        p = jnp.exp(s - m)
        den = jnp.sum(p, axis=1, keepdims=True) + jnp.exp(sink - m)
        o = _dg(p.astype(BF16), v_all) / den
        outs += [o[g * blk:(g + 1) * blk, :] for g in range(grp)]
    o_ref[...] = jnp.concatenate(outs, axis=1).astype(o_ref.dtype)


def _gqa_attn(qkv, sink, nu, seq):
    t, nb = qkv.shape
    qw = B_Q_HEADS * B_HEAD_DIM
    kw = B_KV_HEADS * B_HEAD_DIM
    blk = BLOCK
    nblk = t // blk
    band = lambda off: pl.BlockSpec((blk, nb), lambda j: (jnp.clip(j + off, 0, nblk - 1), 0))
    return pl.pallas_call(
        functools.partial(_gqa_attn_kernel, nu=nu, seq=seq, qw=qw, kw=kw),
        grid=(nblk,),
        in_specs=[pl.BlockSpec(memory_space=pltpu.SMEM),
                  pl.BlockSpec((blk, nb), lambda j: (j, 0)),
                  pl.BlockSpec((nu, nb), lambda j: (0, 0)),
                  band(-1), band(0), band(1)],
        out_specs=pl.BlockSpec((blk, qw), lambda j: (j, 0)),
        out_shape=jax.ShapeDtypeStruct((t, qw), BF16),
        compiler_params=_cp(1), name="gqa_attn",
    )(sink, qkv, qkv, qkv, qkv, qkv)


def _mla_prep_kernel(q_ref, kv_ref, kr_ref, cos_ref, sin_ref, nn_ref, rn_ref, qo_ref, ko_ref, vo_ref):
    cos = cos_ref[...]
    sin = sin_ref[...]
    dn = C_NOPE_DIM
    hw = 2 * LANES
    scale = (C_NOPE_DIM + C_ROPE_DIM) ** -0.5

    def rms_rope(x, gain):
        ms = jnp.sum(x * x, axis=-1, keepdims=True) * (1.0 / C_ROPE_DIM)
        y = x * lax.rsqrt(ms + NORM_EPS) * gain
        return y * cos + _swap16(y) * sin

    def rms(x, gain):
        return x * lax.rsqrt(jnp.mean(x * x, axis=-1, keepdims=True) + NORM_EPS) * gain

    kr = rms_rope(kr_ref[...], rn_ref[1:2, :]).astype(ko_ref.dtype)
    for h in range(C_HEADS):
        qn = rms(q_ref[:, h * hw:h * hw + dn], nn_ref[0:1, :])
        qr = rms_rope(q_ref[:, h * hw + dn:(h + 1) * hw], rn_ref[0:1, :])
        qo_ref[:, h * hw:h * hw + dn] = (qn * scale).astype(qo_ref.dtype)
        qo_ref[:, h * hw + dn:(h + 1) * hw] = (qr * scale).astype(qo_ref.dtype)
        ko_ref[:, h * hw:h * hw + dn] = rms(kv_ref[:, h * dn:(h + 1) * dn], nn_ref[1:2, :]).astype(ko_ref.dtype)
        ko_ref[:, h * hw + dn:(h + 1) * hw] = kr
    vo_ref[...] = kv_ref[:, C_HEADS * dn:].astype(vo_ref.dtype)


def _mla_prep(q, kv, zc, kr_col_block, cos, sin, nope_norm, rope_norm_p):
    t = q.shape[0]
    tm = ROW_TILE
    hw = 2 * LANES
    row = lambda w: pl.BlockSpec((tm, w), lambda i: (i, 0))
    full = lambda a: pl.BlockSpec(a.shape, lambda i: (0, 0))
    return pl.pallas_call(
        _mla_prep_kernel,
        grid=(t // tm,),
        in_specs=[row(q.shape[1]), row(kv.shape[1]),
                  pl.BlockSpec((tm, LANES), lambda i: (i, kr_col_block)),
                  row(LANES), row(LANES), full(nope_norm), full(rope_norm_p)],
        out_specs=(row(C_HEADS * hw), row(C_HEADS * hw), row(C_HEADS * C_V_DIM)),
        out_shape=(jax.ShapeDtypeStruct((t, C_HEADS * hw), BF16),
                   jax.ShapeDtypeStruct((t, C_HEADS * hw), BF16),
                   jax.ShapeDtypeStruct((t, C_HEADS * C_V_DIM), BF16)),
        compiler_params=_cp(1), name="mla_prep",
    )(q, kv, zc, cos, sin, nope_norm, rope_norm_p)


def _mla_flash_kernel(q_ref, k_ref, v_ref, o_ref, m_ref, l_ref, acc_ref, *, nu, tq, tk):
    qi = pl.program_id(1)
    ki = pl.program_id(2)

    @pl.when(ki == 0)
    def _():
        m_ref[...] = jnp.full_like(m_ref, NEG_INF)
        l_ref[...] = jnp.zeros_like(l_ref)
        acc_ref[...] = jnp.zeros_like(acc_ref)

    def update(masked):
        s = _dg(q_ref[...], k_ref[...], 1, 1)
        if masked:
            qrow = qi * tq + _iota((tq, tk), 0)
            kcol = ki * tk + _iota((tq, tk), 1)
            s = jnp.where((qrow < nu) & (kcol >= nu), NEG_INF, s)
        m_prev = m_ref[...]
        m_next = jnp.maximum(m_prev, jnp.max(s, axis=1, keepdims=True))
        alpha = jnp.exp(m_prev - m_next)
        p = jnp.exp(s - m_next[:, 0:1])
        l_ref[...] = alpha * l_ref[...] + jnp.sum(p, axis=1, keepdims=True)
        acc_ref[...] = alpha * acc_ref[...] + _dg(p.astype(BF16), v_ref[...])
        m_ref[...] = m_next

    has_ctx_rows = qi * tq < nu

    @pl.when(has_ctx_rows)
    def _():
        update(True)

    @pl.when(jnp.logical_not(has_ctx_rows))
    def _():
        update(False)

    @pl.when(ki == pl.num_programs(2) - 1)
    def _():
        o_ref[...] = (acc_ref[...] / l_ref[...]).astype(o_ref.dtype)


def _mla_flash(qf, kf, vf, nu):
    t = qf.shape[0]
    hw = 2 * LANES
    tq = _pick(t, (1280, 256))
    tk = _pick(t, (1280, 256))
    return pl.pallas_call(
        functools.partial(_mla_flash_kernel, nu=nu, tq=tq, tk=tk),
        grid=(C_HEADS, t // tq, t // tk),
        in_specs=[pl.BlockSpec((tq, hw), lambda h, i, j: (i, h)),
                  pl.BlockSpec((tk, hw), lambda h, i, j: (j, h)),
                  pl.BlockSpec((tk, C_V_DIM), lambda h, i, j: (j, h))],
        out_specs=pl.BlockSpec((tq, C_V_DIM), lambda h, i, j: (i, h)),
        out_shape=jax.ShapeDtypeStruct((t, C_HEADS * C_V_DIM), BF16),
        scratch_shapes=[pltpu.VMEM((tq, LANES), F32), pltpu.VMEM((tq, LANES), F32), pltpu.VMEM((tq, C_V_DIM), F32)],
        compiler_params=_cp(3), name="mla_flash",
    )(qf, kf, vf)


def _merge_kernel(ya_ref, yb_ref, yc_ref, w_ref, g0_ref, g1_ref, g2_ref, o_ref):
    acc = g0_ref[...].astype(F32) * _dg(ya_ref[...], w_ref[0])
    acc = acc + g1_ref[...].astype(F32) * _dg(yb_ref[...], w_ref[1])
    acc = acc + g2_ref[...].astype(F32) * _dg(yc_ref[...], w_ref[2])
    o_ref[...] = acc.astype(o_ref.dtype)


def _merge(ya, yb, yc, w_branch, gates):
    t, bw = ya.shape
    d = w_branch.shape[2]
    tm = _pick(t, (640, 256, 128))
    tn = _pick(d, (1024, 512, 256, 128))
    nj = d // tn
    y_spec = pl.BlockSpec((tm, bw), lambda j, i: (i, 0))
    gate = lambda b: pl.BlockSpec((tm, tn), lambda j, i: (i, b * nj + j))
    return pl.pallas_call(
        _merge_kernel,
        grid=(nj, t // tm),
        in_specs=[y_spec, y_spec, y_spec, pl.BlockSpec((N_BRANCH, bw, tn), lambda j, i: (0, 0, j)),
                  gate(0), gate(1), gate(2)],
        out_specs=pl.BlockSpec((tm, tn), lambda j, i: (i, j)),
        out_shape=jax.ShapeDtypeStruct((t, d), BF16),
        compiler_params=_cp(2), name="merge",
    )(ya, yb, yc, w_branch, gates, gates, gates)


def _moe_select_kernel(aff_ref, loc_ref, base_ref, *, cap, n_exp):
    aff = aff_ref[...]
    bits = pltpu.bitcast(aff, I32)
    count = lambda m: jnp.sum(jnp.sum(jnp.where(m, 1.0, 0.0), axis=1, keepdims=True), axis=2, keepdims=True)
    theta = jnp.zeros((n_exp, 1, 1), I32)
    for bit in range(30, -1, -1):
        cand = theta | (1 << bit)
        theta = jnp.where(count(bits >= cand) >= cap, cand, theta)
    gt = bits > theta
    eq = bits == theta
    need = cap - count(gt)
    r = aff.shape[1]
    upper = jnp.where(_iota((LANES, LANES), 0) <= _iota((LANES, LANES), 1), 1.0, 0.0).astype(BF16)
    lower_strict = jnp.where(_iota((r, r), 1) < _iota((r, r), 0), 1.0, 0.0).astype(BF16)

    def prefix(x):
        incl = _dg(x.astype(BF16), upper)
        tot = jnp.broadcast_to(incl[:, LANES - 1:LANES], (r, LANES))
        return incl, _dg(lower_strict, tot.astype(BF16))

    for e in range(n_exp):
        xe = jnp.where(eq[e], 1.0, 0.0)
        incl, base = prefix(xe)
        take = eq[e] & ((incl - xe + base) < need[e])
        sel = jnp.where(gt[e] | take, 1.0, 0.0)
        incl, base = prefix(sel)
        loc_ref[e] = jnp.where(sel > 0.0, incl - sel, -4096.0)
        base_ref[e] = base


def _moe_select(aff3, cap):
    e, r, _ = aff3.shape
    spec = pl.BlockSpec((e, r, LANES), lambda i: (0, 0, 0))
    shape = jax.ShapeDtypeStruct((e, r, LANES), F32)
    return pl.pallas_call(
        functools.partial(_moe_select_kernel, cap=cap, n_exp=e),
        grid=(1,), in_specs=[spec], out_specs=(spec, spec), out_shape=(shape, shape),
        compiler_params=_cp(1), name="moe_select",
    )(aff3)


def _window_start(base_ref, e, i):
    b = base_ref[e, i]
    a = (b // MOE_ALIGN) * MOE_ALIGN
    return pl.multiple_of(a, MOE_ALIGN), b - a


def _moe_gather_kernel(base_ref, loc_ref, aff_ref, h_ref, xe_ref, ge_ref):
    e = pl.program_id(0)
    i = pl.program_id(1)

    @pl.when(i == 0)
    def _():
        xe_ref[...] = jnp.zeros_like(xe_ref)
        ge_ref[...] = jnp.zeros_like(ge_ref)

    a, off = _window_start(base_ref, e, i)
    slot = loc_ref[0, 0] + off.astype(F32)
    onehot = jnp.where(_iota((MOE_WIN, MOE_TILE), 0).astype(F32) == slot, 1.0, 0.0)
    rows = _dg(onehot.astype(BF16), h_ref[...])
    win = pl.ds(a, MOE_WIN)
    xe_ref[0, win, :] = (xe_ref[0, win, :].astype(F32) + rows).astype(xe_ref.dtype)
    gsel = jnp.sum(onehot * aff_ref[0, 0], axis=1, keepdims=True)
    ge_ref[0, win, :] = ge_ref[0, win, :] + gsel


def _moe_gather(base_i, loc, aff3, h_all, row0, n, cap_p):
    e = loc.shape[0]
    d = h_all.shape[1]
    nt = n // MOE_TILE
    t0 = row0 // MOE_TILE
    grid_spec = pltpu.PrefetchScalarGridSpec(
        num_scalar_prefetch=1, grid=(e, nt),
        in_specs=[pl.BlockSpec((1, 1, 1, LANES), lambda ee, i, b: (ee, i, 0, 0)),
                  pl.BlockSpec((1, 1, 1, LANES), lambda ee, i, b: (ee, i, 0, 0)),
                  pl.BlockSpec((MOE_TILE, d), lambda ee, i, b: (t0 + i, 0))],
        out_specs=(pl.BlockSpec((1, cap_p, d), lambda ee, i, b: (ee, 0, 0)),
                   pl.BlockSpec((1, cap_p, LANES), lambda ee, i, b: (ee, 0, 0))))
    return pl.pallas_call(
        _moe_gather_kernel, grid_spec=grid_spec,
        out_shape=(jax.ShapeDtypeStruct((e, cap_p, d), BF16), jax.ShapeDtypeStruct((e, cap_p, LANES), F32)),
        compiler_params=_cp(2), name="moe_gather",
    )(base_i, loc, aff3, h_all)


def _moe_ffn_kernel(x_ref, g_ref, wg_ref, wu_ref, wd_ref, y_ref):
    x = x_ref[0]
    hg = _dg(x, wg_ref[0])
    hid = hg * _sigmoid(hg) * _dg(x, wu_ref[0])
    y = _dg(hid.astype(BF16), wd_ref[0]) * g_ref[0][:, 0:1]
    y_ref[0] = y.astype(y_ref.dtype)


def _moe_ffn(xe, ge, w_gate, w_up, w_down, tc):
    e, cap_p, d = xe.shape
    f = w_gate.shape[2]
    return pl.pallas_call(
        _moe_ffn_kernel,
        grid=(e, cap_p // tc),
        in_specs=[pl.BlockSpec((1, tc, d), lambda ee, i: (ee, i, 0)),
                  pl.BlockSpec((1, tc, LANES), lambda ee, i: (ee, i, 0)),
                  pl.BlockSpec((1, d, f), lambda ee, i: (ee, 0, 0)),
                  pl.BlockSpec((1, d, f), lambda ee, i: (ee, 0, 0)),
                  pl.BlockSpec((1, f, d), lambda ee, i: (ee, 0, 0))],
        out_specs=pl.BlockSpec((1, tc, d), lambda ee, i: (ee, i, 0)),
        out_shape=jax.ShapeDtypeStruct((e, cap_p, d), BF16),
        compiler_params=_cp(2), name="moe_ffn",
    )(xe, ge, w_gate, w_up, w_down)


def _moe_combine_kernel(base_ref, loc_ref, *rest, n_win):
    y_refs = rest[:n_win]
    x_ref, gate_ref, o_ref, acc_ref = rest[n_win:]
    i = pl.program_id(0)
    e = pl.program_id(1)

    @pl.when(e == 0)
    def _():
        acc_ref[...] = jnp.zeros_like(acc_ref)

    _, off = _window_start(base_ref, e, i)
    slot_row = jnp.broadcast_to(loc_ref[0, 0] + off.astype(F32), (MOE_TILE, LANES))
    slot_col = slot_row.T
    slot_col = jnp.concatenate([slot_col, slot_col[:, :MOE_WIN - LANES]], axis=1)
    onehot = jnp.where(_iota((MOE_TILE, MOE_WIN), 1).astype(F32) == slot_col, 1.0, 0.0)
    ywin = jnp.concatenate([y[0] for y in y_refs], axis=0)
    acc_ref[...] += _dg(onehot.astype(BF16), ywin)

    @pl.when(e == pl.num_programs(1) - 1)
    def _():
        o_ref[...] = x_ref[...] + gate_ref[...] * acc_ref[...]


def _moe_combine(base_i, loc, ye, x_all, gate_row, row0, n):
    e, cap_p, d = ye.shape
    nt = n // MOE_TILE
    t0 = row0 // MOE_TILE
    n_win = MOE_WIN // MOE_ALIGN

    def y_spec(kk):
        return pl.BlockSpec((1, MOE_ALIGN, d), lambda i, ee, b: (ee, b[ee, i] // MOE_ALIGN + kk, 0))

    grid_spec = pltpu.PrefetchScalarGridSpec(
        num_scalar_prefetch=1, grid=(nt, e),
        in_specs=[pl.BlockSpec((1, 1, 1, LANES), lambda i, ee, b: (ee, i, 0, 0))]
                 + [y_spec(kk) for kk in range(n_win)]
                 + [pl.BlockSpec((MOE_TILE, d), lambda i, ee, b: (t0 + i, 0)),
                    pl.BlockSpec((1, d), lambda i, ee, b: (0, 0))],
        out_specs=pl.BlockSpec((MOE_TILE, d), lambda i, ee, b: (t0 + i, 0)),
        scratch_shapes=[pltpu.VMEM((MOE_TILE, d), F32)])
    x_index = 2 + n_win
    return pl.pallas_call(
        functools.partial(_moe_combine_kernel, n_win=n_win), grid_spec=grid_spec,
        out_shape=jax.ShapeDtypeStruct(x_all.shape, F32),
        input_output_aliases={x_index: 0},
        compiler_params=_cp(2), name="moe_combine",
    )(base_i, loc, *([ye] * n_win), x_all, gate_row)


def _moe_stream(x_all, h_all, aff, row0, n, gate_row, w_gate, w_up, w_down):
    e = aff.shape[0]
    cap = (CAPACITY_FACTOR * n) // e
    tc = 256 if cap >= 256 else 64
    cap_p = -(-(cap + MOE_WIN) // tc) * tc
    n_sel = SEL_ROWS * LANES
    aff_s = lax.dynamic_slice_in_dim(aff, row0, n, axis=1)
    aff3 = jnp.pad(aff_s, ((0, 0), (0, n_sel - n))).reshape(e, SEL_ROWS, LANES)
    loc, base = _moe_select(aff3, cap)
    base_i = base[:, :, 0].astype(I32)
    loc4 = loc.reshape(e, SEL_ROWS, 1, LANES)
    aff4 = aff3.reshape(e, SEL_ROWS, 1, LANES)
    xe, ge = _moe_gather(base_i, loc4, aff4, h_all, row0, n, cap_p)
    ye = _moe_ffn(xe, ge, w_gate, w_up, w_down, tc)
    return _moe_combine(base_i, loc4, ye, x_all, gate_row, row0, n)


def _rope_tables(nu, seq):
    n_freq = ROPE_DIM // 4
    pos = jnp.arange(seq)
    inv = jnp.power(ROPE_THETA, -jnp.arange(n_freq, dtype=F32) / n_freq)
    ang_r = (pos // GRID_W).astype(F32)[:, None] * inv[None]
    ang_c = (pos % GRID_W).astype(F32)[:, None] * inv[None]
    cos = jnp.concatenate([jnp.cos(ang_r)] * 2 + [jnp.cos(ang_c)] * 2, axis=1)
    sin = jnp.concatenate([-jnp.sin(ang_r), jnp.sin(ang_r), -jnp.sin(ang_c), jnp.sin(ang_c)], axis=1)
    cos = jnp.concatenate([jnp.ones((nu, ROPE_DIM), F32), cos], axis=0)
    sin = jnp.concatenate([jnp.zeros((nu, ROPE_DIM), F32), sin], axis=0)
    rep = LANES // ROPE_DIM
    return jnp.tile(cos, (1, rep)), jnp.tile(sin, (1, rep))


def _pad_cols(w, n):
    return jnp.pad(w, ((0, 0), (0, n - w.shape[1])))


def _round_up(x, m):
    return -(-x // m) * m


def kernel(x, c, ctx, c_ctx, norm_mix, norm_ffn, ada_down, ada_up, ada_bias, w_in, rwkv_mu, rwkv_w0, rwkv_w_up, rwkv_a0, rwkv_a_up, rwkv_g_up, rwkv_k_k, rwkv_k_a, rwkv_r_k, rwkv_lnx_w, rwkv_lnx_b, gqa_q_norm, gqa_k_norm, gqa_sink, mla_q_a_norm, mla_q_up, mla_kv_a_norm, mla_kv_up, mla_nope_norm, mla_rope_norm, w_branch, w_out, moe_router, moe_w_gate, moe_w_up, moe_w_down):
    bsz, seq, d = x.shape
    assert bsz == 1
    nu = ctx.shape[1]
    depth = w_in.shape[0]
    aw = A_HEADS * A_HEAD_DIM
    dr = rwkv_w_up.shape[2]
    gr = rwkv_g_up.shape[1]
    n_a = 3 * aw + 4 * dr + gr
    n_a_p = _round_up(n_a, LANES)
    n_b = (B_Q_HEADS + 2 * B_KV_HEADS) * B_HEAD_DIM
    cq = mla_q_up.shape[1]
    ckv = mla_kv_up.shape[1]
    n_c = cq + ckv + C_ROPE_DIM
    n_c_p = cq + ckv + LANES
    hw = 2 * LANES

    x_all = jnp.concatenate([ctx[0], x[0]], axis=0)
    cond8 = jnp.zeros((8, d), F32).at[0].set(c_ctx).at[1].set(c[0])
    cos, sin = _rope_tables(nu, seq)

    for i in range(depth):
        wi = w_in[i]
        w_a = _pad_cols(wi[:, :n_a], n_a_p).astype(BF16)
        w_b = wi[:, n_a:n_a + n_b].astype(BF16)
        w_c = _pad_cols(wi[:, n_a + n_b:n_a + n_b + n_c], n_c_p).astype(BF16)
        w_g = wi[:, n_a + n_b + n_c:].astype(BF16)
        mu_p = _pad_cols(rwkv_mu[i], n_a_p)
        g_up_p = jnp.pad(rwkv_g_up[i], ((0, n_a_p - n_a), (0, 0)))
        qu = mla_q_up[i].reshape(cq, C_HEADS, C_NOPE_DIM + C_ROPE_DIM)
        qu = jnp.pad(qu, ((0, 0), (0, 0), (0, hw - C_NOPE_DIM - C_ROPE_DIM))).reshape(cq, C_HEADS * hw).astype(BF16)
        kvu = mla_kv_up[i].reshape(ckv, C_HEADS, C_NOPE_DIM + C_V_DIM)
        kvu = jnp.concatenate([kvu[:, :, :C_NOPE_DIM].reshape(ckv, -1), kvu[:, :, C_NOPE_DIM:].reshape(ckv, -1)],
                              axis=1).astype(BF16)
        rope_norm_p = _pad_cols(mla_rope_norm[i], LANES)

        mod = _adaln(cond8, ada_down[i], ada_up[i], ada_bias[i])[:2].reshape(2, N_MOD, d)

        h = _norm_mod(x_all, norm_mix[i], mod[:, 0], mod[:, 1], nu)
        za = _matmul(h, w_a, F32, name="w_in_a")
        zb = _matmul(h, w_b, F32, name="w_in_b")
        zc = _matmul(h, w_c, F32, name="w_in_c")
        gates = _matmul(h, w_g, BF16, act="sigmoid", name="w_in_gate")

        r, k, v, e, a, g = _rwkv_feat(za, mu_p, rwkv_w0[i], rwkv_w_up[i], rwkv_a0[i], rwkv_a_up[i], g_up_p, nu)
        ys = []
        for direction in (0, 1):
            p_, q_, r2, y0 = _rwkv_chunks(r, k, v, e, a, rwkv_k_k[i], rwkv_k_a[i], direction)
            ys.append(_rwkv_seq(p_, q_, r2, y0, direction, nu))
        ya = _rwkv_out(ys[0], ys[1], r, k, v, a, g, rwkv_lnx_w[i], rwkv_lnx_b[i], rwkv_r_k[i], rwkv_k_a[i])

        qkv_b = _gqa_prep(zb, cos, sin, gqa_q_norm[i], gqa_k_norm[i])
        yb = _gqa_attn(qkv_b, gqa_sink[i], nu, seq)

        q_c = _matmul(zc[:, :cq], qu, F32, rms_gain=mla_q_a_norm[i], name="mla_q_up")
        kv_c = _matmul(zc[:, cq:cq + ckv], kvu, F32, rms_gain=mla_kv_a_norm[i], name="mla_kv_up")
        qf, kf, vf = _mla_prep(q_c, kv_c, zc, (cq + ckv) // LANES, cos, sin, mla_nope_norm[i], rope_norm_p)
        yc = _mla_flash(qf, kf, vf, nu)

        merged = _merge(ya, yb, yc, w_branch[i].astype(BF16), gates)
        x_all = _matmul(merged, w_out[i].astype(BF16), F32, resid=x_all, gate2=mod[:, 2], nu=nu, name="w_out")

        hf, aff = _norm_mod(x_all, norm_ffn[i], mod[:, 3], mod[:, 4], nu, router_t=moe_router[i].T)
        wg = moe_w_gate[i].astype(BF16)
        wu = moe_w_up[i].astype(BF16)
        wd = moe_w_down[i].astype(BF16)
        x_all = _moe_stream(x_all, hf, aff, nu, seq, mod[1:2, 5], wg, wu, wd)
        if i < depth - 1:
            x_all = _moe_stream(x_all, hf, aff, 0, nu, mod[0:1, 5], wg, wu, wd)
    return x_all[nu:].reshape(bsz, seq, d)
```

```python
import functools
import math

import jax
import jax.numpy as jnp
import numpy as np
from jax import lax
from jax.experimental import pallas as pl
from jax.experimental.pallas import tpu as pltpu

F32 = jnp.float32
BF16 = jnp.bfloat16
I32 = jnp.int32

GRID_W = 64
ROPE_DIM = 64
ROPE_THETA = 10000.0
NORM_EPS = 1e-6
NEG_INF = -1e30
N_MOD = 6
A_HEADS = 16
A_HEAD_DIM = 64
A_LNX_EPS = 64e-5
B_Q_HEADS = 16
B_KV_HEADS = 4
B_HEAD_DIM = 64
WINDOW = 128
BLOCK = 128
C_HEADS = 8
C_NOPE_DIM = 128
C_ROPE_DIM = 64
C_V_DIM = 128
N_BRANCH = 3
CAPACITY_FACTOR = 2

LANES = 128
CHUNK = 64
SEQ_GROUP = 4
ROW_TILE = 256
MOE_TILE = 128
MOE_ALIGN = 16
MOE_WIN = MOE_TILE + MOE_ALIGN
SEL_ROWS = 128
RWKV_SEQ_PASSES = 3
VMEM_MB = 56


def _cp(n_grid, vmem_mb=VMEM_MB):
    return pltpu.CompilerParams(dimension_semantics=("arbitrary",) * n_grid,
                                vmem_limit_bytes=vmem_mb * 1024 * 1024)


def _pick(n, cands):
    for c in cands:
        if n % c == 0:
            return c
    raise ValueError(f"no tile for {n} in {cands}")


def _dg(a, b, ca=1, cb=0):
    return lax.dot_general(a, b, (((ca,), (cb,)), ((), ())), preferred_element_type=F32)


def _split2(x):
    hi = x.astype(BF16)
    lo = (x - hi.astype(F32)).astype(BF16)
    return hi, lo


def _mm(a, b, passes=1, nt=False):
    cb = 1 if nt else 0
    if passes == 1:
        return _dg(a.astype(BF16), b.astype(BF16), 1, cb)
    ah, al = _split2(a)
    bh, bl = _split2(b)
    return _dg(ah, bh, 1, cb) + (_dg(ah, bl, 1, cb) + _dg(al, bh, 1, cb))


def _mm_exact_rhs(x, m_bf16):
    x1 = x.astype(BF16)
    r1 = x - x1.astype(F32)
    x2 = r1.astype(BF16)
    x3 = (r1 - x2.astype(F32)).astype(BF16)
    return _dg(x1, m_bf16) + (_dg(x2, m_bf16) + _dg(x3, m_bf16))


def _sigmoid(x):
    return 1.0 / (1.0 + jnp.exp(-x))


def _iota(shape, dim):
    return lax.broadcasted_iota(I32, shape, dim)


def _block_ones(n, blk):
    i = _iota((n, n), 0) // blk
    j = _iota((n, n), 1) // blk
    return jnp.where(i == j, 1.0, 0.0).astype(BF16)


def _seg_sum(x, ones_bf16):
    return _mm_exact_rhs(x, ones_bf16)


def _swap16(x):
    n = x.shape[-1]
    lane = _iota(x.shape, x.ndim - 1)
    fwd = pltpu.roll(x, n - 16, x.ndim - 1)
    bwd = pltpu.roll(x, 16, x.ndim - 1)
    return jnp.where((lane % 32) < 16, fwd, bwd)


def _adaln_kernel(c_ref, dn_ref, up_ref, b_ref, o_ref):
    c = c_ref[...]
    t = _mm(c * _sigmoid(c), dn_ref[...], 3)
    o_ref[...] = _mm(t, up_ref[...], 3) + b_ref[...]


def _adaln(cond8, down, up, bias):
    d, r = down.shape
    n = up.shape[1]
    tn = _pick(n, (4096, 2048, 1024, 512, 256, 128))
    return pl.pallas_call(
        _adaln_kernel,
        grid=(n // tn,),
        in_specs=[pl.BlockSpec((8, d), lambda j: (0, 0)),
                  pl.BlockSpec((d, r), lambda j: (0, 0)),
                  pl.BlockSpec((r, tn), lambda j: (0, j)),
                  pl.BlockSpec((1, tn), lambda j: (0, j))],
        out_specs=pl.BlockSpec((8, tn), lambda j: (0, j)),
        out_shape=jax.ShapeDtypeStruct((8, n), F32),
        compiler_params=_cp(1),
        name="adaln",
    )(cond8, down, up, bias.reshape(1, n))


def _rows_are_ctx(tile_rows, row0, nu):
    return (row0 + _iota((tile_rows, 1), 0)) < nu


def _norm_mod_kernel(x_ref, g_ref, sh_ref, sc_ref, *rest, nu, tm, router):
    if router:
        wr_ref, h_ref, aff_ref = rest
    else:
        (h_ref,) = rest
    x = x_ref[...]
    is_u = _rows_are_ctx(tm, pl.program_id(0) * tm, nu)
    y = x * lax.rsqrt(jnp.mean(x * x, axis=-1, keepdims=True) + NORM_EPS) * g_ref[...]
    sh = jnp.where(is_u, sh_ref[0:1, :], sh_ref[1:2, :])
    sc = jnp.where(is_u, sc_ref[0:1, :], sc_ref[1:2, :])
    h = y * (1.0 + sc) + sh
    h_ref[...] = h.astype(h_ref.dtype)
    if router:
        logits = _mm(wr_ref[...], h, 3, nt=True)
        m = jnp.max(logits, axis=0, keepdims=True)
        p = jnp.exp(logits - m)
        aff_ref[...] = p / jnp.sum(p, axis=0, keepdims=True)


def _norm_mod(x_all, gain, shift2, scale2, nu, router_t=None):
    t, d = x_all.shape
    tm = ROW_TILE
    router = router_t is not None
    in_specs = [pl.BlockSpec((tm, d), lambda i: (i, 0)),
                pl.BlockSpec((1, d), lambda i: (0, 0)),
                pl.BlockSpec((2, d), lambda i: (0, 0)),
                pl.BlockSpec((2, d), lambda i: (0, 0))]
    args = [x_all, gain.reshape(1, d), shift2, scale2]
    out_specs = pl.BlockSpec((tm, d), lambda i: (i, 0))
    out_shape = jax.ShapeDtypeStruct((t, d), BF16)
    if router:
        e = router_t.shape[0]
        in_specs.append(pl.BlockSpec((e, d), lambda i: (0, 0)))
        args.append(router_t)
        out_specs = (out_specs, pl.BlockSpec((e, tm), lambda i: (0, i)))
        out_shape = (out_shape, jax.ShapeDtypeStruct((e, t), F32))
    return pl.pallas_call(
        functools.partial(_norm_mod_kernel, nu=nu, tm=tm, router=router),
        grid=(t // tm,), in_specs=in_specs, out_specs=out_specs, out_shape=out_shape,
        compiler_params=_cp(1), name="norm_mod_router" if router else "norm_mod",
    )(*args)


def _matmul_kernel(*refs, nu, tm, act, has_rms, has_resid):
    it = iter(refs)
    a_ref = next(it)
    w_ref = next(it)
    g_ref = next(it) if has_rms else None
    x_ref = next(it) if has_resid else None
    gate_ref = next(it) if has_resid else None
    o_ref = next(it)
    a = a_ref[...]
    if has_rms:
        af = a.astype(F32)
        a = af * lax.rsqrt(jnp.mean(af * af, axis=-1, keepdims=True) + NORM_EPS) * g_ref[...]
    acc = _dg(a.astype(BF16), w_ref[...])
    if act == "sigmoid":
        acc = _sigmoid(acc)
    if has_resid:
        is_u = _rows_are_ctx(tm, pl.program_id(1) * tm, nu)
        gate = jnp.where(is_u, gate_ref[0:1, :], gate_ref[1:2, :])
        acc = x_ref[...] + gate * acc
    o_ref[...] = acc.astype(o_ref.dtype)


def _matmul(a, w, out_dtype, *, act=None, rms_gain=None, resid=None, gate2=None, nu=0, name="matmul"):
    m, k = a.shape
    n = w.shape[1]
    tm = _pick(m, (640, 256, 128))
    tn = n if n <= 2048 else _pick(n, (1024, 768, 512, 384, 256, 128))
    has_rms = rms_gain is not None
    has_resid = resid is not None
    in_specs = [pl.BlockSpec((tm, k), lambda j, i: (i, 0)),
                pl.BlockSpec((k, tn), lambda j, i: (0, j))]
    args = [a, w]
    if has_rms:
        in_specs.append(pl.BlockSpec((1, k), lambda j, i: (0, 0)))
        args.append(rms_gain.reshape(1, k))
    if has_resid:
        in_specs += [pl.BlockSpec((tm, tn), lambda j, i: (i, j)),
                     pl.BlockSpec((2, tn), lambda j, i: (0, j))]
        args += [resid, gate2]
    return pl.pallas_call(
        functools.partial(_matmul_kernel, nu=nu, tm=tm, act=act, has_rms=has_rms, has_resid=has_resid),
        grid=(n // tn, m // tm), in_specs=in_specs,
        out_specs=pl.BlockSpec((tm, tn), lambda j, i: (i, j)),
        out_shape=jax.ShapeDtypeStruct((m, n), out_dtype),
        compiler_params=_cp(2), name=name,
    )(*args)


def _rwkv_feat_kernel(z_ref, zp_ref, zn_ref, mu_ref, w0_ref, wup_ref, a0_ref, aup_ref, gup_ref,
                      r_ref, k_ref, v_ref, e_ref, a_ref, g_ref, *, nu, t_all, tm, aw, dr):
    z = z_ref[...]
    row = pl.program_id(0) * tm + _iota((tm, 1), 0)
    ri = _iota((tm, 1), 0)
    up1 = pltpu.roll(z, 1, 0)
    dn1 = pltpu.roll(z, tm - 1, 0)
    zp = jnp.where(ri == 0, zp_ref[7:8, :], up1)
    zn = jnp.where(ri == tm - 1, zn_ref[0:1, :], dn1)
    zp = jnp.where((row == 0) | (row == nu), 0.0, zp)
    zn = jnp.where((row == nu - 1) | (row == t_all - 1), 0.0, zn)
    zs = z + mu_ref[0:1, :] * (zp - z) + mu_ref[1:2, :] * (zn - z)
    r_ref[...] = zs[:, 0:aw]
    k_ref[...] = zs[:, aw:2 * aw]
    v_ref[...] = zs[:, 2 * aw:3 * aw]
    o = 3 * aw
    for n in range(2):
        wd = jnp.tanh(zs[:, o + n * dr:o + (n + 1) * dr])
        w = w0_ref[n:n + 1, :] + _mm(wd, wup_ref[n], 3)
        sp = jnp.maximum(-w, 0.0) + jnp.log(1.0 + jnp.exp(-jnp.abs(w)))
        e_ref[n] = jnp.exp(-sp - 0.5)
    o += 2 * dr
    for n in range(2):
        ad = zs[:, o + n * dr:o + (n + 1) * dr]
        a_ref[n] = _sigmoid(a0_ref[n:n + 1, :] + _mm(ad, aup_ref[n], 3))
    o += 2 * dr
    g_ref[...] = _mm(_sigmoid(zs[:, o:]), gup_ref[...], 3)


def _rwkv_feat(za, mu_p, w0, w_up, a0, a_up, g_up_p, nu):
    t, na = za.shape
    aw = w0.shape[1]
    dr = w_up.shape[1]
    tm = ROW_TILE
    nb8 = tm // 8
    last8 = t // 8 - 1
    row_spec = lambda w: pl.BlockSpec((tm, w), lambda i: (i, 0))
    full = lambda a: pl.BlockSpec(a.shape, lambda i: (0,) * a.ndim)
    out_rows = jax.ShapeDtypeStruct((t, aw), F32)
    out_dir = jax.ShapeDtypeStruct((2, t, aw), F32)
    dir_spec = pl.BlockSpec((2, tm, aw), lambda i: (0, i, 0))
    return pl.pallas_call(
        functools.partial(_rwkv_feat_kernel, nu=nu, t_all=t, tm=tm, aw=aw, dr=dr),
        grid=(t // tm,),
        in_specs=[row_spec(na),
                  pl.BlockSpec((8, na), lambda i: (jnp.maximum(i * nb8 - 1, 0), 0)),
                  pl.BlockSpec((8, na), lambda i: (jnp.minimum((i + 1) * nb8, last8), 0)),
                  full(mu_p), full(w0), full(w_up), full(a0), full(a_up), full(g_up_p)],
        out_specs=(row_spec(aw), row_spec(aw), row_spec(aw), dir_spec, dir_spec, row_spec(aw)),
        out_shape=(out_rows, out_rows, out_rows, out_dir, out_dir, out_rows),
        compiler_params=_cp(1), name="rwkv_feat",
    )(za, za, za, mu_p, w0, w_up, a0, a_up, g_up_p)


def _bmm(a, b, nt=False):
    cb = 2 if nt else 1
    return lax.dot_general(a.astype(BF16), b.astype(BF16), (((2,), (cb,)), ((0,), (0,))),
                           preferred_element_type=F32)


def _bmm_exact_lhs(m_bf16, x):
    dn = (((2,), (1,)), ((0,), (0,)))
    x1 = x.astype(BF16)
    r1 = x - x1.astype(F32)
    x2 = r1.astype(BF16)
    x3 = (r1 - x2.astype(F32)).astype(BF16)
    d = lambda y: lax.dot_general(m_bf16, y, dn, preferred_element_type=F32)
    return d(x1) + (d(x2) + d(x3))


def _bt(x):
    return jnp.stack([x[g].T for g in range(x.shape[0])], axis=0)


def _rwkv_chunk_math(r, k, v, e, a, kk_gain, ka_gain, reverse, groups):
    c = CHUNK
    hd = A_HEAD_DIM
    lane = _iota((1, LANES), 1)
    m_a = jnp.where(lane < hd, 1.0, 0.0)
    m_b = 1.0 - m_a
    ones_seg = _block_ones(LANES, hd)
    ti = _iota((groups, c, c), 1)
    tj = _iota((groups, c, c), 2)
    tri = jnp.where((tj >= ti) if reverse else (tj <= ti), 1.0, 0.0).astype(BF16)

    kk0 = k * kk_gain
    kk = kk0 / jnp.maximum(jnp.sqrt(_seg_sum(kk0 * kk0, ones_seg)), 1e-12)
    kt = k * (1.0 + (a - 1.0) * ka_gain)
    b = kk * a
    g3 = lambda x: x.reshape(groups, c, LANES)
    e3 = g3(e)
    cl = _bmm_exact_lhs(tri, e3)
    last = 0 if reverse else c - 1
    ctot = cl[:, last:last + 1, :]
    g_in = jnp.exp(-cl)
    g_ex = jnp.exp(e3 - cl)
    g_inv = jnp.exp(cl)
    g_end = jnp.exp(cl - ctot)
    st = lambda x: jnp.concatenate([x * m_a, x * m_b], axis=1)
    kk3, b3, kt3 = g3(kk), g3(b), g3(kt)
    kk2 = st(kk3 * g_ex)
    r2 = st(g3(r) * g_in)
    b2 = st(b3 * g_inv)
    k2 = st(kt3 * g_inv)
    v2 = st(g3(v))
    bg2 = st(b3 * g_end)
    kg2 = st(kt3 * g_end)

    s = _bmm(jnp.concatenate([kk2, r2], axis=1), jnp.concatenate([b2, k2], axis=1), nt=True)
    n2 = 2 * c
    i2 = _iota((n2, n2), 0)
    j2 = _iota((n2, n2), 1)
    il = i2 % c
    jl = j2 % c
    strict = (jl > il) if reverse else (jl < il)
    incl = (jl >= il) if reverse else (jl <= il)
    a_b = jnp.where(strict, s[:, :n2, :n2], 0.0)
    a_k = jnp.where(strict, s[:, :n2, n2:], 0.0)
    l_b = jnp.where(incl, s[:, n2:, :n2], 0.0)
    l_k = jnp.where(incl, s[:, n2:, n2:], 0.0)
    eye = jnp.where(i2 == j2, 1.0, 0.0)

    same = lambda m: (i2 // m) == (j2 // m)
    a_d = jnp.where(same(8), a_b, 0.0)
    a_d2 = _bmm(a_d, a_d)
    a_d4 = _bmm(a_d2, a_d2)
    t_inv = _bmm(_bmm(eye - a_d, eye + a_d2), eye + a_d4)
    m = 8
    while m < c:
        a_off = jnp.where(same(2 * m) & jnp.logical_not(same(m)), a_b, 0.0)
        t_inv = t_inv - _bmm(_bmm(t_inv, a_off), t_inv)
        m *= 2

    av = _bmm(a_k, v2)
    z12 = _bmm(t_inv, jnp.concatenate([kk2, av], axis=2))
    z1 = z12[:, :, :LANES]
    z2 = z12[:, :, LANES:]
    vz = jnp.concatenate([v2, z2], axis=1)
    p_bd = eye * jnp.exp(-ctot) - _bmm(_bt(bg2), z1)
    q_bd = _bmm(_bt(jnp.concatenate([kg2, -bg2], axis=1)), vz)
    r2s = r2 - _bmm(l_b, z1)
    y0s = _bmm(jnp.concatenate([l_k, -l_b], axis=2), vz)
    cp = lambda x: x[:, :x.shape[1] // 2] + x[:, x.shape[1] // 2:]
    return cp(p_bd), cp(q_bd), cp(r2s).reshape(groups * c, LANES), cp(y0s).reshape(groups * c, LANES)


def _rwkv_chunk_kernel(r_ref, k_ref, v_ref, e_ref, a_ref, kkg_ref, kag_ref,
                       p_ref, q_ref, r2_ref, y0_ref, *, reverse, groups):
    p, q, r2, y0 = _rwkv_chunk_math(r_ref[...], k_ref[...], v_ref[...], e_ref[0], a_ref[0],
                                    kkg_ref[...], kag_ref[...], reverse, groups)
    p_ref[0] = p
    q_ref[0] = q
    r2_ref[...] = r2
    y0_ref[...] = y0


def _rwkv_chunks(r, k, v, e, a, k_k, k_a, direction):
    t, aw = r.shape
    npair = aw // LANES
    nc = t // CHUNK
    groups = _pick(nc, (10, 4))
    rows = groups * CHUNK
    row_spec = pl.BlockSpec((rows, LANES), lambda p, i: (i, p))
    dir_spec = pl.BlockSpec((1, rows, LANES), lambda p, i: (direction, i, p))
    par_spec = pl.BlockSpec((1, LANES), lambda p, i: (0, p))
    pq_spec = pl.BlockSpec((1, groups, A_HEAD_DIM, LANES), lambda p, i: (p, i, 0, 0))
    pq_shape = jax.ShapeDtypeStruct((npair, nc, A_HEAD_DIM, LANES), F32)
    ry_shape = jax.ShapeDtypeStruct((t, aw), F32)
    return pl.pallas_call(
        functools.partial(_rwkv_chunk_kernel, reverse=bool(direction), groups=groups),
        grid=(npair, t // rows),
        in_specs=[row_spec, row_spec, row_spec, dir_spec, dir_spec, par_spec, par_spec],
        out_specs=(pq_spec, pq_spec, row_spec, row_spec),
        out_shape=(pq_shape, pq_shape, ry_shape, ry_shape),
        compiler_params=_cp(2), name="rwkv_chunks_bwd" if direction else "rwkv_chunks_fwd",
    )(r, k, v, e, a, k_k.reshape(1, aw), k_a.reshape(1, aw))


def _pair_block_diag(x):
    lane = _iota(x.shape, 1)
    return jnp.concatenate([jnp.where(lane < A_HEAD_DIM, x, 0.0), jnp.where(lane >= A_HEAD_DIM, x, 0.0)], axis=0)


def _rwkv_seq_kernel(p_ref, q_ref, r2_ref, y0_ref, y_ref, h_ref, *, reverse, groups, npair):
    c = CHUNK

    @pl.when(pl.program_id(0) == 0)
    def _():
        h_ref[...] = jnp.zeros_like(h_ref)

    order = range(groups - 1, -1, -1) if reverse else range(groups)
    for g in order:
        sl = slice(g * c, (g + 1) * c)
        for p in range(npair):
            ls = slice(p * LANES, (p + 1) * LANES)
            h = h_ref[p]
            y_ref[sl, ls] = _mm(r2_ref[sl, ls], h, RWKV_SEQ_PASSES) + y0_ref[sl, ls]
            h_ref[p] = _mm(_pair_block_diag(p_ref[p, g]), h, RWKV_SEQ_PASSES) + _pair_block_diag(q_ref[p, g])


def _rwkv_seq(p, q, r2, y0, direction, nu):
    npair, nc = p.shape[:2]
    t, aw = r2.shape
    groups = SEQ_GROUP
    rows = groups * CHUNK
    nb = t // rows
    nbu = nu // rows
    assert nu % rows == 0
    if direction:
        blk = lambda s: jnp.where(s < nbu, nbu - 1 - s, nb + nbu - 1 - s)
    else:
        blk = lambda s: s
    pq_spec = pl.BlockSpec((npair, groups, A_HEAD_DIM, LANES), lambda s: (0, blk(s), 0, 0))
    row_spec = pl.BlockSpec((rows, aw), lambda s: (blk(s), 0))
    return pl.pallas_call(
        functools.partial(_rwkv_seq_kernel, reverse=bool(direction), groups=groups, npair=npair),
        grid=(nb,),
        in_specs=[pq_spec, pq_spec, row_spec, row_spec],
        out_specs=row_spec,
        out_shape=jax.ShapeDtypeStruct((t, aw), F32),
        scratch_shapes=[pltpu.VMEM((npair, LANES, LANES), F32)],
        compiler_params=_cp(1), name="rwkv_seq_bwd" if direction else "rwkv_seq_fwd",
    )(p, q, r2, y0)


def _rwkv_out_kernel(yf_ref, yb_ref, r_ref, k_ref, v_ref, a_ref, g_ref, lw_ref, lb_ref, rk_ref, ka_ref, o_ref, *, aw):
    hd = A_HEAD_DIM
    ones_seg = _block_ones(LANES, hd)
    for p in range(aw // LANES):
        ls = slice(p * LANES, (p + 1) * LANES)
        y = yf_ref[:, ls] + yb_ref[:, ls]
        mean = _seg_sum(y, ones_seg) * (1.0 / hd)
        yc = y - mean
        var = _seg_sum(yc * yc, ones_seg) * (1.0 / hd)
        yn = yc * lax.rsqrt(var + A_LNX_EPS) * lw_ref[:, ls] + lb_ref[:, ls]
        r = r_ref[:, ls]
        k = k_ref[:, ls]
        v = v_ref[:, ls]
        bonus = jnp.zeros_like(y)
        for n in range(2):
            kt = k * (1.0 + (a_ref[n, :, ls] - 1.0) * ka_ref[:, ls])
            bonus = bonus + _seg_sum(r * kt * rk_ref[:, ls], ones_seg) * v
        o_ref[:, ls] = ((yn + bonus) * g_ref[:, ls]).astype(o_ref.dtype)


def _rwkv_out(yf, yb, r, k, v, a, g, lnx_w, lnx_b, r_k, k_a):
    t, aw = r.shape
    tm = ROW_TILE
    row_spec = pl.BlockSpec((tm, aw), lambda i: (i, 0))
    par_spec = pl.BlockSpec((1, aw), lambda i: (0, 0))
    return pl.pallas_call(
        functools.partial(_rwkv_out_kernel, aw=aw),
        grid=(t // tm,),
        in_specs=[row_spec] * 5 + [pl.BlockSpec((2, tm, aw), lambda i: (0, i, 0)), row_spec] + [par_spec] * 4,
        out_specs=row_spec,
        out_shape=jax.ShapeDtypeStruct((t, aw), BF16),
        compiler_params=_cp(1), name="rwkv_out",
    )(yf, yb, r, k, v, a, g, lnx_w.reshape(1, aw), lnx_b.reshape(1, aw), r_k.reshape(1, aw), k_a.reshape(1, aw))


def _rms_rope_slab(x, gain, cos, sin, ones_seg):
    ms = _seg_sum(x * x, ones_seg) * (1.0 / B_HEAD_DIM)
    y = x * lax.rsqrt(ms + NORM_EPS) * gain
    return y * cos + _swap16(y) * sin


def _gqa_prep_kernel(z_ref, cos_ref, sin_ref, qg_ref, kg_ref, o_ref, *, qw, kw):
    ones_seg = _block_ones(LANES, B_HEAD_DIM)
    cos = cos_ref[...]
    sin = sin_ref[...]
    scale = B_HEAD_DIM ** -0.5
    for s in range(qw // LANES):
        ls = slice(s * LANES, (s + 1) * LANES)
        o_ref[:, ls] = (_rms_rope_slab(z_ref[:, ls], qg_ref[...], cos, sin, ones_seg) * scale).astype(o_ref.dtype)
    for s in range(kw // LANES):
        ls = slice(qw + s * LANES, qw + (s + 1) * LANES)
        o_ref[:, ls] = _rms_rope_slab(z_ref[:, ls], kg_ref[...], cos, sin, ones_seg).astype(o_ref.dtype)
    o_ref[:, qw + kw:] = z_ref[:, qw + kw:].astype(o_ref.dtype)


def _gqa_prep(zb, cos, sin, q_norm, k_norm):
    t, nb = zb.shape
    tm = ROW_TILE
    qw = B_Q_HEADS * B_HEAD_DIM
    kw = B_KV_HEADS * B_HEAD_DIM
    tile2 = lambda g: jnp.tile(g.reshape(1, B_HEAD_DIM), (1, LANES // B_HEAD_DIM))
    return pl.pallas_call(
        functools.partial(_gqa_prep_kernel, qw=qw, kw=kw),
        grid=(t // tm,),
        in_specs=[pl.BlockSpec((tm, nb), lambda i: (i, 0)),
                  pl.BlockSpec((tm, LANES), lambda i: (i, 0)),
                  pl.BlockSpec((tm, LANES), lambda i: (i, 0)),
                  pl.BlockSpec((1, LANES), lambda i: (0, 0)),
                  pl.BlockSpec((1, LANES), lambda i: (0, 0))],
        out_specs=pl.BlockSpec((tm, nb), lambda i: (i, 0)),
        out_shape=jax.ShapeDtypeStruct((t, nb), BF16),
        compiler_params=_cp(1), name="gqa_prep",
    )(zb, cos, sin, tile2(q_norm), tile2(k_norm))


def _gqa_attn_kernel(sink_ref, q_ref, c_ref, kp_ref, ko_ref, kn_ref, o_ref, *, nu, seq, qw, kw):
    hd = B_HEAD_DIM
    grp = B_Q_HEADS // B_KV_HEADS
    blk = BLOCK
    j = pl.program_id(0)
    jb = j - nu // blk
    nkeys = nu + 3 * blk
    rows = grp * blk
    qi = _iota((rows, nkeys), 0) % blk
    kc = _iota((rows, nkeys), 1)
    q_pos = jb * blk + qi
    k_pos = (jb - 1) * blk + (kc - nu)
    band_ok = (jnp.abs(k_pos - q_pos) <= WINDOW) & (k_pos >= 0) & (k_pos < seq) & (jb >= 0)
    valid = (kc < nu) | band_ok
    rg = _iota((rows, 1), 0) // blk
    outs = []
    for h in range(B_KV_HEADS):
        ks = slice(qw + h * hd, qw + (h + 1) * hd)
        vs = slice(qw + kw + h * hd, qw + kw + (h + 1) * hd)
        k_all = jnp.concatenate([c_ref[:, ks], kp_ref[:, ks], ko_ref[:, ks], kn_ref[:, ks]], axis=0)
        v_all = jnp.concatenate([c_ref[:, vs], kp_ref[:, vs], ko_ref[:, vs], kn_ref[:, vs]], axis=0)
        q4 = jnp.concatenate([q_ref[:, (h * grp + g) * hd:(h * grp + g + 1) * hd] for g in range(grp)], axis=0)
        s = jnp.where(valid, _dg(q4, k_all, 1, 1), NEG_INF)
        sink = jnp.zeros((rows, 1), F32)
        for g in range(grp):
            sink = jnp.where(rg == g, sink_ref[h * grp + g], sink)
        m = jnp.maximum(jnp.max(s, axis=1, keepdims=True), sink)
        p = jnp.exp(s - m)
        den = jnp.sum(p, axis=1, keepdims=True) + jnp.exp(sink - m)
        o = _dg(p.astype(BF16), v_all) / den
        outs += [o[g * blk:(g + 1) * blk, :] for g in range(grp)]
    o_ref[...] = jnp.concatenate(outs, axis=1).astype(o_ref.dtype)


def _gqa_attn(qkv, sink, nu, seq):
    t, nb = qkv.shape
    qw = B_Q_HEADS * B_HEAD_DIM
    kw = B_KV_HEADS * B_HEAD_DIM
    blk = BLOCK
    nblk = t // blk
    band = lambda off: pl.BlockSpec((blk, nb), lambda j: (jnp.clip(j + off, 0, nblk - 1), 0))
    return pl.pallas_call(
        functools.partial(_gqa_attn_kernel, nu=nu, seq=seq, qw=qw, kw=kw),
        grid=(nblk,),
        in_specs=[pl.BlockSpec(memory_space=pltpu.SMEM),
                  pl.BlockSpec((blk, nb), lambda j: (j, 0)),
                  pl.BlockSpec((nu, nb), lambda j: (0, 0)),
                  band(-1), band(0), band(1)],
        out_specs=pl.BlockSpec((blk, qw), lambda j: (j, 0)),
        out_shape=jax.ShapeDtypeStruct((t, qw), BF16),
        compiler_params=_cp(1), name="gqa_attn",
    )(sink, qkv, qkv, qkv, qkv, qkv)


def _mla_prep_kernel(q_ref, kv_ref, kr_ref, cos_ref, sin_ref, nn_ref, rn_ref, qo_ref, ko_ref, vo_ref):
    cos = cos_ref[...]
    sin = sin_ref[...]
    dn = C_NOPE_DIM
    hw = 2 * LANES
    scale = (C_NOPE_DIM + C_ROPE_DIM) ** -0.5 * math.log2(math.e)
    ones = jnp.ones((q_ref.shape[0], C_V_DIM), vo_ref.dtype)

    def rms_rope(x, gain):
        ms = jnp.sum(x * x, axis=-1, keepdims=True) * (1.0 / C_ROPE_DIM)
        y = x * lax.rsqrt(ms + NORM_EPS) * gain
        return y * cos + _swap16(y) * sin

    def rms(x, gain):
        return x * lax.rsqrt(jnp.mean(x * x, axis=-1, keepdims=True) + NORM_EPS) * gain

    kr = rms_rope(kr_ref[...], rn_ref[1:2, :]).astype(ko_ref.dtype)
    for h in range(C_HEADS):
        qn = rms(q_ref[:, h * hw:h * hw + dn], nn_ref[0:1, :])
        qr = rms_rope(q_ref[:, h * hw + dn:(h + 1) * hw], rn_ref[0:1, :])
        qo_ref[:, h * hw:h * hw + dn] = (qn * scale).astype(qo_ref.dtype)
        qo_ref[:, h * hw + dn:(h + 1) * hw] = (qr * scale).astype(qo_ref.dtype)
        ko_ref[:, h * hw:h * hw + dn] = rms(kv_ref[:, h * dn:(h + 1) * dn], nn_ref[1:2, :]).astype(ko_ref.dtype)
        ko_ref[:, h * hw + dn:(h + 1) * hw] = kr
        vo_ref[:, h * hw:h * hw + C_V_DIM] = kv_ref[:, (C_HEADS + h) * dn:(C_HEADS + h + 1) * dn].astype(vo_ref.dtype)
        vo_ref[:, h * hw + C_V_DIM:(h + 1) * hw] = ones


def _mla_prep(q, kv, zc, kr_col_block, cos, sin, nope_norm, rope_norm_p):
    t = q.shape[0]
    tm = ROW_TILE
    hw = 2 * LANES
    row = lambda w: pl.BlockSpec((tm, w), lambda i: (i, 0))
    full = lambda a: pl.BlockSpec(a.shape, lambda i: (0, 0))
    return pl.pallas_call(
        _mla_prep_kernel,
        grid=(t // tm,),
        in_specs=[row(q.shape[1]), row(kv.shape[1]),
                  pl.BlockSpec((tm, LANES), lambda i: (i, kr_col_block)),
                  row(LANES), row(LANES), full(nope_norm), full(rope_norm_p)],
        out_specs=(row(C_HEADS * hw),) * 3,
        out_shape=(jax.ShapeDtypeStruct((t, C_HEADS * hw), BF16),) * 3,
        compiler_params=_cp(1), name="mla_prep",
    )(q, kv, zc, cos, sin, nope_norm, rope_norm_p)


def _mla_flash_kernel(q_ref, k_ref, v_ref, o_ref, m_ref, acc_ref, *, nu, tq, tk, sub):
    qi = pl.program_id(1)
    ki = pl.program_id(2)

    @pl.when(ki == 0)
    def _():
        m_ref[...] = jnp.full_like(m_ref, NEG_INF)
        acc_ref[...] = jnp.zeros_like(acc_ref)

    def update(masked):
        for qs in range(tq // sub):
            rs = slice(qs * sub, (qs + 1) * sub)
            s = _dg(q_ref[rs, :], k_ref[...], 1, 1)
            if masked:
                qrow = qi * tq + qs * sub + _iota((sub, tk), 0)
                kcol = ki * tk + _iota((sub, tk), 1)
                s = jnp.where((qrow < nu) & (kcol >= nu), NEG_INF, s)
            m_prev = m_ref[rs, :]
            m_next = jnp.maximum(m_prev, jnp.max(s, axis=1, keepdims=True))
            alpha = jnp.exp2(m_prev - m_next)
            p = jnp.exp2((s - m_next[:, 0:1]).astype(BF16))
            acc_ref[rs, :] = jnp.concatenate([alpha, alpha], axis=1) * acc_ref[rs, :] + _dg(p, v_ref[...])
            m_ref[rs, :] = m_next

    has_ctx_rows = qi * tq < nu

    @pl.when(has_ctx_rows)
    def _():
        update(True)

    @pl.when(jnp.logical_not(has_ctx_rows))
    def _():
        update(False)

    @pl.when(ki == pl.num_programs(2) - 1)
    def _():
        o_ref[...] = (acc_ref[:, :C_V_DIM] / acc_ref[:, C_V_DIM:]).astype(o_ref.dtype)


def _mla_flash(qf, kf, vf, nu):
    t = qf.shape[0]
    hw = 2 * LANES
    tq = _pick(t, (1280, 256))
    tk = _pick(t, (1280, 256))
    return pl.pallas_call(
        functools.partial(_mla_flash_kernel, nu=nu, tq=tq, tk=tk, sub=256),
        grid=(C_HEADS, t // tq, t // tk),
        in_specs=[pl.BlockSpec((tq, hw), lambda h, i, j: (i, h)),
                  pl.BlockSpec((tk, hw), lambda h, i, j: (j, h)),
                  pl.BlockSpec((tk, hw), lambda h, i, j: (j, h))],
        out_specs=pl.BlockSpec((tq, C_V_DIM), lambda h, i, j: (i, h)),
        out_shape=jax.ShapeDtypeStruct((t, C_HEADS * C_V_DIM), BF16),
        scratch_shapes=[pltpu.VMEM((tq, LANES), F32), pltpu.VMEM((tq, hw), F32)],
        compiler_params=_cp(3), name="mla_flash",
    )(qf, kf, vf)


def _merge_kernel(ya_ref, yb_ref, yc_ref, w_ref, g0_ref, g1_ref, g2_ref, o_ref):
    acc = g0_ref[...].astype(F32) * _dg(ya_ref[...], w_ref[0])
    acc = acc + g1_ref[...].astype(F32) * _dg(yb_ref[...], w_ref[1])
    acc = acc + g2_ref[...].astype(F32) * _dg(yc_ref[...], w_ref[2])
    o_ref[...] = acc.astype(o_ref.dtype)


def _merge(ya, yb, yc, w_branch, gates):
    t, bw = ya.shape
    d = w_branch.shape[2]
    tm = _pick(t, (640, 256, 128))
    tn = _pick(d, (1024, 512, 256, 128))
    nj = d // tn
    y_spec = pl.BlockSpec((tm, bw), lambda j, i: (i, 0))
    gate = lambda b: pl.BlockSpec((tm, tn), lambda j, i: (i, b * nj + j))
    return pl.pallas_call(
        _merge_kernel,
        grid=(nj, t // tm),
        in_specs=[y_spec, y_spec, y_spec, pl.BlockSpec((N_BRANCH, bw, tn), lambda j, i: (0, 0, j)),
                  gate(0), gate(1), gate(2)],
        out_specs=pl.BlockSpec((tm, tn), lambda j, i: (i, j)),
        out_shape=jax.ShapeDtypeStruct((t, d), BF16),
        compiler_params=_cp(2), name="merge",
    )(ya, yb, yc, w_branch, gates, gates, gates)


def _moe_select_kernel(aff_ref, loc_ref, base_ref, *, cap, n_exp):
    aff = aff_ref[...]
    bits = pltpu.bitcast(aff, I32)
    count = lambda m: jnp.sum(jnp.sum(jnp.where(m, 1.0, 0.0), axis=1, keepdims=True), axis=2, keepdims=True)
    theta = jnp.zeros((n_exp, 1, 1), I32)
    for bit in range(30, -1, -1):
        cand = theta | (1 << bit)
        theta = jnp.where(count(bits >= cand) >= cap, cand, theta)
    gt = bits > theta
    eq = bits == theta
    need = cap - count(gt)
    r = aff.shape[1]
    upper = jnp.where(_iota((LANES, LANES), 0) <= _iota((LANES, LANES), 1), 1.0, 0.0).astype(BF16)
    lower_strict = jnp.where(_iota((r, r), 1) < _iota((r, r), 0), 1.0, 0.0).astype(BF16)

    def prefix(x):
        incl = _dg(x.astype(BF16), upper)
        tot = jnp.broadcast_to(incl[:, LANES - 1:LANES], (r, LANES))
        return incl, _dg(lower_strict, tot.astype(BF16))

    for e in range(n_exp):
        xe = jnp.where(eq[e], 1.0, 0.0)
        incl, base = prefix(xe)
        take = eq[e] & ((incl - xe + base) < need[e])
        sel = jnp.where(gt[e] | take, 1.0, 0.0)
        incl, base = prefix(sel)
        loc_ref[e] = jnp.where(sel > 0.0, incl - sel, -4096.0)
        base_ref[e] = base


def _moe_select(aff3, cap):
    e, r, _ = aff3.shape
    spec = pl.BlockSpec((e, r, LANES), lambda i: (0, 0, 0))
    shape = jax.ShapeDtypeStruct((e, r, LANES), F32)
    return pl.pallas_call(
        functools.partial(_moe_select_kernel, cap=cap, n_exp=e),
        grid=(1,), in_specs=[spec], out_specs=(spec, spec), out_shape=(shape, shape),
        compiler_params=_cp(1), name="moe_select",
    )(aff3)


def _window_start(base_ref, e, i):
    b = base_ref[e, i]
    a = (b // MOE_ALIGN) * MOE_ALIGN
    return pl.multiple_of(a, MOE_ALIGN), b - a


def _moe_gather_kernel(base_ref, loc_ref, aff_ref, h_ref, xe_ref, ge_ref):
    e = pl.program_id(0)
    i = pl.program_id(1)

    @pl.when(i == 0)
    def _():
        xe_ref[...] = jnp.zeros_like(xe_ref)
        ge_ref[...] = jnp.zeros_like(ge_ref)

    a, off = _window_start(base_ref, e, i)
    slot = loc_ref[0, 0] + off.astype(F32)
    onehot = jnp.where(_iota((MOE_WIN, MOE_TILE), 0).astype(F32) == slot, 1.0, 0.0)
    rows = _dg(onehot.astype(BF16), h_ref[...])
    win = pl.ds(a, MOE_WIN)
    xe_ref[0, win, :] = (xe_ref[0, win, :].astype(F32) + rows).astype(xe_ref.dtype)
    gsel = jnp.sum(onehot * aff_ref[0, 0], axis=1, keepdims=True)
    ge_ref[0, win, :] = ge_ref[0, win, :] + gsel


def _moe_gather(base_i, loc, aff3, h_all, row0, n, cap_p):
    e = loc.shape[0]
    d = h_all.shape[1]
    nt = n // MOE_TILE
    t0 = row0 // MOE_TILE
    grid_spec = pltpu.PrefetchScalarGridSpec(
        num_scalar_prefetch=1, grid=(e, nt),
        in_specs=[pl.BlockSpec((1, 1, 1, LANES), lambda ee, i, b: (ee, i, 0, 0)),
                  pl.BlockSpec((1, 1, 1, LANES), lambda ee, i, b: (ee, i, 0, 0)),
                  pl.BlockSpec((MOE_TILE, d), lambda ee, i, b: (t0 + i, 0))],
        out_specs=(pl.BlockSpec((1, cap_p, d), lambda ee, i, b: (ee, 0, 0)),
                   pl.BlockSpec((1, cap_p, LANES), lambda ee, i, b: (ee, 0, 0))))
    return pl.pallas_call(
        _moe_gather_kernel, grid_spec=grid_spec,
        out_shape=(jax.ShapeDtypeStruct((e, cap_p, d), BF16), jax.ShapeDtypeStruct((e, cap_p, LANES), F32)),
        compiler_params=_cp(2), name="moe_gather",
    )(base_i, loc, aff3, h_all)


def _moe_ffn_kernel(x_ref, g_ref, wg_ref, wu_ref, wd_ref, y_ref):
    x = x_ref[0]
    hg = _dg(x, wg_ref[0])
    hid = hg * _sigmoid(hg) * _dg(x, wu_ref[0])
    y = _dg(hid.astype(BF16), wd_ref[0]) * g_ref[0][:, 0:1]
    y_ref[0] = y.astype(y_ref.dtype)


def _moe_ffn(xe, ge, w_gate, w_up, w_down, tc):
    e, cap_p, d = xe.shape
    f = w_gate.shape[2]
    return pl.pallas_call(
        _moe_ffn_kernel,
        grid=(e, cap_p // tc),
        in_specs=[pl.BlockSpec((1, tc, d), lambda ee, i: (ee, i, 0)),
                  pl.BlockSpec((1, tc, LANES), lambda ee, i: (ee, i, 0)),
                  pl.BlockSpec((1, d, f), lambda ee, i: (ee, 0, 0)),
                  pl.BlockSpec((1, d, f), lambda ee, i: (ee, 0, 0)),
                  pl.BlockSpec((1, f, d), lambda ee, i: (ee, 0, 0))],
        out_specs=pl.BlockSpec((1, tc, d), lambda ee, i: (ee, i, 0)),
        out_shape=jax.ShapeDtypeStruct((e, cap_p, d), BF16),
        compiler_params=_cp(2), name="moe_ffn",
    )(xe, ge, w_gate, w_up, w_down)


def _moe_combine_kernel(base_ref, loc_ref, *rest, n_win):
    y_refs = rest[:n_win]
    x_ref, gate_ref, o_ref, acc_ref = rest[n_win:]
    i = pl.program_id(0)
    e = pl.program_id(1)

    @pl.when(e == 0)
    def _():
        acc_ref[...] = jnp.zeros_like(acc_ref)

    _, off = _window_start(base_ref, e, i)
    slot_row = jnp.broadcast_to(loc_ref[0, 0] + off.astype(F32), (MOE_TILE, LANES))
    slot_col = slot_row.T
    slot_col = jnp.concatenate([slot_col, slot_col[:, :MOE_WIN - LANES]], axis=1)
    onehot = jnp.where(_iota((MOE_TILE, MOE_WIN), 1).astype(F32) == slot_col, 1.0, 0.0)
    ywin = jnp.concatenate([y[0] for y in y_refs], axis=0)
    acc_ref[...] += _dg(onehot.astype(BF16), ywin)

    @pl.when(e == pl.num_programs(1) - 1)
    def _():
        o_ref[...] = x_ref[...] + gate_ref[...] * acc_ref[...]


def _moe_combine(base_i, loc, ye, x_all, gate_row, row0, n):
    e, cap_p, d = ye.shape
    nt = n // MOE_TILE
    t0 = row0 // MOE_TILE
    n_win = MOE_WIN // MOE_ALIGN

    def y_spec(kk):
        return pl.BlockSpec((1, MOE_ALIGN, d), lambda i, ee, b: (ee, b[ee, i] // MOE_ALIGN + kk, 0))

    grid_spec = pltpu.PrefetchScalarGridSpec(
        num_scalar_prefetch=1, grid=(nt, e),
        in_specs=[pl.BlockSpec((1, 1, 1, LANES), lambda i, ee, b: (ee, i, 0, 0))]
                 + [y_spec(kk) for kk in range(n_win)]
                 + [pl.BlockSpec((MOE_TILE, d), lambda i, ee, b: (t0 + i, 0)),
                    pl.BlockSpec((1, d), lambda i, ee, b: (0, 0))],
        out_specs=pl.BlockSpec((MOE_TILE, d), lambda i, ee, b: (t0 + i, 0)),
        scratch_shapes=[pltpu.VMEM((MOE_TILE, d), F32)])
    x_index = 2 + n_win
    return pl.pallas_call(
        functools.partial(_moe_combine_kernel, n_win=n_win), grid_spec=grid_spec,
        out_shape=jax.ShapeDtypeStruct(x_all.shape, F32),
        input_output_aliases={x_index: 0},
        compiler_params=_cp(2), name="moe_combine",
    )(base_i, loc, *([ye] * n_win), x_all, gate_row)


def _moe_stream(x_all, h_all, aff, row0, n, gate_row, w_gate, w_up, w_down):
    e = aff.shape[0]
    cap = (CAPACITY_FACTOR * n) // e
    tc = 256 if cap >= 256 else 64
    cap_p = -(-(cap + MOE_WIN) // tc) * tc
    n_sel = SEL_ROWS * LANES
    aff_s = lax.dynamic_slice_in_dim(aff, row0, n, axis=1)
    aff3 = jnp.pad(aff_s, ((0, 0), (0, n_sel - n))).reshape(e, SEL_ROWS, LANES)
    loc, base = _moe_select(aff3, cap)
    base_i = base[:, :, 0].astype(I32)
    loc4 = loc.reshape(e, SEL_ROWS, 1, LANES)
    aff4 = aff3.reshape(e, SEL_ROWS, 1, LANES)
    xe, ge = _moe_gather(base_i, loc4, aff4, h_all, row0, n, cap_p)
    ye = _moe_ffn(xe, ge, w_gate, w_up, w_down, tc)
    return _moe_combine(base_i, loc4, ye, x_all, gate_row, row0, n)


def _rope_tables(nu, seq):
    n_freq = ROPE_DIM // 4
    pos = jnp.arange(seq)
    inv = jnp.power(ROPE_THETA, -jnp.arange(n_freq, dtype=F32) / n_freq)
    ang_r = (pos // GRID_W).astype(F32)[:, None] * inv[None]
    ang_c = (pos % GRID_W).astype(F32)[:, None] * inv[None]
    cos = jnp.concatenate([jnp.cos(ang_r)] * 2 + [jnp.cos(ang_c)] * 2, axis=1)
    sin = jnp.concatenate([-jnp.sin(ang_r), jnp.sin(ang_r), -jnp.sin(ang_c), jnp.sin(ang_c)], axis=1)
    cos = jnp.concatenate([jnp.ones((nu, ROPE_DIM), F32), cos], axis=0)
    sin = jnp.concatenate([jnp.zeros((nu, ROPE_DIM), F32), sin], axis=0)
    rep = LANES // ROPE_DIM
    return jnp.tile(cos, (1, rep)), jnp.tile(sin, (1, rep))


def _pad_cols(w, n):
    return jnp.pad(w, ((0, 0), (0, n - w.shape[1])))


def _round_up(x, m):
    return -(-x // m) * m


def kernel(x, c, ctx, c_ctx, norm_mix, norm_ffn, ada_down, ada_up, ada_bias, w_in, rwkv_mu, rwkv_w0, rwkv_w_up, rwkv_a0, rwkv_a_up, rwkv_g_up, rwkv_k_k, rwkv_k_a, rwkv_r_k, rwkv_lnx_w, rwkv_lnx_b, gqa_q_norm, gqa_k_norm, gqa_sink, mla_q_a_norm, mla_q_up, mla_kv_a_norm, mla_kv_up, mla_nope_norm, mla_rope_norm, w_branch, w_out, moe_router, moe_w_gate, moe_w_up, moe_w_down):
    bsz, seq, d = x.shape
    assert bsz == 1
    nu = ctx.shape[1]
    depth = w_in.shape[0]
    aw = A_HEADS * A_HEAD_DIM
    dr = rwkv_w_up.shape[2]
    gr = rwkv_g_up.shape[1]
    n_a = 3 * aw + 4 * dr + gr
    n_a_p = _round_up(n_a, LANES)
    n_b = (B_Q_HEADS + 2 * B_KV_HEADS) * B_HEAD_DIM
    cq = mla_q_up.shape[1]
    ckv = mla_kv_up.shape[1]
    n_c = cq + ckv + C_ROPE_DIM
    n_c_p = cq + ckv + LANES
    hw = 2 * LANES

    x_all = jnp.concatenate([ctx[0], x[0]], axis=0)
    cond8 = jnp.zeros((8, d), F32).at[0].set(c_ctx).at[1].set(c[0])
    cos, sin = _rope_tables(nu, seq)

    for i in range(depth):
        wi = w_in[i]
        w_a = _pad_cols(wi[:, :n_a], n_a_p).astype(BF16)
        w_b = wi[:, n_a:n_a + n_b].astype(BF16)
        w_c = _pad_cols(wi[:, n_a + n_b:n_a + n_b + n_c], n_c_p).astype(BF16)
        w_g = wi[:, n_a + n_b + n_c:].astype(BF16)
        mu_p = _pad_cols(rwkv_mu[i], n_a_p)
        g_up_p = jnp.pad(rwkv_g_up[i], ((0, n_a_p - n_a), (0, 0)))
        qu = mla_q_up[i].reshape(cq, C_HEADS, C_NOPE_DIM + C_ROPE_DIM)
        qu = jnp.pad(qu, ((0, 0), (0, 0), (0, hw - C_NOPE_DIM - C_ROPE_DIM))).reshape(cq, C_HEADS * hw).astype(BF16)
        kvu = mla_kv_up[i].reshape(ckv, C_HEADS, C_NOPE_DIM + C_V_DIM)
        kvu = jnp.concatenate([kvu[:, :, :C_NOPE_DIM].reshape(ckv, -1), kvu[:, :, C_NOPE_DIM:].reshape(ckv, -1)],
                              axis=1).astype(BF16)
        rope_norm_p = _pad_cols(mla_rope_norm[i], LANES)

        mod = _adaln(cond8, ada_down[i], ada_up[i], ada_bias[i])[:2].reshape(2, N_MOD, d)

        h = _norm_mod(x_all, norm_mix[i], mod[:, 0], mod[:, 1], nu)
        za = _matmul(h, w_a, F32, name="w_in_a")
        zb = _matmul(h, w_b, F32, name="w_in_b")
        zc = _matmul(h, w_c, F32, name="w_in_c")
        gates = _matmul(h, w_g, BF16, act="sigmoid", name="w_in_gate")

        r, k, v, e, a, g = _rwkv_feat(za, mu_p, rwkv_w0[i], rwkv_w_up[i], rwkv_a0[i], rwkv_a_up[i], g_up_p, nu)
        ys = []
        for direction in (0, 1):
            p_, q_, r2, y0 = _rwkv_chunks(r, k, v, e, a, rwkv_k_k[i], rwkv_k_a[i], direction)
            ys.append(_rwkv_seq(p_, q_, r2, y0, direction, nu))
        ya = _rwkv_out(ys[0], ys[1], r, k, v, a, g, rwkv_lnx_w[i], rwkv_lnx_b[i], rwkv_r_k[i], rwkv_k_a[i])

        qkv_b = _gqa_prep(zb, cos, sin, gqa_q_norm[i], gqa_k_norm[i])
        yb = _gqa_attn(qkv_b, gqa_sink[i], nu, seq)

        q_c = _matmul(zc[:, :cq], qu, F32, rms_gain=mla_q_a_norm[i], name="mla_q_up")
        kv_c = _matmul(zc[:, cq:cq + ckv], kvu, F32, rms_gain=mla_kv_a_norm[i], name="mla_kv_up")
        qf, kf, vf = _mla_prep(q_c, kv_c, zc, (cq + ckv) // LANES, cos, sin, mla_nope_norm[i], rope_norm_p)
        yc = _mla_flash(qf, kf, vf, nu)

        merged = _merge(ya, yb, yc, w_branch[i].astype(BF16), gates)
        x_all = _matmul(merged, w_out[i].astype(BF16), F32, resid=x_all, gate2=mod[:, 2], nu=nu, name="w_out")

        hf, aff = _norm_mod(x_all, norm_ffn[i], mod[:, 3], mod[:, 4], nu, router_t=moe_router[i].T)
        wg = moe_w_gate[i].astype(BF16)
        wu = moe_w_up[i].astype(BF16)
        wd = moe_w_down[i].astype(BF16)
        x_all = _moe_stream(x_all, hf, aff, nu, seq, mod[1:2, 5], wg, wu, wd)
        if i < depth - 1:
            x_all = _moe_stream(x_all, hf, aff, 0, nu, mod[0:1, 5], wg, wu, wd)
    return x_all[nu:].reshape(bsz, seq, d)
```

```python
import functools
import math

import jax
import jax.numpy as jnp
import numpy as np
from jax import lax
from jax.experimental import pallas as pl
from jax.experimental.pallas import tpu as pltpu

F32 = jnp.float32
BF16 = jnp.bfloat16
I32 = jnp.int32

GRID_W = 64
ROPE_DIM = 64
ROPE_THETA = 10000.0
NORM_EPS = 1e-6
NEG_INF = -1e30
N_MOD = 6
A_HEADS = 16
A_HEAD_DIM = 64
A_LNX_EPS = 64e-5
B_Q_HEADS = 16
B_KV_HEADS = 4
B_HEAD_DIM = 64
WINDOW = 128
BLOCK = 128
C_HEADS = 8
C_NOPE_DIM = 128
C_ROPE_DIM = 64
C_V_DIM = 128
N_BRANCH = 3
CAPACITY_FACTOR = 2

LANES = 128
CHUNK = 64
SEQ_GROUP = 4
ROW_TILE = 256
MOE_TILE = 128
MOE_ALIGN = 16
MOE_WIN = MOE_TILE + MOE_ALIGN
SEL_ROWS = 128
RWKV_SEQ_PASSES = 3
VMEM_MB = 56


def _cp(n_grid, vmem_mb=VMEM_MB):
    return pltpu.CompilerParams(dimension_semantics=("arbitrary",) * n_grid,
                                vmem_limit_bytes=vmem_mb * 1024 * 1024)


def _pick(n, cands):
    for c in cands:
        if n % c == 0:
            return c
    raise ValueError(f"no tile for {n} in {cands}")


def _dg(a, b, ca=1, cb=0):
    return lax.dot_general(a, b, (((ca,), (cb,)), ((), ())), preferred_element_type=F32)


def _split2(x):
    hi = x.astype(BF16)
    lo = (x - hi.astype(F32)).astype(BF16)
    return hi, lo


def _mm(a, b, passes=1, nt=False):
    cb = 1 if nt else 0
    if passes == 1:
        return _dg(a.astype(BF16), b.astype(BF16), 1, cb)
    ah, al = _split2(a)
    bh, bl = _split2(b)
    return _dg(ah, bh, 1, cb) + (_dg(ah, bl, 1, cb) + _dg(al, bh, 1, cb))


def _mm_exact_rhs(x, m_bf16):
    x1 = x.astype(BF16)
    r1 = x - x1.astype(F32)
    x2 = r1.astype(BF16)
    x3 = (r1 - x2.astype(F32)).astype(BF16)
    return _dg(x1, m_bf16) + (_dg(x2, m_bf16) + _dg(x3, m_bf16))


def _sigmoid(x):
    return 1.0 / (1.0 + jnp.exp(-x))


def _iota(shape, dim):
    return lax.broadcasted_iota(I32, shape, dim)


def _block_ones(n, blk):
    i = _iota((n, n), 0) // blk
    j = _iota((n, n), 1) // blk
    return jnp.where(i == j, 1.0, 0.0).astype(BF16)


def _seg_sum(x, ones_bf16):
    return _mm_exact_rhs(x, ones_bf16)


def _swap16(x):
    n = x.shape[-1]
    lane = _iota(x.shape, x.ndim - 1)
    fwd = pltpu.roll(x, n - 16, x.ndim - 1)
    bwd = pltpu.roll(x, 16, x.ndim - 1)
    return jnp.where((lane % 32) < 16, fwd, bwd)


def _adaln_kernel(c_ref, dn_ref, up_ref, b_ref, o_ref):
    c = c_ref[...]
    t = _mm(c * _sigmoid(c), dn_ref[...], 3)
    o_ref[...] = _mm(t, up_ref[...], 3) + b_ref[...]


def _adaln(cond8, down, up, bias):
    d, r = down.shape
    n = up.shape[1]
    tn = _pick(n, (4096, 2048, 1024, 512, 256, 128))
    return pl.pallas_call(
        _adaln_kernel,
        grid=(n // tn,),
        in_specs=[pl.BlockSpec((8, d), lambda j: (0, 0)),
                  pl.BlockSpec((d, r), lambda j: (0, 0)),
                  pl.BlockSpec((r, tn), lambda j: (0, j)),
                  pl.BlockSpec((1, tn), lambda j: (0, j))],
        out_specs=pl.BlockSpec((8, tn), lambda j: (0, j)),
        out_shape=jax.ShapeDtypeStruct((8, n), F32),
        compiler_params=_cp(1),
        name="adaln",
    )(cond8, down, up, bias.reshape(1, n))


def _rows_are_ctx(tile_rows, row0, nu):
    return (row0 + _iota((tile_rows, 1), 0)) < nu


def _norm_mod_kernel(x_ref, g_ref, sh_ref, sc_ref, *rest, nu, tm, router):
    if router:
        wr_ref, h_ref, aff_ref = rest
    else:
        (h_ref,) = rest
    x = x_ref[...]
    is_u = _rows_are_ctx(tm, pl.program_id(0) * tm, nu)
    y = x * lax.rsqrt(jnp.mean(x * x, axis=-1, keepdims=True) + NORM_EPS) * g_ref[...]
    sh = jnp.where(is_u, sh_ref[0:1, :], sh_ref[1:2, :])
    sc = jnp.where(is_u, sc_ref[0:1, :], sc_ref[1:2, :])
    h = y * (1.0 + sc) + sh
    h_ref[...] = h.astype(h_ref.dtype)
    if router:
        logits = _mm(wr_ref[...], h, 3, nt=True)
        m = jnp.max(logits, axis=0, keepdims=True)
        p = jnp.exp(logits - m)
        aff_ref[...] = p / jnp.sum(p, axis=0, keepdims=True)


def _norm_mod(x_all, gain, shift2, scale2, nu, router_t=None):
    t, d = x_all.shape
    tm = ROW_TILE
    router = router_t is not None
    in_specs = [pl.BlockSpec((tm, d), lambda i: (i, 0)),
                pl.BlockSpec((1, d), lambda i: (0, 0)),
                pl.BlockSpec((2, d), lambda i: (0, 0)),
                pl.BlockSpec((2, d), lambda i: (0, 0))]
    args = [x_all, gain.reshape(1, d), shift2, scale2]
    out_specs = pl.BlockSpec((tm, d), lambda i: (i, 0))
    out_shape = jax.ShapeDtypeStruct((t, d), BF16)
    if router:
        e = router_t.shape[0]
        in_specs.append(pl.BlockSpec((e, d), lambda i: (0, 0)))
        args.append(router_t)
        out_specs = (out_specs, pl.BlockSpec((e, tm), lambda i: (0, i)))
        out_shape = (out_shape, jax.ShapeDtypeStruct((e, t), F32))
    return pl.pallas_call(
        functools.partial(_norm_mod_kernel, nu=nu, tm=tm, router=router),
        grid=(t // tm,), in_specs=in_specs, out_specs=out_specs, out_shape=out_shape,
        compiler_params=_cp(1), name="norm_mod_router" if router else "norm_mod",
    )(*args)


def _matmul_kernel(*refs, nu, tm, act, has_rms, has_resid):
    it = iter(refs)
    a_ref = next(it)
    w_ref = next(it)
    g_ref = next(it) if has_rms else None
    x_ref = next(it) if has_resid else None
    gate_ref = next(it) if has_resid else None
    o_ref = next(it)
    a = a_ref[...]
    if has_rms:
        af = a.astype(F32)
        a = af * lax.rsqrt(jnp.mean(af * af, axis=-1, keepdims=True) + NORM_EPS) * g_ref[...]
    acc = _dg(a.astype(BF16), w_ref[...])
    if act == "sigmoid":
        acc = _sigmoid(acc)
    if has_resid:
        is_u = _rows_are_ctx(tm, pl.program_id(1) * tm, nu)
        gate = jnp.where(is_u, gate_ref[0:1, :], gate_ref[1:2, :])
        acc = x_ref[...] + gate * acc
    o_ref[...] = acc.astype(o_ref.dtype)


def _matmul(a, w, out_dtype, *, a_col_block=0, act=None, rms_gain=None, resid=None, gate2=None, nu=0,
            name="matmul"):
    m = a.shape[0]
    k = w.shape[0]
    n = w.shape[1]
    tm = _pick(m, (640, 256, 128))
    tn = n if n <= 2048 else _pick(n, (1024, 768, 512, 384, 256, 128))
    has_rms = rms_gain is not None
    has_resid = resid is not None
    in_specs = [pl.BlockSpec((tm, k), lambda j, i: (i, a_col_block)),
                pl.BlockSpec((k, tn), lambda j, i: (0, j))]
    args = [a, w]
    if has_rms:
        in_specs.append(pl.BlockSpec((1, k), lambda j, i: (0, 0)))
        args.append(rms_gain.reshape(1, k))
    if has_resid:
        in_specs += [pl.BlockSpec((tm, tn), lambda j, i: (i, j)),
                     pl.BlockSpec((2, tn), lambda j, i: (0, j))]
        args += [resid, gate2]
    return pl.pallas_call(
        functools.partial(_matmul_kernel, nu=nu, tm=tm, act=act, has_rms=has_rms, has_resid=has_resid),
        grid=(n // tn, m // tm), in_specs=in_specs,
        out_specs=pl.BlockSpec((tm, tn), lambda j, i: (i, j)),
        out_shape=jax.ShapeDtypeStruct((m, n), out_dtype),
        compiler_params=_cp(2), name=name,
    )(*args)


def _rwkv_feat_kernel(z_ref, zp_ref, zn_ref, mu_ref, w0_ref, wup_ref, a0_ref, aup_ref, gup_ref,
                      r_ref, k_ref, v_ref, e_ref, a_ref, g_ref, *, nu, t_all, tm, aw, dr):
    z = z_ref[...]
    row = pl.program_id(0) * tm + _iota((tm, 1), 0)
    ri = _iota((tm, 1), 0)
    up1 = pltpu.roll(z, 1, 0)
    dn1 = pltpu.roll(z, tm - 1, 0)
    zp = jnp.where(ri == 0, zp_ref[7:8, :], up1)
    zn = jnp.where(ri == tm - 1, zn_ref[0:1, :], dn1)
    zp = jnp.where((row == 0) | (row == nu), 0.0, zp)
    zn = jnp.where((row == nu - 1) | (row == t_all - 1), 0.0, zn)
    zs = z + mu_ref[0:1, :] * (zp - z) + mu_ref[1:2, :] * (zn - z)
    r_ref[...] = zs[:, 0:aw]
    k_ref[...] = zs[:, aw:2 * aw]
    v_ref[...] = zs[:, 2 * aw:3 * aw]
    o = 3 * aw
    for n in range(2):
        wd = jnp.tanh(zs[:, o + n * dr:o + (n + 1) * dr])
        w = w0_ref[n:n + 1, :] + _mm(wd, wup_ref[n], 3)
        sp = jnp.maximum(-w, 0.0) + jnp.log(1.0 + jnp.exp(-jnp.abs(w)))
        e_ref[n] = jnp.exp(-sp - 0.5)
    o += 2 * dr
    for n in range(2):
        ad = zs[:, o + n * dr:o + (n + 1) * dr]
        a_ref[n] = _sigmoid(a0_ref[n:n + 1, :] + _mm(ad, aup_ref[n], 3))
    o += 2 * dr
    g_ref[...] = _mm(_sigmoid(zs[:, o:]), gup_ref[...], 3)


def _rwkv_feat(za, mu_p, w0, w_up, a0, a_up, g_up_p, nu):
    t, na = za.shape
    aw = w0.shape[1]
    dr = w_up.shape[1]
    tm = ROW_TILE
    nb8 = tm // 8
    last8 = t // 8 - 1
    row_spec = lambda w: pl.BlockSpec((tm, w), lambda i: (i, 0))
    full = lambda a: pl.BlockSpec(a.shape, lambda i: (0,) * a.ndim)
    out_rows = jax.ShapeDtypeStruct((t, aw), F32)
    out_dir = jax.ShapeDtypeStruct((2, t, aw), F32)
    dir_spec = pl.BlockSpec((2, tm, aw), lambda i: (0, i, 0))
    return pl.pallas_call(
        functools.partial(_rwkv_feat_kernel, nu=nu, t_all=t, tm=tm, aw=aw, dr=dr),
        grid=(t // tm,),
        in_specs=[row_spec(na),
                  pl.BlockSpec((8, na), lambda i: (jnp.maximum(i * nb8 - 1, 0), 0)),
                  pl.BlockSpec((8, na), lambda i: (jnp.minimum((i + 1) * nb8, last8), 0)),
                  full(mu_p), full(w0), full(w_up), full(a0), full(a_up), full(g_up_p)],
        out_specs=(row_spec(aw), row_spec(aw), row_spec(aw), dir_spec, dir_spec, row_spec(aw)),
        out_shape=(out_rows, out_rows, out_rows, out_dir, out_dir, out_rows),
        compiler_params=_cp(1), name="rwkv_feat",
    )(za, za, za, mu_p, w0, w_up, a0, a_up, g_up_p)


def _bmm(a, b, nt=False):
    cb = 2 if nt else 1
    return lax.dot_general(a.astype(BF16), b.astype(BF16), (((2,), (cb,)), ((0,), (0,))),
                           preferred_element_type=F32)


def _bmm_exact_lhs(m_bf16, x):
    dn = (((2,), (1,)), ((0,), (0,)))
    x1 = x.astype(BF16)
    r1 = x - x1.astype(F32)
    x2 = r1.astype(BF16)
    x3 = (r1 - x2.astype(F32)).astype(BF16)
    d = lambda y: lax.dot_general(m_bf16, y, dn, preferred_element_type=F32)
    return d(x1) + (d(x2) + d(x3))


def _bt(x):
    return jnp.stack([x[g].T for g in range(x.shape[0])], axis=0)


def _rwkv_chunk_math(r, k, v, e, a, kk_gain, ka_gain, reverse, groups):
    c = CHUNK
    hd = A_HEAD_DIM
    lane = _iota((1, LANES), 1)
    m_a = jnp.where(lane < hd, 1.0, 0.0)
    m_b = 1.0 - m_a
    ones_seg = _block_ones(LANES, hd)
    ti = _iota((groups, c, c), 1)
    tj = _iota((groups, c, c), 2)
    tri = jnp.where((tj >= ti) if reverse else (tj <= ti), 1.0, 0.0).astype(BF16)

    kk0 = k * kk_gain
    kk = kk0 / jnp.maximum(jnp.sqrt(_seg_sum(kk0 * kk0, ones_seg)), 1e-12)
    kt = k * (1.0 + (a - 1.0) * ka_gain)
    b = kk * a
    g3 = lambda x: x.reshape(groups, c, LANES)
    e3 = g3(e)
    cl = _bmm_exact_lhs(tri, e3)
    last = 0 if reverse else c - 1
    ctot = cl[:, last:last + 1, :]
    g_in = jnp.exp(-cl)
    g_ex = jnp.exp(e3 - cl)
    g_inv = jnp.exp(cl)
    g_end = jnp.exp(cl - ctot)
    st = lambda x: jnp.concatenate([x * m_a, x * m_b], axis=1)
    kk3, b3, kt3 = g3(kk), g3(b), g3(kt)
    kk2 = st(kk3 * g_ex)
    r2 = st(g3(r) * g_in)
    b2 = st(b3 * g_inv)
    k2 = st(kt3 * g_inv)
    v2 = st(g3(v))
    bg2 = st(b3 * g_end)
    kg2 = st(kt3 * g_end)

    s = _bmm(jnp.concatenate([kk2, r2], axis=1), jnp.concatenate([b2, k2], axis=1), nt=True)
    n2 = 2 * c
    i2 = _iota((n2, n2), 0)
    j2 = _iota((n2, n2), 1)
    il = i2 % c
    jl = j2 % c
    strict = (jl > il) if reverse else (jl < il)
    incl = (jl >= il) if reverse else (jl <= il)
    a_b = jnp.where(strict, s[:, :n2, :n2], 0.0)
    a_k = jnp.where(strict, s[:, :n2, n2:], 0.0)
    l_b = jnp.where(incl, s[:, n2:, :n2], 0.0)
    l_k = jnp.where(incl, s[:, n2:, n2:], 0.0)
    eye = jnp.where(i2 == j2, 1.0, 0.0)

    same = lambda m: (i2 // m) == (j2 // m)
    a_d = jnp.where(same(8), a_b, 0.0)
    a_d2 = _bmm(a_d, a_d)
    a_d4 = _bmm(a_d2, a_d2)
    t_inv = _bmm(_bmm(eye - a_d, eye + a_d2), eye + a_d4)
    m = 8
    while m < c:
        a_off = jnp.where(same(2 * m) & jnp.logical_not(same(m)), a_b, 0.0)
        t_inv = t_inv - _bmm(_bmm(t_inv, a_off), t_inv)
        m *= 2

    av = _bmm(a_k, v2)
    z12 = _bmm(t_inv, jnp.concatenate([kk2, av], axis=2))
    z1 = z12[:, :, :LANES]
    z2 = z12[:, :, LANES:]
    vz = jnp.concatenate([v2, z2], axis=1)
    p_bd = eye * jnp.exp(-ctot) - _bmm(_bt(bg2), z1)
    q_bd = _bmm(_bt(jnp.concatenate([kg2, -bg2], axis=1)), vz)
    r2s = r2 - _bmm(l_b, z1)
    y0s = _bmm(jnp.concatenate([l_k, -l_b], axis=2), vz)
    cp = lambda x: x[:, :x.shape[1] // 2] + x[:, x.shape[1] // 2:]
    return cp(p_bd), cp(q_bd), cp(r2s).reshape(groups * c, LANES), cp(y0s).reshape(groups * c, LANES)


def _rwkv_chunk_kernel(r_ref, k_ref, v_ref, e_ref, a_ref, kkg_ref, kag_ref,
                       p_ref, q_ref, r2_ref, y0_ref, *, reverse, groups):
    p, q, r2, y0 = _rwkv_chunk_math(r_ref[...], k_ref[...], v_ref[...], e_ref[0], a_ref[0],
                                    kkg_ref[...], kag_ref[...], reverse, groups)
    p_ref[0] = p
    q_ref[0] = q
    r2_ref[...] = r2
    y0_ref[...] = y0


def _rwkv_chunks(r, k, v, e, a, k_k, k_a, direction):
    t, aw = r.shape
    npair = aw // LANES
    nc = t // CHUNK
    groups = _pick(nc, (10, 4))
    rows = groups * CHUNK
    row_spec = pl.BlockSpec((rows, LANES), lambda p, i: (i, p))
    dir_spec = pl.BlockSpec((1, rows, LANES), lambda p, i: (direction, i, p))
    par_spec = pl.BlockSpec((1, LANES), lambda p, i: (0, p))
    pq_spec = pl.BlockSpec((1, groups, A_HEAD_DIM, LANES), lambda p, i: (p, i, 0, 0))
    pq_shape = jax.ShapeDtypeStruct((npair, nc, A_HEAD_DIM, LANES), F32)
    ry_shape = jax.ShapeDtypeStruct((t, aw), F32)
    return pl.pallas_call(
        functools.partial(_rwkv_chunk_kernel, reverse=bool(direction), groups=groups),
        grid=(npair, t // rows),
        in_specs=[row_spec, row_spec, row_spec, dir_spec, dir_spec, par_spec, par_spec],
        out_specs=(pq_spec, pq_spec, row_spec, row_spec),
        out_shape=(pq_shape, pq_shape, ry_shape, ry_shape),
        compiler_params=_cp(2), name="rwkv_chunks_bwd" if direction else "rwkv_chunks_fwd",
    )(r, k, v, e, a, k_k.reshape(1, aw), k_a.reshape(1, aw))


def _pair_block_diag(x):
    lane = _iota(x.shape, 1)
    return jnp.concatenate([jnp.where(lane < A_HEAD_DIM, x, 0.0), jnp.where(lane >= A_HEAD_DIM, x, 0.0)], axis=0)


def _rwkv_seq_kernel(p_ref, q_ref, r2_ref, y0_ref, y_ref, h_ref, *, reverse, groups, npair):
    c = CHUNK

    @pl.when(pl.program_id(0) == 0)
    def _():
        h_ref[...] = jnp.zeros_like(h_ref)

    order = range(groups - 1, -1, -1) if reverse else range(groups)
    for g in order:
        sl = slice(g * c, (g + 1) * c)
        for p in range(npair):
            ls = slice(p * LANES, (p + 1) * LANES)
            h = h_ref[p]
            y_ref[sl, ls] = _mm(r2_ref[sl, ls], h, RWKV_SEQ_PASSES) + y0_ref[sl, ls]
            h_ref[p] = _mm(_pair_block_diag(p_ref[p, g]), h, RWKV_SEQ_PASSES) + _pair_block_diag(q_ref[p, g])


def _rwkv_seq(p, q, r2, y0, direction, nu):
    npair, nc = p.shape[:2]
    t, aw = r2.shape
    groups = SEQ_GROUP
    rows = groups * CHUNK
    nb = t // rows
    nbu = nu // rows
    assert nu % rows == 0
    if direction:
        blk = lambda s: jnp.where(s < nbu, nbu - 1 - s, nb + nbu - 1 - s)
    else:
        blk = lambda s: s
    pq_spec = pl.BlockSpec((npair, groups, A_HEAD_DIM, LANES), lambda s: (0, blk(s), 0, 0))
    row_spec = pl.BlockSpec((rows, aw), lambda s: (blk(s), 0))
    return pl.pallas_call(
        functools.partial(_rwkv_seq_kernel, reverse=bool(direction), groups=groups, npair=npair),
        grid=(nb,),
        in_specs=[pq_spec, pq_spec, row_spec, row_spec],
        out_specs=row_spec,
        out_shape=jax.ShapeDtypeStruct((t, aw), F32),
        scratch_shapes=[pltpu.VMEM((npair, LANES, LANES), F32)],
        compiler_params=_cp(1), name="rwkv_seq_bwd" if direction else "rwkv_seq_fwd",
    )(p, q, r2, y0)


def _rwkv_out_kernel(yf_ref, yb_ref, r_ref, k_ref, v_ref, a_ref, g_ref, lw_ref, lb_ref, rk_ref, ka_ref, o_ref, *, aw):
    hd = A_HEAD_DIM
    ones_seg = _block_ones(LANES, hd)
    for p in range(aw // LANES):
        ls = slice(p * LANES, (p + 1) * LANES)
        y = yf_ref[:, ls] + yb_ref[:, ls]
        mean = _seg_sum(y, ones_seg) * (1.0 / hd)
        yc = y - mean
        var = _seg_sum(yc * yc, ones_seg) * (1.0 / hd)
        yn = yc * lax.rsqrt(var + A_LNX_EPS) * lw_ref[:, ls] + lb_ref[:, ls]
        r = r_ref[:, ls]
        k = k_ref[:, ls]
        v = v_ref[:, ls]
        bonus = jnp.zeros_like(y)
        for n in range(2):
            kt = k * (1.0 + (a_ref[n, :, ls] - 1.0) * ka_ref[:, ls])
            bonus = bonus + _seg_sum(r * kt * rk_ref[:, ls], ones_seg) * v
        o_ref[:, ls] = ((yn + bonus) * g_ref[:, ls]).astype(o_ref.dtype)


def _rwkv_out(yf, yb, r, k, v, a, g, lnx_w, lnx_b, r_k, k_a):
    t, aw = r.shape
    tm = ROW_TILE
    row_spec = pl.BlockSpec((tm, aw), lambda i: (i, 0))
    par_spec = pl.BlockSpec((1, aw), lambda i: (0, 0))
    return pl.pallas_call(
        functools.partial(_rwkv_out_kernel, aw=aw),
        grid=(t // tm,),
        in_specs=[row_spec] * 5 + [pl.BlockSpec((2, tm, aw), lambda i: (0, i, 0)), row_spec] + [par_spec] * 4,
        out_specs=row_spec,
        out_shape=jax.ShapeDtypeStruct((t, aw), BF16),
        compiler_params=_cp(1), name="rwkv_out",
    )(yf, yb, r, k, v, a, g, lnx_w.reshape(1, aw), lnx_b.reshape(1, aw), r_k.reshape(1, aw), k_a.reshape(1, aw))


def _rms_rope_slab(x, gain, cos, sin, ones_seg):
    ms = _seg_sum(x * x, ones_seg) * (1.0 / B_HEAD_DIM)
    y = x * lax.rsqrt(ms + NORM_EPS) * gain
    return y * cos + _swap16(y) * sin


def _gqa_prep_kernel(z_ref, cos_ref, sin_ref, qg_ref, kg_ref, o_ref, *, qw, kw):
    ones_seg = _block_ones(LANES, B_HEAD_DIM)
    cos = cos_ref[...]
    sin = sin_ref[...]
    scale = B_HEAD_DIM ** -0.5
    for s in range(qw // LANES):
        ls = slice(s * LANES, (s + 1) * LANES)
        o_ref[:, ls] = (_rms_rope_slab(z_ref[:, ls], qg_ref[...], cos, sin, ones_seg) * scale).astype(o_ref.dtype)
    for s in range(kw // LANES):
        ls = slice(qw + s * LANES, qw + (s + 1) * LANES)
        o_ref[:, ls] = _rms_rope_slab(z_ref[:, ls], kg_ref[...], cos, sin, ones_seg).astype(o_ref.dtype)
    o_ref[:, qw + kw:] = z_ref[:, qw + kw:].astype(o_ref.dtype)


def _gqa_prep(zb, cos, sin, q_norm, k_norm):
    t, nb = zb.shape
    tm = ROW_TILE
    qw = B_Q_HEADS * B_HEAD_DIM
    kw = B_KV_HEADS * B_HEAD_DIM
    tile2 = lambda g: jnp.tile(g.reshape(1, B_HEAD_DIM), (1, LANES // B_HEAD_DIM))
    return pl.pallas_call(
        functools.partial(_gqa_prep_kernel, qw=qw, kw=kw),
        grid=(t // tm,),
        in_specs=[pl.BlockSpec((tm, nb), lambda i: (i, 0)),
                  pl.BlockSpec((tm, LANES), lambda i: (i, 0)),
                  pl.BlockSpec((tm, LANES), lambda i: (i, 0)),
                  pl.BlockSpec((1, LANES), lambda i: (0, 0)),
                  pl.BlockSpec((1, LANES), lambda i: (0, 0))],
        out_specs=pl.BlockSpec((tm, nb), lambda i: (i, 0)),
        out_shape=jax.ShapeDtypeStruct((t, nb), BF16),
        compiler_params=_cp(1), name="gqa_prep",
    )(zb, cos, sin, tile2(q_norm), tile2(k_norm))


def _gqa_attn_kernel(sink_ref, q_ref, c_ref, kp_ref, ko_ref, kn_ref, o_ref, *, nu, seq, qw, kw):
    hd = B_HEAD_DIM
    grp = B_Q_HEADS // B_KV_HEADS
    blk = BLOCK
    j = pl.program_id(0)
    jb = j - nu // blk
    nkeys = nu + 3 * blk
    rows = grp * blk
    qi = _iota((rows, nkeys), 0) % blk
    kc = _iota((rows, nkeys), 1)
    q_pos = jb * blk + qi
    k_pos = (jb - 1) * blk + (kc - nu)
    band_ok = (jnp.abs(k_pos - q_pos) <= WINDOW) & (k_pos >= 0) & (k_pos < seq) & (jb >= 0)
    valid = (kc < nu) | band_ok
    rg = _iota((rows, 1), 0) // blk
    outs = []
    for h in range(B_KV_HEADS):
        ks = slice(qw + h * hd, qw + (h + 1) * hd)
        vs = slice(qw + kw + h * hd, qw + kw + (h + 1) * hd)
        k_all = jnp.concatenate([c_ref[:, ks], kp_ref[:, ks], ko_ref[:, ks], kn_ref[:, ks]], axis=0)
        v_all = jnp.concatenate([c_ref[:, vs], kp_ref[:, vs], ko_ref[:, vs], kn_ref[:, vs]], axis=0)
        q4 = jnp.concatenate([q_ref[:, (h * grp + g) * hd:(h * grp + g + 1) * hd] for g in range(grp)], axis=0)
        s = jnp.where(valid, _dg(q4, k_all, 1, 1), NEG_INF)
        sink = jnp.zeros((rows, 1), F32)
        for g in range(grp):
            sink = jnp.where(rg == g, sink_ref[h * grp + g], sink)
        m = jnp.maximum(jnp.max(s, axis=1, keepdims=True), sink)
        p = jnp.exp(s - m)
        den = jnp.sum(p, axis=1, keepdims=True) + jnp.exp(sink - m)
        o = _dg(p.astype(BF16), v_all) / den
        outs += [o[g * blk:(g + 1) * blk, :] for g in range(grp)]
    o_ref[...] = jnp.concatenate(outs, axis=1).astype(o_ref.dtype)


def _gqa_attn(qkv, sink, nu, seq):
    t, nb = qkv.shape
    qw = B_Q_HEADS * B_HEAD_DIM
    kw = B_KV_HEADS * B_HEAD_DIM
    blk = BLOCK
    nblk = t // blk
    band = lambda off: pl.BlockSpec((blk, nb), lambda j: (jnp.clip(j + off, 0, nblk - 1), 0))
    return pl.pallas_call(
        functools.partial(_gqa_attn_kernel, nu=nu, seq=seq, qw=qw, kw=kw),
        grid=(nblk,),
        in_specs=[pl.BlockSpec(memory_space=pltpu.SMEM),
                  pl.BlockSpec((blk, nb), lambda j: (j, 0)),
                  pl.BlockSpec((nu, nb), lambda j: (0, 0)),
                  band(-1), band(0), band(1)],
        out_specs=pl.BlockSpec((blk, qw), lambda j: (j, 0)),
        out_shape=jax.ShapeDtypeStruct((t, qw), BF16),
        compiler_params=_cp(1), name="gqa_attn",
    )(sink, qkv, qkv, qkv, qkv, qkv)


def _mla_prep_kernel(q_ref, kv_ref, kr_ref, cos_ref, sin_ref, nn_ref, rn_ref, qo_ref, ko_ref, vo_ref):
    cos = cos_ref[...]
    sin = sin_ref[...]
    dn = C_NOPE_DIM
    hw = 2 * LANES
    scale = (C_NOPE_DIM + C_ROPE_DIM) ** -0.5 * math.log2(math.e)
    ones = jnp.ones((q_ref.shape[0], C_V_DIM), vo_ref.dtype)

    def rms_rope(x, gain):
        ms = jnp.sum(x * x, axis=-1, keepdims=True) * (1.0 / C_ROPE_DIM)
        y = x * lax.rsqrt(ms + NORM_EPS) * gain
        return y * cos + _swap16(y) * sin

    def rms(x, gain):
        return x * lax.rsqrt(jnp.mean(x * x, axis=-1, keepdims=True) + NORM_EPS) * gain

    kr = rms_rope(kr_ref[...], rn_ref[1:2, :]).astype(ko_ref.dtype)
    for h in range(C_HEADS):
        qn = rms(q_ref[:, h * hw:h * hw + dn], nn_ref[0:1, :])
        qr = rms_rope(q_ref[:, h * hw + dn:(h + 1) * hw], rn_ref[0:1, :])
        qo_ref[:, h * hw:h * hw + dn] = (qn * scale).astype(qo_ref.dtype)
        qo_ref[:, h * hw + dn:(h + 1) * hw] = (qr * scale).astype(qo_ref.dtype)
        ko_ref[:, h * hw:h * hw + dn] = rms(kv_ref[:, h * dn:(h + 1) * dn], nn_ref[1:2, :]).astype(ko_ref.dtype)
        ko_ref[:, h * hw + dn:(h + 1) * hw] = kr
        vo_ref[:, h * hw:h * hw + C_V_DIM] = kv_ref[:, (C_HEADS + h) * dn:(C_HEADS + h + 1) * dn].astype(vo_ref.dtype)
        vo_ref[:, h * hw + C_V_DIM:(h + 1) * hw] = ones


def _mla_prep(q, kv, zc, kr_col_block, cos, sin, nope_norm, rope_norm_p):
    t = q.shape[0]
    tm = ROW_TILE
    hw = 2 * LANES
    row = lambda w: pl.BlockSpec((tm, w), lambda i: (i, 0))
    full = lambda a: pl.BlockSpec(a.shape, lambda i: (0, 0))
    return pl.pallas_call(
        _mla_prep_kernel,
        grid=(t // tm,),
        in_specs=[row(q.shape[1]), row(kv.shape[1]),
                  pl.BlockSpec((tm, LANES), lambda i: (i, kr_col_block)),
                  row(LANES), row(LANES), full(nope_norm), full(rope_norm_p)],
        out_specs=(row(C_HEADS * hw),) * 3,
        out_shape=(jax.ShapeDtypeStruct((t, C_HEADS * hw), BF16),) * 3,
        compiler_params=_cp(1), name="mla_prep",
    )(q, kv, zc, cos, sin, nope_norm, rope_norm_p)


def _mla_flash_kernel(q_ref, k_ref, v_ref, o_ref, m_ref, acc_ref, *, nu, tq, tk, sub):
    qi = pl.program_id(1)
    ki = pl.program_id(2)

    @pl.when(ki == 0)
    def _():
        m_ref[...] = jnp.full_like(m_ref, NEG_INF)
        acc_ref[...] = jnp.zeros_like(acc_ref)

    def update(masked):
        for qs in range(tq // sub):
            rs = slice(qs * sub, (qs + 1) * sub)
            s = _dg(q_ref[rs, :], k_ref[...], 1, 1)
            if masked:
                qrow = qi * tq + qs * sub + _iota((sub, tk), 0)
                kcol = ki * tk + _iota((sub, tk), 1)
                s = jnp.where((qrow < nu) & (kcol >= nu), NEG_INF, s)
            m_prev = m_ref[rs, :]
            m_next = jnp.maximum(m_prev, jnp.max(s, axis=1, keepdims=True))
            alpha = jnp.exp2(m_prev - m_next)
            p = jnp.exp2((s - m_next[:, 0:1]).astype(BF16))
            acc_ref[rs, :] = jnp.concatenate([alpha, alpha], axis=1) * acc_ref[rs, :] + _dg(p, v_ref[...])
            m_ref[rs, :] = m_next

    has_ctx_rows = qi * tq < nu

    @pl.when(has_ctx_rows)
    def _():
        update(True)

    @pl.when(jnp.logical_not(has_ctx_rows))
    def _():
        update(False)

    @pl.when(ki == pl.num_programs(2) - 1)
    def _():
        o_ref[...] = (acc_ref[:, :C_V_DIM] / acc_ref[:, C_V_DIM:]).astype(o_ref.dtype)


def _mla_flash(qf, kf, vf, nu):
    t = qf.shape[0]
    hw = 2 * LANES
    tq = _pick(t, (1280, 256))
    tk = _pick(t, (1280, 256))
    return pl.pallas_call(
        functools.partial(_mla_flash_kernel, nu=nu, tq=tq, tk=tk, sub=256),
        grid=(C_HEADS, t // tq, t // tk),
        in_specs=[pl.BlockSpec((tq, hw), lambda h, i, j: (i, h)),
                  pl.BlockSpec((tk, hw), lambda h, i, j: (j, h)),
                  pl.BlockSpec((tk, hw), lambda h, i, j: (j, h))],
        out_specs=pl.BlockSpec((tq, C_V_DIM), lambda h, i, j: (i, h)),
        out_shape=jax.ShapeDtypeStruct((t, C_HEADS * C_V_DIM), BF16),
        scratch_shapes=[pltpu.VMEM((tq, LANES), F32), pltpu.VMEM((tq, hw), F32)],
        compiler_params=_cp(3), name="mla_flash",
    )(qf, kf, vf)


def _merge_kernel(ya_ref, yb_ref, yc_ref, w_ref, g0_ref, g1_ref, g2_ref, o_ref):
    acc = g0_ref[...].astype(F32) * _dg(ya_ref[...], w_ref[0])
    acc = acc + g1_ref[...].astype(F32) * _dg(yb_ref[...], w_ref[1])
    acc = acc + g2_ref[...].astype(F32) * _dg(yc_ref[...], w_ref[2])
    o_ref[...] = acc.astype(o_ref.dtype)


def _merge(ya, yb, yc, w_branch, gates):
    t, bw = ya.shape
    d = w_branch.shape[2]
    tm = _pick(t, (640, 256, 128))
    tn = _pick(d, (1024, 512, 256, 128))
    nj = d // tn
    y_spec = pl.BlockSpec((tm, bw), lambda j, i: (i, 0))
    gate = lambda b: pl.BlockSpec((tm, tn), lambda j, i: (i, b * nj + j))
    return pl.pallas_call(
        _merge_kernel,
        grid=(nj, t // tm),
        in_specs=[y_spec, y_spec, y_spec, pl.BlockSpec((N_BRANCH, bw, tn), lambda j, i: (0, 0, j)),
                  gate(0), gate(1), gate(2)],
        out_specs=pl.BlockSpec((tm, tn), lambda j, i: (i, j)),
        out_shape=jax.ShapeDtypeStruct((t, d), BF16),
        compiler_params=_cp(2), name="merge",
    )(ya, yb, yc, w_branch, gates, gates, gates)


def _moe_select_kernel(aff_ref, loc_ref, base_ref, *, cap, n_exp):
    aff = aff_ref[...]
    bits = pltpu.bitcast(aff, I32)
    count = lambda m: jnp.sum(jnp.sum(jnp.where(m, 1.0, 0.0), axis=1, keepdims=True), axis=2, keepdims=True)
    theta = jnp.zeros((n_exp, 1, 1), I32)
    for bit in range(30, -1, -1):
        cand = theta | (1 << bit)
        theta = jnp.where(count(bits >= cand) >= cap, cand, theta)
    gt = bits > theta
    eq = bits == theta
    need = cap - count(gt)
    r = aff.shape[1]
    upper = jnp.where(_iota((LANES, LANES), 0) <= _iota((LANES, LANES), 1), 1.0, 0.0).astype(BF16)
    lower_strict = jnp.where(_iota((r, r), 1) < _iota((r, r), 0), 1.0, 0.0).astype(BF16)

    def prefix(x):
        incl = _dg(x.astype(BF16), upper)
        tot = jnp.broadcast_to(incl[:, LANES - 1:LANES], (r, LANES))
        return incl, _dg(lower_strict, tot.astype(BF16))

    for e in range(n_exp):
        xe = jnp.where(eq[e], 1.0, 0.0)
        incl, base = prefix(xe)
        take = eq[e] & ((incl - xe + base) < need[e])
        sel = jnp.where(gt[e] | take, 1.0, 0.0)
        incl, base = prefix(sel)
        loc_ref[e] = jnp.where(sel > 0.0, incl - sel, -4096.0)
        base_ref[e] = base


def _moe_select(aff3, cap):
    e, r, _ = aff3.shape
    spec = pl.BlockSpec((e, r, LANES), lambda i: (0, 0, 0))
    shape = jax.ShapeDtypeStruct((e, r, LANES), F32)
    return pl.pallas_call(
        functools.partial(_moe_select_kernel, cap=cap, n_exp=e),
        grid=(1,), in_specs=[spec], out_specs=(spec, spec), out_shape=(shape, shape),
        compiler_params=_cp(1), name="moe_select",
    )(aff3)


def _window_start(base_ref, e, i):
    b = base_ref[e, i]
    a = (b // MOE_ALIGN) * MOE_ALIGN
    return pl.multiple_of(a, MOE_ALIGN), b - a


def _moe_gather_kernel(base_ref, loc_ref, aff_ref, h_ref, xe_ref, ge_ref, *, tiles):
    e = pl.program_id(0)
    i = pl.program_id(1)

    @pl.when(i == 0)
    def _():
        xe_ref[...] = jnp.zeros_like(xe_ref)
        ge_ref[...] = jnp.zeros_like(ge_ref)

    win_row = _iota((MOE_WIN, MOE_TILE), 0).astype(F32)
    for r in range(tiles):
        a, off = _window_start(base_ref, e, i * tiles + r)
        slot = loc_ref[0, r] + off.astype(F32)
        onehot = jnp.where(win_row == slot, 1.0, 0.0)
        rows = _dg(onehot.astype(BF16), h_ref[r * MOE_TILE:(r + 1) * MOE_TILE, :])
        head = pl.ds(a, MOE_ALIGN)
        xe_ref[0, head, :] = (xe_ref[0, head, :].astype(F32) + rows[:MOE_ALIGN]).astype(xe_ref.dtype)
        xe_ref[0, pl.ds(a + MOE_ALIGN, MOE_TILE), :] = rows[MOE_ALIGN:].astype(xe_ref.dtype)
        win = pl.ds(a, MOE_WIN)
        gsel = jnp.sum(onehot * aff_ref[0, r], axis=1, keepdims=True)
        ge_ref[0, win, :] = ge_ref[0, win, :] + gsel


def _moe_gather(base_i, loc, aff3, h_all, row0, n, cap_p):
    e = loc.shape[0]
    d = h_all.shape[1]
    nt = n // MOE_TILE
    tiles = _pick(nt, (4, 2, 1))
    rows = tiles * MOE_TILE
    sel_spec = pl.BlockSpec((1, tiles, 1, LANES), lambda ee, i, b: (ee, i, 0, 0))
    grid_spec = pltpu.PrefetchScalarGridSpec(
        num_scalar_prefetch=1, grid=(e, nt // tiles),
        in_specs=[sel_spec, sel_spec,
                  pl.BlockSpec((pl.Element(rows), pl.Element(d)),
                               lambda ee, i, b: (pl.multiple_of(row0 + i * rows, MOE_TILE), 0))],
        out_specs=(pl.BlockSpec((1, cap_p, d), lambda ee, i, b: (ee, 0, 0)),
                   pl.BlockSpec((1, cap_p, LANES), lambda ee, i, b: (ee, 0, 0))))
    return pl.pallas_call(
        functools.partial(_moe_gather_kernel, tiles=tiles), grid_spec=grid_spec,
        out_shape=(jax.ShapeDtypeStruct((e, cap_p, d), BF16), jax.ShapeDtypeStruct((e, cap_p, LANES), F32)),
        compiler_params=_cp(2), name="moe_gather",
    )(base_i, loc, aff3, h_all)


def _moe_ffn_kernel(x_ref, g_ref, wg_ref, wu_ref, wd_ref, y_ref):
    x = x_ref[0]
    hg = _dg(x, wg_ref[0])
    hid = hg * _sigmoid(hg) * _dg(x, wu_ref[0])
    y = _dg(hid.astype(BF16), wd_ref[0]) * g_ref[0][:, 0:1]
    y_ref[0] = y.astype(y_ref.dtype)


def _moe_ffn(xe, ge, w_gate, w_up, w_down, tc):
    e, cap_p, d = xe.shape
    f = w_gate.shape[2]
    return pl.pallas_call(
        _moe_ffn_kernel,
        grid=(e, cap_p // tc),
        in_specs=[pl.BlockSpec((1, tc, d), lambda ee, i: (ee, i, 0)),
                  pl.BlockSpec((1, tc, LANES), lambda ee, i: (ee, i, 0)),
                  pl.BlockSpec((1, d, f), lambda ee, i: (ee, 0, 0)),
                  pl.BlockSpec((1, d, f), lambda ee, i: (ee, 0, 0)),
                  pl.BlockSpec((1, f, d), lambda ee, i: (ee, 0, 0))],
        out_specs=pl.BlockSpec((1, tc, d), lambda ee, i: (ee, i, 0)),
        out_shape=jax.ShapeDtypeStruct((e, cap_p, d), BF16),
        compiler_params=_cp(2), name="moe_ffn",
    )(xe, ge, w_gate, w_up, w_down)


def _moe_combine_kernel(base_ref, loc_ref, *rest, n_exp):
    y_refs = rest[:n_exp]
    x_ref, gate_ref, o_ref = rest[n_exp:]
    i = pl.program_id(0)
    win_row = _iota((MOE_TILE, MOE_WIN), 1).astype(F32)
    acc = None
    for e in range(n_exp):
        _, off = _window_start(base_ref, e, i)
        slot_row = jnp.broadcast_to(loc_ref[e, 0] + off.astype(F32), (MOE_TILE, LANES))
        slot_col = slot_row.T
        slot_col = jnp.concatenate([slot_col, slot_col[:, :MOE_WIN - LANES]], axis=1)
        onehot = jnp.where(win_row == slot_col, 1.0, 0.0).astype(BF16)
        part = _dg(onehot, y_refs[e][...])
        acc = part if acc is None else acc + part
    o_ref[...] = x_ref[...] + gate_ref[...] * acc


def _moe_combine(base_i, loc, ye, x_all, gate_row, row0, n, rows_only):
    e, cap_p, d = ye.shape
    nt = n // MOE_TILE
    t0 = row0 // MOE_TILE
    o0 = 0 if rows_only else t0
    dcol = min(d, 2048)

    def y_spec(ee):
        def index(i, j, b):
            start = ee * cap_p + (b[ee, i] // MOE_ALIGN) * MOE_ALIGN
            return pl.multiple_of(start, MOE_ALIGN), pl.multiple_of(j * dcol, LANES)
        return pl.BlockSpec((pl.Element(MOE_WIN), pl.Element(dcol)), index)

    grid_spec = pltpu.PrefetchScalarGridSpec(
        num_scalar_prefetch=1, grid=(nt, d // dcol),
        in_specs=[pl.BlockSpec((e, 1, 1, LANES), lambda i, j, b: (0, i, 0, 0))]
                 + [y_spec(ee) for ee in range(e)]
                 + [pl.BlockSpec((MOE_TILE, dcol), lambda i, j, b: (t0 + i, j)),
                    pl.BlockSpec((1, dcol), lambda i, j, b: (0, j))],
        out_specs=pl.BlockSpec((MOE_TILE, dcol), lambda i, j, b: (o0 + i, j)))
    x_index = 2 + e
    return pl.pallas_call(
        functools.partial(_moe_combine_kernel, n_exp=e), grid_spec=grid_spec,
        out_shape=jax.ShapeDtypeStruct((n, d) if rows_only else x_all.shape, F32),
        input_output_aliases={} if rows_only else {x_index: 0},
        compiler_params=_cp(2), name="moe_combine",
    )(base_i, loc, *([ye.reshape(e * cap_p, d)] * e), x_all, gate_row)


def _moe_stream(x_all, h_all, aff, row0, n, gate_row, w_gate, w_up, w_down, rows_only=False):
    e = aff.shape[0]
    cap = (CAPACITY_FACTOR * n) // e
    tc = 256 if cap >= 256 else 64
    cap_p = -(-(cap + MOE_WIN) // tc) * tc
    n_sel = SEL_ROWS * LANES
    aff_s = lax.dynamic_slice_in_dim(aff, row0, n, axis=1)
    aff3 = jnp.pad(aff_s, ((0, 0), (0, n_sel - n))).reshape(e, SEL_ROWS, LANES)
    loc, base = _moe_select(aff3, cap)
    base_i = base[:, :, 0].astype(I32)
    loc4 = loc.reshape(e, SEL_ROWS, 1, LANES)
    aff4 = aff3.reshape(e, SEL_ROWS, 1, LANES)
    xe, ge = _moe_gather(base_i, loc4, aff4, h_all, row0, n, cap_p)
    ye = _moe_ffn(xe, ge, w_gate, w_up, w_down, tc)
    return _moe_combine(base_i, loc4, ye, x_all, gate_row, row0, n, rows_only)


def _rope_tables(nu, seq):
    n_freq = ROPE_DIM // 4
    pos = jnp.arange(seq)
    inv = jnp.power(ROPE_THETA, -jnp.arange(n_freq, dtype=F32) / n_freq)
    ang_r = (pos // GRID_W).astype(F32)[:, None] * inv[None]
    ang_c = (pos % GRID_W).astype(F32)[:, None] * inv[None]
    cos = jnp.concatenate([jnp.cos(ang_r)] * 2 + [jnp.cos(ang_c)] * 2, axis=1)
    sin = jnp.concatenate([-jnp.sin(ang_r), jnp.sin(ang_r), -jnp.sin(ang_c), jnp.sin(ang_c)], axis=1)
    cos = jnp.concatenate([jnp.ones((nu, ROPE_DIM), F32), cos], axis=0)
    sin = jnp.concatenate([jnp.zeros((nu, ROPE_DIM), F32), sin], axis=0)
    rep = LANES // ROPE_DIM
    return jnp.tile(cos, (1, rep)), jnp.tile(sin, (1, rep))


def _pad_cols(w, n):
    return jnp.pad(w, ((0, 0), (0, n - w.shape[1])))


def _round_up(x, m):
    return -(-x // m) * m


def kernel(x, c, ctx, c_ctx, norm_mix, norm_ffn, ada_down, ada_up, ada_bias, w_in, rwkv_mu, rwkv_w0, rwkv_w_up, rwkv_a0, rwkv_a_up, rwkv_g_up, rwkv_k_k, rwkv_k_a, rwkv_r_k, rwkv_lnx_w, rwkv_lnx_b, gqa_q_norm, gqa_k_norm, gqa_sink, mla_q_a_norm, mla_q_up, mla_kv_a_norm, mla_kv_up, mla_nope_norm, mla_rope_norm, w_branch, w_out, moe_router, moe_w_gate, moe_w_up, moe_w_down):
    bsz, seq, d = x.shape
    assert bsz == 1
    nu = ctx.shape[1]
    depth = w_in.shape[0]
    aw = A_HEADS * A_HEAD_DIM
    dr = rwkv_w_up.shape[2]
    gr = rwkv_g_up.shape[1]
    n_a = 3 * aw + 4 * dr + gr
    n_a_p = _round_up(n_a, LANES)
    n_b = (B_Q_HEADS + 2 * B_KV_HEADS) * B_HEAD_DIM
    cq = mla_q_up.shape[1]
    ckv = mla_kv_up.shape[1]
    n_c = cq + ckv + C_ROPE_DIM
    n_c_p = cq + ckv + LANES
    hw = 2 * LANES

    x_all = jnp.concatenate([ctx[0], x[0]], axis=0)
    cond8 = jnp.zeros((8, d), F32).at[0].set(c_ctx).at[1].set(c[0])
    cos, sin = _rope_tables(nu, seq)

    for i in range(depth):
        wi = w_in[i]
        w_a = _pad_cols(wi[:, :n_a], n_a_p).astype(BF16)
        w_b = wi[:, n_a:n_a + n_b].astype(BF16)
        w_c = _pad_cols(wi[:, n_a + n_b:n_a + n_b + n_c], n_c_p).astype(BF16)
        w_g = wi[:, n_a + n_b + n_c:].astype(BF16)
        mu_p = _pad_cols(rwkv_mu[i], n_a_p)
        g_up_p = jnp.pad(rwkv_g_up[i], ((0, n_a_p - n_a), (0, 0)))
        qu = mla_q_up[i].reshape(cq, C_HEADS, C_NOPE_DIM + C_ROPE_DIM)
        qu = jnp.pad(qu, ((0, 0), (0, 0), (0, hw - C_NOPE_DIM - C_ROPE_DIM))).reshape(cq, C_HEADS * hw).astype(BF16)
        kvu = mla_kv_up[i].reshape(ckv, C_HEADS, C_NOPE_DIM + C_V_DIM)
        kvu = jnp.concatenate([kvu[:, :, :C_NOPE_DIM].reshape(ckv, -1), kvu[:, :, C_NOPE_DIM:].reshape(ckv, -1)],
                              axis=1).astype(BF16)
        rope_norm_p = _pad_cols(mla_rope_norm[i], LANES)

        mod = _adaln(cond8, ada_down[i], ada_up[i], ada_bias[i])[:2].reshape(2, N_MOD, d)

        h = _norm_mod(x_all, norm_mix[i], mod[:, 0], mod[:, 1], nu)
        za = _matmul(h, w_a, F32, name="w_in_a")
        zb = _matmul(h, w_b, F32, name="w_in_b")
        zc = _matmul(h, w_c, F32, name="w_in_c")
        gates = _matmul(h, w_g, BF16, act="sigmoid", name="w_in_gate")

        r, k, v, e, a, g = _rwkv_feat(za, mu_p, rwkv_w0[i], rwkv_w_up[i], rwkv_a0[i], rwkv_a_up[i], g_up_p, nu)
        ys = []
        for direction in (0, 1):
            p_, q_, r2, y0 = _rwkv_chunks(r, k, v, e, a, rwkv_k_k[i], rwkv_k_a[i], direction)
            ys.append(_rwkv_seq(p_, q_, r2, y0, direction, nu))
        ya = _rwkv_out(ys[0], ys[1], r, k, v, a, g, rwkv_lnx_w[i], rwkv_lnx_b[i], rwkv_r_k[i], rwkv_k_a[i])

        qkv_b = _gqa_prep(zb, cos, sin, gqa_q_norm[i], gqa_k_norm[i])
        yb = _gqa_attn(qkv_b, gqa_sink[i], nu, seq)

        assert cq % ckv == 0
        q_c = _matmul(zc, qu, F32, rms_gain=mla_q_a_norm[i], name="mla_q_up")
        kv_c = _matmul(zc, kvu, F32, a_col_block=cq // ckv, rms_gain=mla_kv_a_norm[i], name="mla_kv_up")
        qf, kf, vf = _mla_prep(q_c, kv_c, zc, (cq + ckv) // LANES, cos, sin, mla_nope_norm[i], rope_norm_p)
        yc = _mla_flash(qf, kf, vf, nu)

        merged = _merge(ya, yb, yc, w_branch[i].astype(BF16), gates)
        x_all = _matmul(merged, w_out[i].astype(BF16), F32, resid=x_all, gate2=mod[:, 2], nu=nu, name="w_out")

        hf, aff = _norm_mod(x_all, norm_ffn[i], mod[:, 3], mod[:, 4], nu, router_t=moe_router[i].T)
        wg = moe_w_gate[i].astype(BF16)
        wu = moe_w_up[i].astype(BF16)
        wd = moe_w_down[i].astype(BF16)
        last = i == depth - 1
        x_all = _moe_stream(x_all, hf, aff, nu, seq, mod[1:2, 5], wg, wu, wd, rows_only=last)
        if not last:
            x_all = _moe_stream(x_all, hf, aff, 0, nu, mod[0:1, 5], wg, wu, wd)
    return x_all.reshape(bsz, seq, d)
```

```python
import functools
import math

import jax
import jax.numpy as jnp
import numpy as np
from jax import lax
from jax.experimental import pallas as pl
from jax.experimental.pallas import tpu as pltpu

F32 = jnp.float32
BF16 = jnp.bfloat16
I32 = jnp.int32

GRID_W = 64
ROPE_DIM = 64
ROPE_THETA = 10000.0
NORM_EPS = 1e-6
NEG_INF = -1e30
N_MOD = 6
A_HEADS = 16
A_HEAD_DIM = 64
A_LNX_EPS = 64e-5
B_Q_HEADS = 16
B_KV_HEADS = 4
B_HEAD_DIM = 64
WINDOW = 128
BLOCK = 128
C_HEADS = 8
C_NOPE_DIM = 128
C_ROPE_DIM = 64
C_V_DIM = 128
N_BRANCH = 3
CAPACITY_FACTOR = 2

LANES = 128
CHUNK = 64
SEQ_GROUP = 4
ROW_TILE = 256
MOE_TILE = 128
MOE_ALIGN = 16
MOE_WIN = MOE_TILE + MOE_ALIGN
SEL_ROWS = 128
VMEM_MB = 56


def _cp(n_grid, vmem_mb=VMEM_MB):
    return pltpu.CompilerParams(dimension_semantics=("arbitrary",) * n_grid,
                                vmem_limit_bytes=vmem_mb * 1024 * 1024)


def _pick(n, cands):
    for c in cands:
        if n % c == 0:
            return c
    raise ValueError(f"no tile for {n} in {cands}")


def _dg(a, b, ca=1, cb=0):
    return lax.dot_general(a, b, (((ca,), (cb,)), ((), ())), preferred_element_type=F32)


def _split2(x):
    hi = x.astype(BF16)
    lo = (x - hi.astype(F32)).astype(BF16)
    return hi, lo


def _mm(a, b, passes=1, nt=False):
    cb = 1 if nt else 0
    if passes == 1:
        return _dg(a.astype(BF16), b.astype(BF16), 1, cb)
    ah, al = _split2(a)
    bh, bl = _split2(b)
    return _dg(ah, bh, 1, cb) + (_dg(ah, bl, 1, cb) + _dg(al, bh, 1, cb))


def _mm_exact_rhs(x, m_bf16):
    x1 = x.astype(BF16)
    r1 = x - x1.astype(F32)
    x2 = r1.astype(BF16)
    x3 = (r1 - x2.astype(F32)).astype(BF16)
    return _dg(x1, m_bf16) + (_dg(x2, m_bf16) + _dg(x3, m_bf16))


def _sigmoid(x):
    return 1.0 / (1.0 + jnp.exp(-x))


def _iota(shape, dim):
    return lax.broadcasted_iota(I32, shape, dim)


def _block_ones(n, blk):
    i = _iota((n, n), 0) // blk
    j = _iota((n, n), 1) // blk
    return jnp.where(i == j, 1.0, 0.0).astype(BF16)


def _seg_sum(x, ones_bf16):
    return _mm_exact_rhs(x, ones_bf16)


def _swap16(x):
    n = x.shape[-1]
    lane = _iota(x.shape, x.ndim - 1)
    fwd = pltpu.roll(x, n - 16, x.ndim - 1)
    bwd = pltpu.roll(x, 16, x.ndim - 1)
    return jnp.where((lane % 32) < 16, fwd, bwd)


def _adaln_kernel(c_ref, dn_ref, up_ref, b_ref, o_ref):
    c = c_ref[...]
    t = _mm(c * _sigmoid(c), dn_ref[...], 3)
    o_ref[...] = _mm(t, up_ref[...], 3) + b_ref[...]


def _adaln(cond8, down, up, bias):
    d, r = down.shape
    n = up.shape[1]
    tn = _pick(n, (4096, 2048, 1024, 512, 256, 128))
    return pl.pallas_call(
        _adaln_kernel,
        grid=(n // tn,),
        in_specs=[pl.BlockSpec((8, d), lambda j: (0, 0)),
                  pl.BlockSpec((d, r), lambda j: (0, 0)),
                  pl.BlockSpec((r, tn), lambda j: (0, j)),
                  pl.BlockSpec((1, tn), lambda j: (0, j))],
        out_specs=pl.BlockSpec((8, tn), lambda j: (0, j)),
        out_shape=jax.ShapeDtypeStruct((8, n), F32),
        compiler_params=_cp(1),
        name="adaln",
    )(cond8, down, up, bias.reshape(1, n))


def _rows_are_ctx(tile_rows, row0, nu):
    return (row0 + _iota((tile_rows, 1), 0)) < nu


def _norm_mod_kernel(x_ref, g_ref, sh_ref, sc_ref, *rest, nu, tm, router):
    if router:
        wr_ref, h_ref, aff_ref = rest
    else:
        (h_ref,) = rest
    x = x_ref[...]
    is_u = pl.program_id(0) * tm < nu
    y = x * lax.rsqrt(jnp.mean(x * x, axis=-1, keepdims=True) + NORM_EPS) * g_ref[...]
    sh = jnp.where(is_u, sh_ref[0:1, :], sh_ref[1:2, :])
    sc = jnp.where(is_u, sc_ref[0:1, :], sc_ref[1:2, :])
    h = y * (1.0 + sc) + sh
    h_ref[...] = h.astype(h_ref.dtype)
    if router:
        logits = _mm(wr_ref[...], h, 3, nt=True)
        m = jnp.max(logits, axis=0, keepdims=True)
        p = jnp.exp(logits - m)
        aff_ref[...] = p / jnp.sum(p, axis=0, keepdims=True)


def _norm_mod(x_all, gain, shift2, scale2, nu, router_t=None):
    t, d = x_all.shape
    tm = ROW_TILE
    assert nu % tm == 0
    router = router_t is not None
    in_specs = [pl.BlockSpec((tm, d), lambda i: (i, 0)),
                pl.BlockSpec((1, d), lambda i: (0, 0)),
                pl.BlockSpec((2, d), lambda i: (0, 0)),
                pl.BlockSpec((2, d), lambda i: (0, 0))]
    args = [x_all, gain.reshape(1, d), shift2, scale2]
    out_specs = pl.BlockSpec((tm, d), lambda i: (i, 0))
    out_shape = jax.ShapeDtypeStruct((t, d), BF16)
    if router:
        e = router_t.shape[0]
        in_specs.append(pl.BlockSpec((e, d), lambda i: (0, 0)))
        args.append(router_t)
        out_specs = (out_specs, pl.BlockSpec((e, tm), lambda i: (0, i)))
        out_shape = (out_shape, jax.ShapeDtypeStruct((e, t), F32))
    return pl.pallas_call(
        functools.partial(_norm_mod_kernel, nu=nu, tm=tm, router=router),
        grid=(t // tm,), in_specs=in_specs, out_specs=out_specs, out_shape=out_shape,
        compiler_params=_cp(1), name="norm_mod_router" if router else "norm_mod",
    )(*args)


def _matmul_kernel(*refs, nu, tm, act, has_rms, has_resid):
    it = iter(refs)
    a_ref = next(it)
    w_ref = next(it)
    g_ref = next(it) if has_rms else None
    x_ref = next(it) if has_resid else None
    gate_ref = next(it) if has_resid else None
    o_ref = next(it)
    a = a_ref[...]
    if has_rms:
        af = a.astype(F32)
        a = af * lax.rsqrt(jnp.mean(af * af, axis=-1, keepdims=True) + NORM_EPS) * g_ref[...]
    acc = _dg(a.astype(BF16), w_ref[...])
    if act == "sigmoid":
        acc = _sigmoid(acc)
    if has_resid:
        is_u = _rows_are_ctx(tm, pl.program_id(1) * tm, nu)
        gate = jnp.where(is_u, gate_ref[0:1, :], gate_ref[1:2, :])
        acc = x_ref[...] + gate * acc
    o_ref[...] = acc.astype(o_ref.dtype)


def _matmul(a, w, out_dtype, *, a_col_block=0, act=None, rms_gain=None, resid=None, gate2=None, nu=0,
            name="matmul"):
    m = a.shape[0]
    k = w.shape[0]
    n = w.shape[1]
    tm = _pick(m, (640, 256, 128))
    tn = n if n <= 2048 else _pick(n, (1024, 768, 512, 384, 256, 128))
    has_rms = rms_gain is not None
    has_resid = resid is not None
    in_specs = [pl.BlockSpec((tm, k), lambda j, i: (i, a_col_block)),
                pl.BlockSpec((k, tn), lambda j, i: (0, j))]
    args = [a, w]
    if has_rms:
        in_specs.append(pl.BlockSpec((1, k), lambda j, i: (0, 0)))
        args.append(rms_gain.reshape(1, k))
    if has_resid:
        in_specs += [pl.BlockSpec((tm, tn), lambda j, i: (i, j)),
                     pl.BlockSpec((2, tn), lambda j, i: (0, j))]
        args += [resid, gate2]
    return pl.pallas_call(
        functools.partial(_matmul_kernel, nu=nu, tm=tm, act=act, has_rms=has_rms, has_resid=has_resid),
        grid=(n // tn, m // tm), in_specs=in_specs,
        out_specs=pl.BlockSpec((tm, tn), lambda j, i: (i, j)),
        out_shape=jax.ShapeDtypeStruct((m, n), out_dtype),
        compiler_params=_cp(2), name=name,
    )(*args)


def _rwkv_feat_kernel(z_ref, zp_ref, zn_ref, mu_ref, w0_ref, wup_ref, a0_ref, aup_ref, gup_ref,
                      r_ref, k_ref, v_ref, e_ref, a_ref, g_ref, *, nu, t_all, tm, aw, dr):
    z = z_ref[...]
    row = pl.program_id(0) * tm + _iota((tm, 1), 0)
    ri = _iota((tm, 1), 0)
    up1 = pltpu.roll(z, 1, 0)
    dn1 = pltpu.roll(z, tm - 1, 0)
    zp = jnp.where(ri == 0, zp_ref[7:8, :], up1)
    zn = jnp.where(ri == tm - 1, zn_ref[0:1, :], dn1)
    zp = jnp.where((row == 0) | (row == nu), 0.0, zp)
    zn = jnp.where((row == nu - 1) | (row == t_all - 1), 0.0, zn)
    zs = z + mu_ref[0:1, :] * (zp - z) + mu_ref[1:2, :] * (zn - z)
    r_ref[...] = zs[:, 0:aw]
    k_ref[...] = zs[:, aw:2 * aw]
    v_ref[...] = zs[:, 2 * aw:3 * aw]
    o = 3 * aw
    for n in range(2):
        wd = jnp.tanh(zs[:, o + n * dr:o + (n + 1) * dr])
        w = w0_ref[n:n + 1, :] + _mm(wd, wup_ref[n])
        sp = jnp.maximum(-w, 0.0) + jnp.log(1.0 + jnp.exp(-jnp.abs(w)))
        e_ref[n] = jnp.exp(-sp - 0.5)
    o += 2 * dr
    for n in range(2):
        ad = zs[:, o + n * dr:o + (n + 1) * dr]
        a_ref[n] = _sigmoid(a0_ref[n:n + 1, :] + _mm(ad, aup_ref[n]))
    o += 2 * dr
    g_ref[...] = _mm(_sigmoid(zs[:, o:]), gup_ref[...])


def _rwkv_feat(za, mu_p, w0, w_up, a0, a_up, g_up_p, nu):
    t, na = za.shape
    aw = w0.shape[1]
    dr = w_up.shape[1]
    tm = ROW_TILE
    nb8 = tm // 8
    last8 = t // 8 - 1
    row_spec = lambda w: pl.BlockSpec((tm, w), lambda i: (i, 0))
    full = lambda a: pl.BlockSpec(a.shape, lambda i: (0,) * a.ndim)
    out_rows = jax.ShapeDtypeStruct((t, aw), F32)
    out_dir = jax.ShapeDtypeStruct((2, t, aw), F32)
    dir_spec = pl.BlockSpec((2, tm, aw), lambda i: (0, i, 0))
    return pl.pallas_call(
        functools.partial(_rwkv_feat_kernel, nu=nu, t_all=t, tm=tm, aw=aw, dr=dr),
        grid=(t // tm,),
        in_specs=[row_spec(na),
                  pl.BlockSpec((8, na), lambda i: (jnp.maximum(i * nb8 - 1, 0), 0)),
                  pl.BlockSpec((8, na), lambda i: (jnp.minimum((i + 1) * nb8, last8), 0)),
                  full(mu_p), full(w0), full(w_up), full(a0), full(a_up), full(g_up_p)],
        out_specs=(row_spec(aw), row_spec(aw), row_spec(aw), dir_spec, dir_spec, row_spec(aw)),
        out_shape=(out_rows, out_rows, out_rows, out_dir, out_dir, out_rows),
        compiler_params=_cp(1), name="rwkv_feat",
    )(za, za, za, mu_p, w0, w_up, a0, a_up, g_up_p)


def _bmm(a, b, nt=False):
    cb = 2 if nt else 1
    return lax.dot_general(a.astype(BF16), b.astype(BF16), (((2,), (cb,)), ((0,), (0,))),
                           preferred_element_type=F32)


def _bmm_exact_lhs(m_bf16, x):
    dn = (((2,), (1,)), ((0,), (0,)))
    x1 = x.astype(BF16)
    r1 = x - x1.astype(F32)
    x2 = r1.astype(BF16)
    x3 = (r1 - x2.astype(F32)).astype(BF16)
    d = lambda y: lax.dot_general(m_bf16, y, dn, preferred_element_type=F32)
    return d(x1) + (d(x2) + d(x3))


def _bt(x):
    return jnp.stack([x[g].T for g in range(x.shape[0])], axis=0)


def _rwkv_chunk_math(r, k, v, e, a, kk_gain, ka_gain, reverse, groups):
    c = CHUNK
    hd = A_HEAD_DIM
    lane = _iota((1, LANES), 1)
    m_a = jnp.where(lane < hd, 1.0, 0.0)
    m_b = 1.0 - m_a
    ones_seg = _block_ones(LANES, hd)
    ti = _iota((groups, c, c), 1)
    tj = _iota((groups, c, c), 2)
    tri = jnp.where((tj >= ti) if reverse else (tj <= ti), 1.0, 0.0).astype(BF16)

    kk0 = k * kk_gain
    kk = kk0 / jnp.maximum(jnp.sqrt(_seg_sum(kk0 * kk0, ones_seg)), 1e-12)
    kt = k * (1.0 + (a - 1.0) * ka_gain)
    b = kk * a
    g3 = lambda x: x.reshape(groups, c, LANES)
    e3 = g3(e)
    cl = _bmm_exact_lhs(tri, e3)
    last = 0 if reverse else c - 1
    ctot = cl[:, last:last + 1, :]
    g_in = jnp.exp(-cl)
    g_ex = jnp.exp(e3 - cl)
    g_inv = jnp.exp(cl)
    g_end = jnp.exp(cl - ctot)
    st = lambda x: jnp.concatenate([x * m_a, x * m_b], axis=1)
    kk3, b3, kt3 = g3(kk), g3(b), g3(kt)
    kk2 = st(kk3 * g_ex)
    r2 = st(g3(r) * g_in)
    b2 = st(b3 * g_inv)
    k2 = st(kt3 * g_inv)
    v2 = st(g3(v))
    bg2 = st(b3 * g_end)
    kg2 = st(kt3 * g_end)

    s = _bmm(jnp.concatenate([kk2, r2], axis=1), jnp.concatenate([b2, k2], axis=1), nt=True)
    n2 = 2 * c
    i2 = _iota((n2, n2), 0)
    j2 = _iota((n2, n2), 1)
    il = i2 % c
    jl = j2 % c
    strict = (jl > il) if reverse else (jl < il)
    incl = (jl >= il) if reverse else (jl <= il)
    a_b = jnp.where(strict, s[:, :n2, :n2], 0.0)
    a_k = jnp.where(strict, s[:, :n2, n2:], 0.0)
    l_b = jnp.where(incl, s[:, n2:, :n2], 0.0)
    l_k = jnp.where(incl, s[:, n2:, n2:], 0.0)
    eye = jnp.where(i2 == j2, 1.0, 0.0)

    same = lambda m: (i2 // m) == (j2 // m)
    a_d = jnp.where(same(8), a_b, 0.0)
    a_d2 = _bmm(a_d, a_d)
    a_d4 = _bmm(a_d2, a_d2)
    t_inv = _bmm(_bmm(eye - a_d, eye + a_d2), eye + a_d4)
    m = 8
    while m < c:
        a_off = jnp.where(same(2 * m) & jnp.logical_not(same(m)), a_b, 0.0)
        t_inv = t_inv - _bmm(_bmm(t_inv, a_off), t_inv)
        m *= 2

    av = _bmm(a_k, v2)
    z12 = _bmm(t_inv, jnp.concatenate([kk2, av], axis=2))
    z1 = z12[:, :, :LANES]
    z2 = z12[:, :, LANES:]
    vz = jnp.concatenate([v2, z2], axis=1)
    p_bd = eye * jnp.exp(-ctot) - _bmm(_bt(bg2), z1)
    q_bd = _bmm(_bt(jnp.concatenate([kg2, -bg2], axis=1)), vz)
    r2s = r2 - _bmm(l_b, z1)
    y0s = _bmm(jnp.concatenate([l_k, -l_b], axis=2), vz)
    cp = lambda x: x[:, :x.shape[1] // 2] + x[:, x.shape[1] // 2:]
    return cp(p_bd), cp(q_bd), cp(r2s).reshape(groups * c, LANES), cp(y0s).reshape(groups * c, LANES)


def _rwkv_chunk_kernel(r_ref, k_ref, v_ref, e_ref, a_ref, kkg_ref, kag_ref,
                       p_ref, q_ref, r2_ref, y0_ref, *, reverse, groups):
    p, q, r2, y0 = _rwkv_chunk_math(r_ref[...], k_ref[...], v_ref[...], e_ref[0], a_ref[0],
                                    kkg_ref[...], kag_ref[...], reverse, groups)
    p_ref[0] = p
    q_ref[0] = q
    r2_ref[...] = r2
    y0_ref[...] = y0


def _rwkv_chunks(r, k, v, e, a, k_k, k_a, direction):
    t, aw = r.shape
    npair = aw // LANES
    nc = t // CHUNK
    groups = _pick(nc, (10, 4))
    rows = groups * CHUNK
    row_spec = pl.BlockSpec((rows, LANES), lambda p, i: (i, p))
    dir_spec = pl.BlockSpec((1, rows, LANES), lambda p, i: (direction, i, p))
    par_spec = pl.BlockSpec((1, LANES), lambda p, i: (0, p))
    pq_spec = pl.BlockSpec((1, groups, A_HEAD_DIM, LANES), lambda p, i: (p, i, 0, 0))
    pq_shape = jax.ShapeDtypeStruct((npair, nc, A_HEAD_DIM, LANES), F32)
    ry_shape = jax.ShapeDtypeStruct((t, aw), F32)
    return pl.pallas_call(
        functools.partial(_rwkv_chunk_kernel, reverse=bool(direction), groups=groups),
        grid=(npair, t // rows),
        in_specs=[row_spec, row_spec, row_spec, dir_spec, dir_spec, par_spec, par_spec],
        out_specs=(pq_spec, pq_spec, row_spec, row_spec),
        out_shape=(pq_shape, pq_shape, ry_shape, ry_shape),
        compiler_params=_cp(2), name="rwkv_chunks_bwd" if direction else "rwkv_chunks_fwd",
    )(r, k, v, e, a, k_k.reshape(1, aw), k_a.reshape(1, aw))


def _pair_block_diag(x):
    lane = _iota(x.shape, 1)
    return jnp.concatenate([jnp.where(lane < A_HEAD_DIM, x, 0.0), jnp.where(lane >= A_HEAD_DIM, x, 0.0)], axis=0)


def _rwkv_seq_kernel(p_ref, q_ref, r2_ref, y0_ref, y_ref, h_ref, *, reverse, groups, npair):
    c = CHUNK

    @pl.when(pl.program_id(0) == 0)
    def _():
        h_ref[...] = jnp.zeros_like(h_ref)

    order = range(groups - 1, -1, -1) if reverse else range(groups)
    for g in order:
        sl = slice(g * c, (g + 1) * c)
        for p in range(npair):
            ls = slice(p * LANES, (p + 1) * LANES)
            h = h_ref[p]
            y_ref[sl, ls] = _mm(r2_ref[sl, ls], h) + y0_ref[sl, ls]
            h_ref[p] = _mm(_pair_block_diag(p_ref[p, g]), h) + _pair_block_diag(q_ref[p, g])


def _rwkv_seq(p, q, r2, y0, direction, nu):
    npair, nc = p.shape[:2]
    t, aw = r2.shape
    groups = SEQ_GROUP
    rows = groups * CHUNK
    nb = t // rows
    nbu = nu // rows
    assert nu % rows == 0
    if direction:
        blk = lambda s: jnp.where(s < nbu, nbu - 1 - s, nb + nbu - 1 - s)
    else:
        blk = lambda s: s
    pq_spec = pl.BlockSpec((npair, groups, A_HEAD_DIM, LANES), lambda s: (0, blk(s), 0, 0))
    row_spec = pl.BlockSpec((rows, aw), lambda s: (blk(s), 0))
    return pl.pallas_call(
        functools.partial(_rwkv_seq_kernel, reverse=bool(direction), groups=groups, npair=npair),
        grid=(nb,),
        in_specs=[pq_spec, pq_spec, row_spec, row_spec],
        out_specs=row_spec,
        out_shape=jax.ShapeDtypeStruct((t, aw), F32),
        scratch_shapes=[pltpu.VMEM((npair, LANES, LANES), F32)],
        compiler_params=_cp(1), name="rwkv_seq_bwd" if direction else "rwkv_seq_fwd",
    )(p, q, r2, y0)


def _rwkv_out_kernel(yf_ref, yb_ref, r_ref, k_ref, v_ref, a_ref, g_ref, lw_ref, lb_ref, rk_ref, ka_ref, o_ref, *, aw):
    hd = A_HEAD_DIM
    ones_seg = _block_ones(LANES, hd)
    for p in range(aw // LANES):
        ls = slice(p * LANES, (p + 1) * LANES)
        y = yf_ref[:, ls] + yb_ref[:, ls]
        mean = _seg_sum(y, ones_seg) * (1.0 / hd)
        yc = y - mean
        var = _seg_sum(yc * yc, ones_seg) * (1.0 / hd)
        yn = yc * lax.rsqrt(var + A_LNX_EPS) * lw_ref[:, ls] + lb_ref[:, ls]
        r = r_ref[:, ls]
        k = k_ref[:, ls]
        v = v_ref[:, ls]
        bonus = jnp.zeros_like(y)
        for n in range(2):
            kt = k * (1.0 + (a_ref[n, :, ls] - 1.0) * ka_ref[:, ls])
            bonus = bonus + _seg_sum(r * kt * rk_ref[:, ls], ones_seg) * v
        o_ref[:, ls] = ((yn + bonus) * g_ref[:, ls]).astype(o_ref.dtype)


def _rwkv_out(yf, yb, r, k, v, a, g, lnx_w, lnx_b, r_k, k_a):
    t, aw = r.shape
    tm = ROW_TILE
    row_spec = pl.BlockSpec((tm, aw), lambda i: (i, 0))
    par_spec = pl.BlockSpec((1, aw), lambda i: (0, 0))
    return pl.pallas_call(
        functools.partial(_rwkv_out_kernel, aw=aw),
        grid=(t // tm,),
        in_specs=[row_spec] * 5 + [pl.BlockSpec((2, tm, aw), lambda i: (0, i, 0)), row_spec] + [par_spec] * 4,
        out_specs=row_spec,
        out_shape=jax.ShapeDtypeStruct((t, aw), BF16),
        compiler_params=_cp(1), name="rwkv_out",
    )(yf, yb, r, k, v, a, g, lnx_w.reshape(1, aw), lnx_b.reshape(1, aw), r_k.reshape(1, aw), k_a.reshape(1, aw))


def _rms_rope_slab(x, gain, cos, sin, ones_seg):
    ms = _seg_sum(x * x, ones_seg) * (1.0 / B_HEAD_DIM)
    y = x * lax.rsqrt(ms + NORM_EPS) * gain
    return y * cos + _swap16(y) * sin


def _gqa_prep_kernel(z_ref, cos_ref, sin_ref, qg_ref, kg_ref, o_ref, *, qw, kw):
    ones_seg = _block_ones(LANES, B_HEAD_DIM)
    cos = cos_ref[...]
    sin = sin_ref[...]
    scale = B_HEAD_DIM ** -0.5
    for s in range(qw // LANES):
        ls = slice(s * LANES, (s + 1) * LANES)
        o_ref[:, ls] = (_rms_rope_slab(z_ref[:, ls], qg_ref[...], cos, sin, ones_seg) * scale).astype(o_ref.dtype)
    for s in range(kw // LANES):
        ls = slice(qw + s * LANES, qw + (s + 1) * LANES)
        o_ref[:, ls] = _rms_rope_slab(z_ref[:, ls], kg_ref[...], cos, sin, ones_seg).astype(o_ref.dtype)
    o_ref[:, qw + kw:] = z_ref[:, qw + kw:].astype(o_ref.dtype)


def _gqa_prep(zb, cos, sin, q_norm, k_norm):
    t, nb = zb.shape
    tm = ROW_TILE
    qw = B_Q_HEADS * B_HEAD_DIM
    kw = B_KV_HEADS * B_HEAD_DIM
    tile2 = lambda g: jnp.tile(g.reshape(1, B_HEAD_DIM), (1, LANES // B_HEAD_DIM))
    return pl.pallas_call(
        functools.partial(_gqa_prep_kernel, qw=qw, kw=kw),
        grid=(t // tm,),
        in_specs=[pl.BlockSpec((tm, nb), lambda i: (i, 0)),
                  pl.BlockSpec((tm, LANES), lambda i: (i, 0)),
                  pl.BlockSpec((tm, LANES), lambda i: (i, 0)),
                  pl.BlockSpec((1, LANES), lambda i: (0, 0)),
                  pl.BlockSpec((1, LANES), lambda i: (0, 0))],
        out_specs=pl.BlockSpec((tm, nb), lambda i: (i, 0)),
        out_shape=jax.ShapeDtypeStruct((t, nb), BF16),
        compiler_params=_cp(1), name="gqa_prep",
    )(zb, cos, sin, tile2(q_norm), tile2(k_norm))


def _gqa_attn_kernel(sink_ref, q_ref, c_ref, kp_ref, ko_ref, kn_ref, o_ref, *, nu, seq, qw, kw):
    hd = B_HEAD_DIM
    grp = B_Q_HEADS // B_KV_HEADS
    blk = BLOCK
    j = pl.program_id(0)
    jb = j - nu // blk
    nkeys = nu + 3 * blk
    rows = grp * blk
    qi = _iota((rows, nkeys), 0) % blk
    kc = _iota((rows, nkeys), 1)
    q_pos = jb * blk + qi
    k_pos = (jb - 1) * blk + (kc - nu)
    band_ok = (jnp.abs(k_pos - q_pos) <= WINDOW) & (k_pos >= 0) & (k_pos < seq) & (jb >= 0)
    valid = (kc < nu) | band_ok
    rg = _iota((rows, 1), 0) // blk
    outs = []
    for h in range(B_KV_HEADS):
        ks = slice(qw + h * hd, qw + (h + 1) * hd)
        vs = slice(qw + kw + h * hd, qw + kw + (h + 1) * hd)
        k_all = jnp.concatenate([c_ref[:, ks], kp_ref[:, ks], ko_ref[:, ks], kn_ref[:, ks]], axis=0)
        v_all = jnp.concatenate([c_ref[:, vs], kp_ref[:, vs], ko_ref[:, vs], kn_ref[:, vs]], axis=0)
        q4 = jnp.concatenate([q_ref[:, (h * grp + g) * hd:(h * grp + g + 1) * hd] for g in range(grp)], axis=0)
        s = jnp.where(valid, _dg(q4, k_all, 1, 1), NEG_INF)
        sink = jnp.zeros((rows, 1), F32)
        for g in range(grp):
            sink = jnp.where(rg == g, sink_ref[h * grp + g], sink)
        m = jnp.maximum(jnp.max(s, axis=1, keepdims=True), sink)
        p = jnp.exp(s - m)
        den = jnp.sum(p, axis=1, keepdims=True) + jnp.exp(sink - m)
        o = _dg(p.astype(BF16), v_all) / den
        outs += [o[g * blk:(g + 1) * blk, :] for g in range(grp)]
    o_ref[...] = jnp.concatenate(outs, axis=1).astype(o_ref.dtype)


def _gqa_attn(qkv, sink, nu, seq):
    t, nb = qkv.shape
    qw = B_Q_HEADS * B_HEAD_DIM
    kw = B_KV_HEADS * B_HEAD_DIM
    blk = BLOCK
    nblk = t // blk
    band = lambda off: pl.BlockSpec((blk, nb), lambda j: (jnp.clip(j + off, 0, nblk - 1), 0))
    return pl.pallas_call(
        functools.partial(_gqa_attn_kernel, nu=nu, seq=seq, qw=qw, kw=kw),
        grid=(nblk,),
        in_specs=[pl.BlockSpec(memory_space=pltpu.SMEM),
                  pl.BlockSpec((blk, nb), lambda j: (j, 0)),
                  pl.BlockSpec((nu, nb), lambda j: (0, 0)),
                  band(-1), band(0), band(1)],
        out_specs=pl.BlockSpec((blk, qw), lambda j: (j, 0)),
        out_shape=jax.ShapeDtypeStruct((t, qw), BF16),
        compiler_params=_cp(1), name="gqa_attn",
    )(sink, qkv, qkv, qkv, qkv, qkv)


def _mla_prep_kernel(q_ref, kv_ref, kr_ref, cos_ref, sin_ref, nn_ref, rn_ref, qo_ref, ko_ref, vo_ref):
    cos = cos_ref[...]
    sin = sin_ref[...]
    dn = C_NOPE_DIM
    hw = 2 * LANES
    scale = (C_NOPE_DIM + C_ROPE_DIM) ** -0.5 * math.log2(math.e)
    ones = jnp.ones((q_ref.shape[0], C_V_DIM), vo_ref.dtype)

    def rms_rope(x, gain):
        ms = jnp.sum(x * x, axis=-1, keepdims=True) * (1.0 / C_ROPE_DIM)
        y = x * lax.rsqrt(ms + NORM_EPS) * gain
        return y * cos + _swap16(y) * sin

    def rms(x, gain):
        return x * lax.rsqrt(jnp.mean(x * x, axis=-1, keepdims=True) + NORM_EPS) * gain

    kr = rms_rope(kr_ref[...], rn_ref[1:2, :]).astype(ko_ref.dtype)
    for h in range(C_HEADS):
        qn = rms(q_ref[:, h * hw:h * hw + dn], nn_ref[0:1, :])
        qr = rms_rope(q_ref[:, h * hw + dn:(h + 1) * hw], rn_ref[0:1, :])
        qo_ref[:, h * hw:h * hw + dn] = (qn * scale).astype(qo_ref.dtype)
        qo_ref[:, h * hw + dn:(h + 1) * hw] = (qr * scale).astype(qo_ref.dtype)
        ko_ref[:, h * hw:h * hw + dn] = rms(kv_ref[:, h * dn:(h + 1) * dn], nn_ref[1:2, :]).astype(ko_ref.dtype)
        ko_ref[:, h * hw + dn:(h + 1) * hw] = kr
        vo_ref[:, h * hw:h * hw + C_V_DIM] = kv_ref[:, (C_HEADS + h) * dn:(C_HEADS + h + 1) * dn].astype(vo_ref.dtype)
        vo_ref[:, h * hw + C_V_DIM:(h + 1) * hw] = ones


def _mla_prep(q, kv, zc, kr_col_block, cos, sin, nope_norm, rope_norm_p):
    t = q.shape[0]
    tm = ROW_TILE
    hw = 2 * LANES
    row = lambda w: pl.BlockSpec((tm, w), lambda i: (i, 0))
    full = lambda a: pl.BlockSpec(a.shape, lambda i: (0, 0))
    return pl.pallas_call(
        _mla_prep_kernel,
        grid=(t // tm,),
        in_specs=[row(q.shape[1]), row(kv.shape[1]),
                  pl.BlockSpec((tm, LANES), lambda i: (i, kr_col_block)),
                  row(LANES), row(LANES), full(nope_norm), full(rope_norm_p)],
        out_specs=(row(C_HEADS * hw),) * 3,
        out_shape=(jax.ShapeDtypeStruct((t, C_HEADS * hw), BF16),) * 3,
        compiler_params=_cp(1), name="mla_prep",
    )(q, kv, zc, cos, sin, nope_norm, rope_norm_p)


def _mla_flash_kernel(q_ref, k_ref, v_ref, o_ref, m_ref, acc_ref, *, nu, tq, tk, sub):
    qi = pl.program_id(1)
    ki = pl.program_id(2)

    @pl.when(ki == 0)
    def _():
        m_ref[...] = jnp.full_like(m_ref, NEG_INF)
        acc_ref[...] = jnp.zeros_like(acc_ref)

    def update(masked):
        for qs in range(tq // sub):
            rs = slice(qs * sub, (qs + 1) * sub)
            s = _dg(q_ref[rs, :], k_ref[...], 1, 1)
            if masked and qs * sub < nu:
                qrow = qi * tq + qs * sub + _iota((sub, tk), 0)
                kcol = ki * tk + _iota((sub, tk), 1)
                s = jnp.where((qrow < nu) & (kcol >= nu), NEG_INF, s)
            m_prev = m_ref[rs, :]
            m_next = jnp.maximum(m_prev, jnp.max(s, axis=1, keepdims=True))
            alpha = jnp.exp2(m_prev - m_next)
            p = jnp.exp2((s - m_next[:, 0:1]).astype(BF16))
            acc_ref[rs, :] = jnp.concatenate([alpha, alpha], axis=1) * acc_ref[rs, :] + _dg(p, v_ref[...])
            m_ref[rs, :] = m_next

    has_ctx_rows = qi * tq < nu

    @pl.when(has_ctx_rows)
    def _():
        update(True)

    @pl.when(jnp.logical_not(has_ctx_rows))
    def _():
        update(False)

    @pl.when(ki == pl.num_programs(2) - 1)
    def _():
        o_ref[...] = (acc_ref[:, :C_V_DIM] / acc_ref[:, C_V_DIM:]).astype(o_ref.dtype)


def _mla_flash(qf, kf, vf, nu):
    t = qf.shape[0]
    hw = 2 * LANES
    tq = _pick(t, (3328, 256))
    tk = _pick(t, (1280, 256))
    return pl.pallas_call(
        functools.partial(_mla_flash_kernel, nu=nu, tq=tq, tk=tk, sub=256),
        grid=(C_HEADS, t // tq, t // tk),
        in_specs=[pl.BlockSpec((tq, hw), lambda h, i, j: (i, h)),
                  pl.BlockSpec((tk, hw), lambda h, i, j: (j, h)),
                  pl.BlockSpec((tk, hw), lambda h, i, j: (j, h))],
        out_specs=pl.BlockSpec((tq, C_V_DIM), lambda h, i, j: (i, h)),
        out_shape=jax.ShapeDtypeStruct((t, C_HEADS * C_V_DIM), BF16),
        scratch_shapes=[pltpu.VMEM((tq, LANES), F32), pltpu.VMEM((tq, hw), F32)],
        compiler_params=_cp(3), name="mla_flash",
    )(qf, kf, vf)


def _merge_kernel(ya_ref, yb_ref, yc_ref, w_ref, g0_ref, g1_ref, g2_ref, o_ref):
    acc = g0_ref[...].astype(F32) * _dg(ya_ref[...], w_ref[0])
    acc = acc + g1_ref[...].astype(F32) * _dg(yb_ref[...], w_ref[1])
    acc = acc + g2_ref[...].astype(F32) * _dg(yc_ref[...], w_ref[2])
    o_ref[...] = acc.astype(o_ref.dtype)


def _merge(ya, yb, yc, w_branch, gates):
    t, bw = ya.shape
    d = w_branch.shape[2]
    tm = _pick(t, (640, 256, 128))
    tn = _pick(d, (1024, 512, 256, 128))
    nj = d // tn
    y_spec = pl.BlockSpec((tm, bw), lambda j, i: (i, 0))
    gate = lambda b: pl.BlockSpec((tm, tn), lambda j, i: (i, b * nj + j))
    return pl.pallas_call(
        _merge_kernel,
        grid=(nj, t // tm),
        in_specs=[y_spec, y_spec, y_spec, pl.BlockSpec((N_BRANCH, bw, tn), lambda j, i: (0, 0, j)),
                  gate(0), gate(1), gate(2)],
        out_specs=pl.BlockSpec((tm, tn), lambda j, i: (i, j)),
        out_shape=jax.ShapeDtypeStruct((t, d), BF16),
        compiler_params=_cp(2), name="merge",
    )(ya, yb, yc, w_branch, gates, gates, gates)


def _moe_select_kernel(aff_ref, loc_ref, base_ref, *, cap, n_exp):
    aff = aff_ref[...]
    bits = pltpu.bitcast(aff, I32)
    count = lambda m: jnp.sum(jnp.sum(jnp.where(m, 1.0, 0.0), axis=1, keepdims=True), axis=2, keepdims=True)
    theta = jnp.zeros((n_exp, 1, 1), I32)
    for bit in range(30, -1, -1):
        cand = theta | (1 << bit)
        theta = jnp.where(count(bits >= cand) >= cap, cand, theta)
    gt = bits > theta
    eq = bits == theta
    need = cap - count(gt)
    r = aff.shape[1]
    upper = jnp.where(_iota((LANES, LANES), 0) <= _iota((LANES, LANES), 1), 1.0, 0.0).astype(BF16)
    lower_strict = jnp.where(_iota((r, r), 1) < _iota((r, r), 0), 1.0, 0.0).astype(BF16)

    def prefix(x):
        incl = _dg(x.astype(BF16), upper)
        tot = jnp.broadcast_to(incl[:, LANES - 1:LANES], (r, LANES))
        return incl, _dg(lower_strict, tot.astype(BF16))

    for e in range(n_exp):
        xe = jnp.where(eq[e], 1.0, 0.0)
        incl, base = prefix(xe)
        take = eq[e] & ((incl - xe + base) < need[e])
        sel = jnp.where(gt[e] | take, 1.0, 0.0)
        incl, base = prefix(sel)
        loc_ref[e] = jnp.where(sel > 0.0, incl - sel, -4096.0)
        base_ref[e] = base


def _moe_select(aff3, cap):
    e, r, _ = aff3.shape
    spec = pl.BlockSpec((e, r, LANES), lambda i: (0, 0, 0))
    shape = jax.ShapeDtypeStruct((e, r, LANES), F32)
    return pl.pallas_call(
        functools.partial(_moe_select_kernel, cap=cap, n_exp=e),
        grid=(1,), in_specs=[spec], out_specs=(spec, spec), out_shape=(shape, shape),
        compiler_params=_cp(1), name="moe_select",
    )(aff3)


def _window_start(base_ref, e, i):
    b = base_ref[e, i]
    a = (b // MOE_ALIGN) * MOE_ALIGN
    return pl.multiple_of(a, MOE_ALIGN), b - a


def _moe_gather_kernel(base_ref, loc_ref, aff_ref, h_ref, xe_ref, ge_ref, *, tiles, group):
    g = pl.program_id(0)
    i = pl.program_id(2)

    @pl.when(i == 0)
    def _():
        xe_ref[...] = jnp.zeros_like(xe_ref)
        ge_ref[...] = jnp.zeros_like(ge_ref)

    win_row = _iota((MOE_WIN, MOE_TILE), 0).astype(F32)
    for k in range(group):
        e = g * group + k
        for r in range(tiles):
            a, off = _window_start(base_ref, e, i * tiles + r)
            slot = loc_ref[k, r] + off.astype(F32)
            onehot = jnp.where(win_row == slot, 1.0, 0.0)
            rows = _dg(onehot.astype(BF16), h_ref[r * MOE_TILE:(r + 1) * MOE_TILE, :])
            head = pl.ds(a, MOE_ALIGN)
            xe_ref[k, head, :] = (xe_ref[k, head, :].astype(F32) + rows[:MOE_ALIGN]).astype(xe_ref.dtype)
            xe_ref[k, pl.ds(a + MOE_ALIGN, MOE_TILE), :] = rows[MOE_ALIGN:].astype(xe_ref.dtype)
            win = pl.ds(a, MOE_WIN)
            gsel = jnp.sum(onehot * aff_ref[k, r], axis=1, keepdims=True)
            ge_ref[k, win, :] = ge_ref[k, win, :] + gsel


def _moe_gather(base_i, loc, aff3, h_all, row0, n, cap_p):
    e = loc.shape[0]
    d = h_all.shape[1]
    nt = n // MOE_TILE
    tiles = _pick(nt, (4, 2, 1))
    rows = tiles * MOE_TILE
    group = _pick(e, (4, 2, 1))
    dcol = min(d, 1024)
    sel_spec = pl.BlockSpec((group, tiles, 1, LANES), lambda g, c, i, b: (g, i, 0, 0))
    grid_spec = pltpu.PrefetchScalarGridSpec(
        num_scalar_prefetch=1, grid=(e // group, d // dcol, nt // tiles),
        in_specs=[sel_spec, sel_spec,
                  pl.BlockSpec((pl.Element(rows), pl.Element(dcol)),
                               lambda g, c, i, b: (pl.multiple_of(row0 + i * rows, MOE_TILE),
                                                   pl.multiple_of(c * dcol, LANES)))],
        out_specs=(pl.BlockSpec((group, cap_p, dcol), lambda g, c, i, b: (g, 0, c)),
                   pl.BlockSpec((group, cap_p, LANES), lambda g, c, i, b: (g, 0, 0))))
    return pl.pallas_call(
        functools.partial(_moe_gather_kernel, tiles=tiles, group=group), grid_spec=grid_spec,
        out_shape=(jax.ShapeDtypeStruct((e, cap_p, d), BF16), jax.ShapeDtypeStruct((e, cap_p, LANES), F32)),
        compiler_params=_cp(3), name="moe_gather",
    )(base_i, loc, aff3, h_all)


def _moe_ffn_kernel(x_ref, g_ref, wg_ref, wu_ref, wd_ref, y_ref):
    x = x_ref[0]
    hg = _dg(x, wg_ref[0])
    hid = hg * _sigmoid(hg) * _dg(x, wu_ref[0])
    y = _dg(hid.astype(BF16), wd_ref[0]) * g_ref[0][:, 0:1]
    y_ref[0] = y.astype(y_ref.dtype)


def _moe_ffn(xe, ge, w_gate, w_up, w_down, tc):
    e, cap_p, d = xe.shape
    f = w_gate.shape[2]
    return pl.pallas_call(
        _moe_ffn_kernel,
        grid=(e, cap_p // tc),
        in_specs=[pl.BlockSpec((1, tc, d), lambda ee, i: (ee, i, 0)),
                  pl.BlockSpec((1, tc, LANES), lambda ee, i: (ee, i, 0)),
                  pl.BlockSpec((1, d, f), lambda ee, i: (ee, 0, 0)),
                  pl.BlockSpec((1, d, f), lambda ee, i: (ee, 0, 0)),
                  pl.BlockSpec((1, f, d), lambda ee, i: (ee, 0, 0))],
        out_specs=pl.BlockSpec((1, tc, d), lambda ee, i: (ee, i, 0)),
        out_shape=jax.ShapeDtypeStruct((e, cap_p, d), BF16),
        compiler_params=_cp(2), name="moe_ffn",
    )(xe, ge, w_gate, w_up, w_down)


def _moe_combine_kernel(base_ref, loc_ref, *rest, n_exp):
    y_refs = rest[:n_exp]
    x_ref, gate_ref, o_ref = rest[n_exp:]
    i = pl.program_id(0)
    win_row = _iota((MOE_TILE, MOE_WIN), 1).astype(F32)
    acc = None
    for e in range(n_exp):
        _, off = _window_start(base_ref, e, i)
        slot_row = jnp.broadcast_to(loc_ref[e, 0] + off.astype(F32), (MOE_TILE, LANES))
        slot_col = slot_row.T
        slot_col = jnp.concatenate([slot_col, slot_col[:, :MOE_WIN - LANES]], axis=1)
        onehot = jnp.where(win_row == slot_col, 1.0, 0.0).astype(BF16)
        part = _dg(onehot, y_refs[e][...])
        acc = part if acc is None else acc + part
    o_ref[...] = x_ref[...] + gate_ref[...] * acc


def _moe_combine(base_i, loc, ye, x_all, gate_row, row0, n, rows_only):
    e, cap_p, d = ye.shape
    nt = n // MOE_TILE
    t0 = row0 // MOE_TILE
    o0 = 0 if rows_only else t0
    dcol = d

    def y_spec(ee):
        def index(i, j, b):
            start = ee * cap_p + (b[ee, i] // MOE_ALIGN) * MOE_ALIGN
            return pl.multiple_of(start, MOE_ALIGN), pl.multiple_of(j * dcol, LANES)
        return pl.BlockSpec((pl.Element(MOE_WIN), pl.Element(dcol)), index)

    grid_spec = pltpu.PrefetchScalarGridSpec(
        num_scalar_prefetch=1, grid=(nt, d // dcol),
        in_specs=[pl.BlockSpec((e, 1, 1, LANES), lambda i, j, b: (0, i, 0, 0))]
                 + [y_spec(ee) for ee in range(e)]
                 + [pl.BlockSpec((MOE_TILE, dcol), lambda i, j, b: (t0 + i, j)),
                    pl.BlockSpec((1, dcol), lambda i, j, b: (0, j))],
        out_specs=pl.BlockSpec((MOE_TILE, dcol), lambda i, j, b: (o0 + i, j)))
    x_index = 2 + e
    return pl.pallas_call(
        functools.partial(_moe_combine_kernel, n_exp=e), grid_spec=grid_spec,
        out_shape=jax.ShapeDtypeStruct((n, d) if rows_only else x_all.shape, F32),
        input_output_aliases={} if rows_only else {x_index: 0},
        compiler_params=_cp(2), name="moe_combine",
    )(base_i, loc, *([ye.reshape(e * cap_p, d)] * e), x_all, gate_row)


def _moe_stream(x_all, h_all, aff, row0, n, gate_row, w_gate, w_up, w_down, rows_only=False):
    e = aff.shape[0]
    cap = (CAPACITY_FACTOR * n) // e
    tc = 256 if cap >= 256 else 64
    cap_p = -(-(cap + MOE_WIN) // tc) * tc
    n_sel = SEL_ROWS * LANES
    aff_s = lax.dynamic_slice_in_dim(aff, row0, n, axis=1)
    aff3 = jnp.pad(aff_s, ((0, 0), (0, n_sel - n))).reshape(e, SEL_ROWS, LANES)
    loc, base = _moe_select(aff3, cap)
    base_i = base[:, :, 0].astype(I32)
    loc4 = loc.reshape(e, SEL_ROWS, 1, LANES)
    aff4 = aff3.reshape(e, SEL_ROWS, 1, LANES)
    xe, ge = _moe_gather(base_i, loc4, aff4, h_all, row0, n, cap_p)
    ye = _moe_ffn(xe, ge, w_gate, w_up, w_down, tc)
    return _moe_combine(base_i, loc4, ye, x_all, gate_row, row0, n, rows_only)


def _rope_tables(nu, seq):
    n_freq = ROPE_DIM // 4
    pos = jnp.arange(seq)
    inv = jnp.power(ROPE_THETA, -jnp.arange(n_freq, dtype=F32) / n_freq)
    ang_r = (pos // GRID_W).astype(F32)[:, None] * inv[None]
    ang_c = (pos % GRID_W).astype(F32)[:, None] * inv[None]
    cos = jnp.concatenate([jnp.cos(ang_r)] * 2 + [jnp.cos(ang_c)] * 2, axis=1)
    sin = jnp.concatenate([-jnp.sin(ang_r), jnp.sin(ang_r), -jnp.sin(ang_c), jnp.sin(ang_c)], axis=1)
    cos = jnp.concatenate([jnp.ones((nu, ROPE_DIM), F32), cos], axis=0)
    sin = jnp.concatenate([jnp.zeros((nu, ROPE_DIM), F32), sin], axis=0)
    rep = LANES // ROPE_DIM
    return jnp.tile(cos, (1, rep)), jnp.tile(sin, (1, rep))


def _pad_cols(w, n):
    return jnp.pad(w, ((0, 0), (0, n - w.shape[1])))


def _round_up(x, m):
    return -(-x // m) * m


def kernel(x, c, ctx, c_ctx, norm_mix, norm_ffn, ada_down, ada_up, ada_bias, w_in, rwkv_mu, rwkv_w0, rwkv_w_up, rwkv_a0, rwkv_a_up, rwkv_g_up, rwkv_k_k, rwkv_k_a, rwkv_r_k, rwkv_lnx_w, rwkv_lnx_b, gqa_q_norm, gqa_k_norm, gqa_sink, mla_q_a_norm, mla_q_up, mla_kv_a_norm, mla_kv_up, mla_nope_norm, mla_rope_norm, w_branch, w_out, moe_router, moe_w_gate, moe_w_up, moe_w_down):
    bsz, seq, d = x.shape
    assert bsz == 1
    nu = ctx.shape[1]
    depth = w_in.shape[0]
    aw = A_HEADS * A_HEAD_DIM
    dr = rwkv_w_up.shape[2]
    gr = rwkv_g_up.shape[1]
    n_a = 3 * aw + 4 * dr + gr
    n_a_p = _round_up(n_a, LANES)
    n_b = (B_Q_HEADS + 2 * B_KV_HEADS) * B_HEAD_DIM
    cq = mla_q_up.shape[1]
    ckv = mla_kv_up.shape[1]
    n_c = cq + ckv + C_ROPE_DIM
    n_c_p = cq + ckv + LANES
    hw = 2 * LANES

    x_all = jnp.concatenate([ctx[0], x[0]], axis=0)
    cond8 = jnp.zeros((8, d), F32).at[0].set(c_ctx).at[1].set(c[0])
    cos, sin = _rope_tables(nu, seq)

    for i in range(depth):
        wi = w_in[i]
        w_a = _pad_cols(wi[:, :n_a], n_a_p).astype(BF16)
        w_b = wi[:, n_a:n_a + n_b].astype(BF16)
        w_c = _pad_cols(wi[:, n_a + n_b:n_a + n_b + n_c], n_c_p).astype(BF16)
        w_g = wi[:, n_a + n_b + n_c:].astype(BF16)
        mu_p = _pad_cols(rwkv_mu[i], n_a_p)
        g_up_p = jnp.pad(rwkv_g_up[i], ((0, n_a_p - n_a), (0, 0)))
        qu = mla_q_up[i].reshape(cq, C_HEADS, C_NOPE_DIM + C_ROPE_DIM)
        qu = jnp.pad(qu, ((0, 0), (0, 0), (0, hw - C_NOPE_DIM - C_ROPE_DIM))).reshape(cq, C_HEADS * hw).astype(BF16)
        kvu = mla_kv_up[i].reshape(ckv, C_HEADS, C_NOPE_DIM + C_V_DIM)
        kvu = jnp.concatenate([kvu[:, :, :C_NOPE_DIM].reshape(ckv, -1), kvu[:, :, C_NOPE_DIM:].reshape(ckv, -1)],
                              axis=1).astype(BF16)
        rope_norm_p = _pad_cols(mla_rope_norm[i], LANES)

        mod = _adaln(cond8, ada_down[i], ada_up[i], ada_bias[i])[:2].reshape(2, N_MOD, d)

        h = _norm_mod(x_all, norm_mix[i], mod[:, 0], mod[:, 1], nu)
        za = _matmul(h, w_a, F32, name="w_in_a")
        zb = _matmul(h, w_b, F32, name="w_in_b")
        zc = _matmul(h, w_c, F32, name="w_in_c")
        gates = _matmul(h, w_g, BF16, act="sigmoid", name="w_in_gate")

        r, k, v, e, a, g = _rwkv_feat(za, mu_p, rwkv_w0[i], rwkv_w_up[i], rwkv_a0[i], rwkv_a_up[i], g_up_p, nu)
        ys = []
        for direction in (0, 1):
            p_, q_, r2, y0 = _rwkv_chunks(r, k, v, e, a, rwkv_k_k[i], rwkv_k_a[i], direction)
            ys.append(_rwkv_seq(p_, q_, r2, y0, direction, nu))
        ya = _rwkv_out(ys[0], ys[1], r, k, v, a, g, rwkv_lnx_w[i], rwkv_lnx_b[i], rwkv_r_k[i], rwkv_k_a[i])

        qkv_b = _gqa_prep(zb, cos, sin, gqa_q_norm[i], gqa_k_norm[i])
        yb = _gqa_attn(qkv_b, gqa_sink[i], nu, seq)

        assert cq % ckv == 0
        q_c = _matmul(zc, qu, F32, rms_gain=mla_q_a_norm[i], name="mla_q_up")
        kv_c = _matmul(zc, kvu, F32, a_col_block=cq // ckv, rms_gain=mla_kv_a_norm[i], name="mla_kv_up")
        qf, kf, vf = _mla_prep(q_c, kv_c, zc, (cq + ckv) // LANES, cos, sin, mla_nope_norm[i], rope_norm_p)
        yc = _mla_flash(qf, kf, vf, nu)

        merged = _merge(ya, yb, yc, w_branch[i].astype(BF16), gates)
        x_all = _matmul(merged, w_out[i].astype(BF16), F32, resid=x_all, gate2=mod[:, 2], nu=nu, name="w_out")

        hf, aff = _norm_mod(x_all, norm_ffn[i], mod[:, 3], mod[:, 4], nu, router_t=moe_router[i].T)
        wg = moe_w_gate[i].astype(BF16)
        wu = moe_w_up[i].astype(BF16)
        wd = moe_w_down[i].astype(BF16)
        last = i == depth - 1
        x_all = _moe_stream(x_all, hf, aff, nu, seq, mod[1:2, 5], wg, wu, wd, rows_only=last)
        if not last:
            x_all = _moe_stream(x_all, hf, aff, 0, nu, mod[0:1, 5], wg, wu, wd)
    return x_all.reshape(bsz, seq, d)
```

```python
import functools
import math

import jax
import jax.numpy as jnp
import numpy as np
from jax import lax
from jax.experimental import pallas as pl
from jax.experimental.pallas import tpu as pltpu

F32 = jnp.float32
BF16 = jnp.bfloat16
I32 = jnp.int32

GRID_W = 64
ROPE_DIM = 64
ROPE_THETA = 10000.0
NORM_EPS = 1e-6
NEG_INF = -1e30
N_MOD = 6
A_HEADS = 16
A_HEAD_DIM = 64
A_LNX_EPS = 64e-5
B_Q_HEADS = 16
B_KV_HEADS = 4
B_HEAD_DIM = 64
WINDOW = 128
BLOCK = 128
C_HEADS = 8
C_NOPE_DIM = 128
C_ROPE_DIM = 64
C_V_DIM = 128
N_BRANCH = 3
CAPACITY_FACTOR = 2

LANES = 128
CHUNK = 64
SEQ_GROUP = 4
ROW_TILE = 256
MOE_TILE = 128
MOE_ALIGN = 16
MOE_WIN = MOE_TILE + MOE_ALIGN
SEL_ROWS = 128
VMEM_MB = 56


def _cp(n_grid, vmem_mb=VMEM_MB):
    return pltpu.CompilerParams(dimension_semantics=("arbitrary",) * n_grid,
                                vmem_limit_bytes=vmem_mb * 1024 * 1024)


def _pick(n, cands):
    for c in cands:
        if n % c == 0:
            return c
    raise ValueError(f"no tile for {n} in {cands}")


def _dg(a, b, ca=1, cb=0):
    return lax.dot_general(a, b, (((ca,), (cb,)), ((), ())), preferred_element_type=F32)


def _split2(x):
    hi = x.astype(BF16)
    lo = (x - hi.astype(F32)).astype(BF16)
    return hi, lo


def _mm(a, b, passes=1, nt=False):
    cb = 1 if nt else 0
    if passes == 1:
        return _dg(a.astype(BF16), b.astype(BF16), 1, cb)
    ah, al = _split2(a)
    bh, bl = _split2(b)
    return _dg(ah, bh, 1, cb) + (_dg(ah, bl, 1, cb) + _dg(al, bh, 1, cb))


def _mm_exact_rhs(x, m_bf16):
    x1 = x.astype(BF16)
    r1 = x - x1.astype(F32)
    x2 = r1.astype(BF16)
    x3 = (r1 - x2.astype(F32)).astype(BF16)
    return _dg(x1, m_bf16) + (_dg(x2, m_bf16) + _dg(x3, m_bf16))


def _sigmoid(x):
    return 1.0 / (1.0 + jnp.exp(-x))


def _iota(shape, dim):
    return lax.broadcasted_iota(I32, shape, dim)


def _block_ones(n, blk):
    i = _iota((n, n), 0) // blk
    j = _iota((n, n), 1) // blk
    return jnp.where(i == j, 1.0, 0.0).astype(BF16)


def _seg_sum(x, ones_bf16):
    return _mm_exact_rhs(x, ones_bf16)


def _swap16(x):
    n = x.shape[-1]
    lane = _iota(x.shape, x.ndim - 1)
    fwd = pltpu.roll(x, n - 16, x.ndim - 1)
    bwd = pltpu.roll(x, 16, x.ndim - 1)
    return jnp.where((lane % 32) < 16, fwd, bwd)


def _adaln_kernel(c_ref, dn_ref, up_ref, b_ref, o_ref):
    c = c_ref[...]
    t = _mm(c * _sigmoid(c), dn_ref[...], 3)
    o_ref[...] = _mm(t, up_ref[...], 3) + b_ref[...]


def _adaln(cond8, down, up, bias):
    d, r = down.shape
    n = up.shape[1]
    tn = _pick(n, (4096, 2048, 1024, 512, 256, 128))
    return pl.pallas_call(
        _adaln_kernel,
        grid=(n // tn,),
        in_specs=[pl.BlockSpec((8, d), lambda j: (0, 0)),
                  pl.BlockSpec((d, r), lambda j: (0, 0)),
                  pl.BlockSpec((r, tn), lambda j: (0, j)),
                  pl.BlockSpec((1, tn), lambda j: (0, j))],
        out_specs=pl.BlockSpec((8, tn), lambda j: (0, j)),
        out_shape=jax.ShapeDtypeStruct((8, n), F32),
        compiler_params=_cp(1),
        name="adaln",
    )(cond8, down, up, bias.reshape(1, n))


def _rows_are_ctx(tile_rows, row0, nu):
    return (row0 + _iota((tile_rows, 1), 0)) < nu


def _norm_mod_kernel(x_ref, g_ref, sh_ref, sc_ref, *rest, nu, tm, router):
    if router:
        wr_ref, h_ref, aff_ref = rest
    else:
        (h_ref,) = rest
    x = x_ref[...]
    is_u = pl.program_id(0) * tm < nu
    y = x * lax.rsqrt(jnp.mean(x * x, axis=-1, keepdims=True) + NORM_EPS) * g_ref[...]
    sh = jnp.where(is_u, sh_ref[0:1, :], sh_ref[1:2, :])
    sc = jnp.where(is_u, sc_ref[0:1, :], sc_ref[1:2, :])
    h = y * (1.0 + sc) + sh
    h_ref[...] = h.astype(h_ref.dtype)
    if router:
        logits = _mm(wr_ref[...], h, 3, nt=True)
        m = jnp.max(logits, axis=0, keepdims=True)
        p = jnp.exp(logits - m)
        aff_ref[...] = p / jnp.sum(p, axis=0, keepdims=True)


def _norm_mod(x_all, gain, shift2, scale2, nu, router_t=None):
    t, d = x_all.shape
    tm = ROW_TILE
    assert nu % tm == 0
    router = router_t is not None
    in_specs = [pl.BlockSpec((tm, d), lambda i: (i, 0)),
                pl.BlockSpec((1, d), lambda i: (0, 0)),
                pl.BlockSpec((2, d), lambda i: (0, 0)),
                pl.BlockSpec((2, d), lambda i: (0, 0))]
    args = [x_all, gain.reshape(1, d), shift2, scale2]
    out_specs = pl.BlockSpec((tm, d), lambda i: (i, 0))
    out_shape = jax.ShapeDtypeStruct((t, d), BF16)
    if router:
        e = router_t.shape[0]
        in_specs.append(pl.BlockSpec((e, d), lambda i: (0, 0)))
        args.append(router_t)
        out_specs = (out_specs, pl.BlockSpec((e, tm), lambda i: (0, i)))
        out_shape = (out_shape, jax.ShapeDtypeStruct((e, t), F32))
    return pl.pallas_call(
        functools.partial(_norm_mod_kernel, nu=nu, tm=tm, router=router),
        grid=(t // tm,), in_specs=in_specs, out_specs=out_specs, out_shape=out_shape,
        compiler_params=_cp(1), name="norm_mod_router" if router else "norm_mod",
    )(*args)


def _matmul_kernel(*refs, nu, tm, act, has_rms, has_resid):
    it = iter(refs)
    a_ref = next(it)
    w_ref = next(it)
    g_ref = next(it) if has_rms else None
    x_ref = next(it) if has_resid else None
    gate_ref = next(it) if has_resid else None
    o_ref = next(it)
    a = a_ref[...]
    if has_rms:
        af = a.astype(F32)
        a = af * lax.rsqrt(jnp.mean(af * af, axis=-1, keepdims=True) + NORM_EPS) * g_ref[...]
    acc = _dg(a.astype(BF16), w_ref[...])
    if act == "sigmoid":
        acc = _sigmoid(acc)
    if has_resid:
        is_u = _rows_are_ctx(tm, pl.program_id(1) * tm, nu)
        gate = jnp.where(is_u, gate_ref[0:1, :], gate_ref[1:2, :])
        acc = x_ref[...] + gate * acc
    o_ref[...] = acc.astype(o_ref.dtype)


def _matmul(a, w, out_dtype, *, a_col_block=0, w_col0=0, n=None, act=None, rms_gain=None, resid=None, gate2=None,
            nu=0, name="matmul"):
    m = a.shape[0]
    k = w.shape[0]
    n = w.shape[1] if n is None else n
    tm = _pick(m, (640, 256, 128))
    tn = n if n <= 2048 else _pick(n, (1024, 768, 512, 384, 256, 128))
    has_rms = rms_gain is not None
    has_resid = resid is not None
    in_specs = [pl.BlockSpec((tm, k), lambda j, i: (i, a_col_block)),
                pl.BlockSpec((pl.Element(k), pl.Element(tn)),
                             lambda j, i: (0, pl.multiple_of(w_col0 + j * tn, LANES)))]
    args = [a, w]
    if has_rms:
        in_specs.append(pl.BlockSpec((1, k), lambda j, i: (0, 0)))
        args.append(rms_gain.reshape(1, k))
    if has_resid:
        in_specs += [pl.BlockSpec((tm, tn), lambda j, i: (i, j)),
                     pl.BlockSpec((2, tn), lambda j, i: (0, j))]
        args += [resid, gate2]
    return pl.pallas_call(
        functools.partial(_matmul_kernel, nu=nu, tm=tm, act=act, has_rms=has_rms, has_resid=has_resid),
        grid=(n // tn, m // tm), in_specs=in_specs,
        out_specs=pl.BlockSpec((tm, tn), lambda j, i: (i, j)),
        out_shape=jax.ShapeDtypeStruct((m, n), out_dtype),
        compiler_params=_cp(2), name=name,
    )(*args)


def _rwkv_feat_kernel(z_ref, zp_ref, zn_ref, mu_ref, w0_ref, wup_ref, a0_ref, aup_ref, gup_ref,
                      r_ref, k_ref, v_ref, e_ref, a_ref, g_ref, *, nu, t_all, tm, aw, dr):
    z = z_ref[...]
    row = pl.program_id(0) * tm + _iota((tm, 1), 0)
    ri = _iota((tm, 1), 0)
    up1 = pltpu.roll(z, 1, 0)
    dn1 = pltpu.roll(z, tm - 1, 0)
    zp = jnp.where(ri == 0, zp_ref[7:8, :], up1)
    zn = jnp.where(ri == tm - 1, zn_ref[0:1, :], dn1)
    zp = jnp.where((row == 0) | (row == nu), 0.0, zp)
    zn = jnp.where((row == nu - 1) | (row == t_all - 1), 0.0, zn)
    zs = z + mu_ref[0:1, :] * (zp - z) + mu_ref[1:2, :] * (zn - z)
    r_ref[...] = zs[:, 0:aw]
    k_ref[...] = zs[:, aw:2 * aw]
    v_ref[...] = zs[:, 2 * aw:3 * aw]
    o = 3 * aw
    for n in range(2):
        wd = jnp.tanh(zs[:, o + n * dr:o + (n + 1) * dr])
        w = w0_ref[n:n + 1, :] + _mm(wd, wup_ref[n])
        sp = jnp.maximum(-w, 0.0) + jnp.log(1.0 + jnp.exp(-jnp.abs(w)))
        e_ref[n] = jnp.exp(-sp - 0.5)
    o += 2 * dr
    for n in range(2):
        ad = zs[:, o + n * dr:o + (n + 1) * dr]
        a_ref[n] = _sigmoid(a0_ref[n:n + 1, :] + _mm(ad, aup_ref[n]))
    o += 2 * dr
    g_ref[...] = _mm(_sigmoid(zs[:, o:]), gup_ref[...])


def _rwkv_feat(za, mu_p, w0, w_up, a0, a_up, g_up_p, nu):
    t, na = za.shape
    aw = w0.shape[1]
    dr = w_up.shape[1]
    tm = ROW_TILE
    nb8 = tm // 8
    last8 = t // 8 - 1
    row_spec = lambda w: pl.BlockSpec((tm, w), lambda i: (i, 0))
    full = lambda a: pl.BlockSpec(a.shape, lambda i: (0,) * a.ndim)
    out_rows = jax.ShapeDtypeStruct((t, aw), F32)
    out_dir = jax.ShapeDtypeStruct((2, t, aw), F32)
    dir_spec = pl.BlockSpec((2, tm, aw), lambda i: (0, i, 0))
    return pl.pallas_call(
        functools.partial(_rwkv_feat_kernel, nu=nu, t_all=t, tm=tm, aw=aw, dr=dr),
        grid=(t // tm,),
        in_specs=[row_spec(na),
                  pl.BlockSpec((8, na), lambda i: (jnp.maximum(i * nb8 - 1, 0), 0)),
                  pl.BlockSpec((8, na), lambda i: (jnp.minimum((i + 1) * nb8, last8), 0)),
                  full(mu_p), full(w0), full(w_up), full(a0), full(a_up), full(g_up_p)],
        out_specs=(row_spec(aw), row_spec(aw), row_spec(aw), dir_spec, dir_spec, row_spec(aw)),
        out_shape=(out_rows, out_rows, out_rows, out_dir, out_dir, out_rows),
        compiler_params=_cp(1), name="rwkv_feat",
    )(za, za, za, mu_p, w0, w_up, a0, a_up, g_up_p)


def _bmm(a, b, nt=False):
    cb = 2 if nt else 1
    return lax.dot_general(a.astype(BF16), b.astype(BF16), (((2,), (cb,)), ((0,), (0,))),
                           preferred_element_type=F32)


def _bmm_exact_lhs(m_bf16, x):
    dn = (((2,), (1,)), ((0,), (0,)))
    x1 = x.astype(BF16)
    r1 = x - x1.astype(F32)
    x2 = r1.astype(BF16)
    x3 = (r1 - x2.astype(F32)).astype(BF16)
    d = lambda y: lax.dot_general(m_bf16, y, dn, preferred_element_type=F32)
    return d(x1) + (d(x2) + d(x3))


def _bt(x):
    return jnp.stack([x[g].T for g in range(x.shape[0])], axis=0)


def _rwkv_chunk_math(r, k, v, e, a, kk_gain, ka_gain, reverse, groups):
    c = CHUNK
    hd = A_HEAD_DIM
    lane = _iota((1, LANES), 1)
    m_a = jnp.where(lane < hd, 1.0, 0.0)
    m_b = 1.0 - m_a
    ones_seg = _block_ones(LANES, hd)
    ti = _iota((groups, c, c), 1)
    tj = _iota((groups, c, c), 2)
    tri = jnp.where((tj >= ti) if reverse else (tj <= ti), 1.0, 0.0).astype(BF16)

    kk0 = k * kk_gain
    kk = kk0 / jnp.maximum(jnp.sqrt(_seg_sum(kk0 * kk0, ones_seg)), 1e-12)
    kt = k * (1.0 + (a - 1.0) * ka_gain)
    b = kk * a
    g3 = lambda x: x.reshape(groups, c, LANES)
    e3 = g3(e)
    cl = _bmm_exact_lhs(tri, e3)
    last = 0 if reverse else c - 1
    ctot = cl[:, last:last + 1, :]
    g_in = jnp.exp(-cl)
    g_ex = jnp.exp(e3 - cl)
    g_inv = jnp.exp(cl)
    g_end = jnp.exp(cl - ctot)
    st = lambda x: jnp.concatenate([x * m_a, x * m_b], axis=1)
    kk3, b3, kt3 = g3(kk), g3(b), g3(kt)
    kk2 = st(kk3 * g_ex)
    r2 = st(g3(r) * g_in)
    b2 = st(b3 * g_inv)
    k2 = st(kt3 * g_inv)
    v2 = st(g3(v))
    bg2 = st(b3 * g_end)
    kg2 = st(kt3 * g_end)

    s = _bmm(jnp.concatenate([kk2, r2], axis=1), jnp.concatenate([b2, k2], axis=1), nt=True)
    n2 = 2 * c
    i2 = _iota((n2, n2), 0)
    j2 = _iota((n2, n2), 1)
    il = i2 % c
    jl = j2 % c
    strict = (jl > il) if reverse else (jl < il)
    incl = (jl >= il) if reverse else (jl <= il)
    a_b = jnp.where(strict, s[:, :n2, :n2], 0.0)
    a_k = jnp.where(strict, s[:, :n2, n2:], 0.0)
    l_b = jnp.where(incl, s[:, n2:, :n2], 0.0)
    l_k = jnp.where(incl, s[:, n2:, n2:], 0.0)
    eye = jnp.where(i2 == j2, 1.0, 0.0)

    same = lambda m: (i2 // m) == (j2 // m)
    a_d = jnp.where(same(8), a_b, 0.0)
    a_d2 = _bmm(a_d, a_d)
    a_d4 = _bmm(a_d2, a_d2)
    t_inv = _bmm(_bmm(eye - a_d, eye + a_d2), eye + a_d4)
    m = 8
    while m < c:
        a_off = jnp.where(same(2 * m) & jnp.logical_not(same(m)), a_b, 0.0)
        t_inv = t_inv - _bmm(_bmm(t_inv, a_off), t_inv)
        m *= 2

    av = _bmm(a_k, v2)
    z12 = _bmm(t_inv, jnp.concatenate([kk2, av], axis=2))
    z1 = z12[:, :, :LANES]
    z2 = z12[:, :, LANES:]
    vz = jnp.concatenate([v2, z2], axis=1)
    p_bd = eye * jnp.exp(-ctot) - _bmm(_bt(bg2), z1)
    q_bd = _bmm(_bt(jnp.concatenate([kg2, -bg2], axis=1)), vz)
    r2s = r2 - _bmm(l_b, z1)
    y0s = _bmm(jnp.concatenate([l_k, -l_b], axis=2), vz)
    cp = lambda x: x[:, :x.shape[1] // 2] + x[:, x.shape[1] // 2:]
    return cp(p_bd), cp(q_bd), cp(r2s).reshape(groups * c, LANES), cp(y0s).reshape(groups * c, LANES)


def _rwkv_chunk_kernel(r_ref, k_ref, v_ref, e_ref, a_ref, kkg_ref, kag_ref,
                       p_ref, q_ref, r2_ref, y0_ref, *, reverse, groups):
    p, q, r2, y0 = _rwkv_chunk_math(r_ref[...], k_ref[...], v_ref[...], e_ref[0], a_ref[0],
                                    kkg_ref[...], kag_ref[...], reverse, groups)
    p_ref[0] = p
    q_ref[0] = q
    r2_ref[...] = r2
    y0_ref[...] = y0


def _rwkv_chunks(r, k, v, e, a, k_k, k_a, direction):
    t, aw = r.shape
    npair = aw // LANES
    nc = t // CHUNK
    groups = _pick(nc, (20, 10, 4))
    rows = groups * CHUNK
    row_spec = pl.BlockSpec((rows, LANES), lambda p, i: (i, p))
    dir_spec = pl.BlockSpec((1, rows, LANES), lambda p, i: (direction, i, p))
    par_spec = pl.BlockSpec((1, LANES), lambda p, i: (0, p))
    pq_spec = pl.BlockSpec((1, groups, A_HEAD_DIM, LANES), lambda p, i: (p, i, 0, 0))
    pq_shape = jax.ShapeDtypeStruct((npair, nc, A_HEAD_DIM, LANES), F32)
    ry_shape = jax.ShapeDtypeStruct((t, aw), F32)
    return pl.pallas_call(
        functools.partial(_rwkv_chunk_kernel, reverse=bool(direction), groups=groups),
        grid=(npair, t // rows),
        in_specs=[row_spec, row_spec, row_spec, dir_spec, dir_spec, par_spec, par_spec],
        out_specs=(pq_spec, pq_spec, row_spec, row_spec),
        out_shape=(pq_shape, pq_shape, ry_shape, ry_shape),
        compiler_params=_cp(2), name="rwkv_chunks_bwd" if direction else "rwkv_chunks_fwd",
    )(r, k, v, e, a, k_k.reshape(1, aw), k_a.reshape(1, aw))


def _pair_block_diag(x):
    lane = _iota(x.shape, 1)
    return jnp.concatenate([jnp.where(lane < A_HEAD_DIM, x, 0.0), jnp.where(lane >= A_HEAD_DIM, x, 0.0)], axis=0)


def _rwkv_seq_kernel(p_ref, q_ref, r2_ref, y0_ref, y_ref, h_ref, *, reverse, groups, npair):
    c = CHUNK

    @pl.when(pl.program_id(0) == 0)
    def _():
        h_ref[...] = jnp.zeros_like(h_ref)

    order = range(groups - 1, -1, -1) if reverse else range(groups)
    for g in order:
        sl = slice(g * c, (g + 1) * c)
        for p in range(npair):
            ls = slice(p * LANES, (p + 1) * LANES)
            h = h_ref[p]
            y_ref[sl, ls] = _mm(r2_ref[sl, ls], h) + y0_ref[sl, ls]
            h_ref[p] = _mm(_pair_block_diag(p_ref[p, g]), h) + _pair_block_diag(q_ref[p, g])


def _rwkv_seq(p, q, r2, y0, direction, nu):
    npair, nc = p.shape[:2]
    t, aw = r2.shape
    groups = SEQ_GROUP
    rows = groups * CHUNK
    nb = t // rows
    nbu = nu // rows
    assert nu % rows == 0
    if direction:
        blk = lambda s: jnp.where(s < nbu, nbu - 1 - s, nb + nbu - 1 - s)
    else:
        blk = lambda s: s
    pq_spec = pl.BlockSpec((npair, groups, A_HEAD_DIM, LANES), lambda s: (0, blk(s), 0, 0))
    row_spec = pl.BlockSpec((rows, aw), lambda s: (blk(s), 0))
    return pl.pallas_call(
        functools.partial(_rwkv_seq_kernel, reverse=bool(direction), groups=groups, npair=npair),
        grid=(nb,),
        in_specs=[pq_spec, pq_spec, row_spec, row_spec],
        out_specs=row_spec,
        out_shape=jax.ShapeDtypeStruct((t, aw), F32),
        scratch_shapes=[pltpu.VMEM((npair, LANES, LANES), F32)],
        compiler_params=_cp(1), name="rwkv_seq_bwd" if direction else "rwkv_seq_fwd",
    )(p, q, r2, y0)


def _rwkv_out_kernel(yf_ref, yb_ref, r_ref, k_ref, v_ref, a_ref, g_ref, lw_ref, lb_ref, rk_ref, ka_ref, o_ref, *, aw):
    hd = A_HEAD_DIM
    ones_seg = _block_ones(LANES, hd)
    for p in range(aw // LANES):
        ls = slice(p * LANES, (p + 1) * LANES)
        y = yf_ref[:, ls] + yb_ref[:, ls]
        mean = _seg_sum(y, ones_seg) * (1.0 / hd)
        yc = y - mean
        var = _seg_sum(yc * yc, ones_seg) * (1.0 / hd)
        yn = yc * lax.rsqrt(var + A_LNX_EPS) * lw_ref[:, ls] + lb_ref[:, ls]
        r = r_ref[:, ls]
        k = k_ref[:, ls]
        v = v_ref[:, ls]
        bonus = jnp.zeros_like(y)
        for n in range(2):
            kt = k * (1.0 + (a_ref[n, :, ls] - 1.0) * ka_ref[:, ls])
            bonus = bonus + _seg_sum(r * kt * rk_ref[:, ls], ones_seg) * v
        o_ref[:, ls] = ((yn + bonus) * g_ref[:, ls]).astype(o_ref.dtype)


def _rwkv_out(yf, yb, r, k, v, a, g, lnx_w, lnx_b, r_k, k_a):
    t, aw = r.shape
    tm = ROW_TILE
    row_spec = pl.BlockSpec((tm, aw), lambda i: (i, 0))
    par_spec = pl.BlockSpec((1, aw), lambda i: (0, 0))
    return pl.pallas_call(
        functools.partial(_rwkv_out_kernel, aw=aw),
        grid=(t // tm,),
        in_specs=[row_spec] * 5 + [pl.BlockSpec((2, tm, aw), lambda i: (0, i, 0)), row_spec] + [par_spec] * 4,
        out_specs=row_spec,
        out_shape=jax.ShapeDtypeStruct((t, aw), BF16),
        compiler_params=_cp(1), name="rwkv_out",
    )(yf, yb, r, k, v, a, g, lnx_w.reshape(1, aw), lnx_b.reshape(1, aw), r_k.reshape(1, aw), k_a.reshape(1, aw))


def _rms_rope_slab(x, gain, cos, sin, ones_seg):
    ms = _seg_sum(x * x, ones_seg) * (1.0 / B_HEAD_DIM)
    y = x * lax.rsqrt(ms + NORM_EPS) * gain
    return y * cos + _swap16(y) * sin


def _gqa_prep_kernel(z_ref, cos_ref, sin_ref, qg_ref, kg_ref, o_ref, *, qw, kw):
    ones_seg = _block_ones(LANES, B_HEAD_DIM)
    cos = cos_ref[...]
    sin = sin_ref[...]
    scale = B_HEAD_DIM ** -0.5 * math.log2(math.e)
    for s in range(qw // LANES):
        ls = slice(s * LANES, (s + 1) * LANES)
        o_ref[:, ls] = (_rms_rope_slab(z_ref[:, ls], qg_ref[...], cos, sin, ones_seg) * scale).astype(o_ref.dtype)
    for s in range(kw // LANES):
        ls = slice(qw + s * LANES, qw + (s + 1) * LANES)
        o_ref[:, ls] = _rms_rope_slab(z_ref[:, ls], kg_ref[...], cos, sin, ones_seg).astype(o_ref.dtype)
    o_ref[:, qw + kw:] = z_ref[:, qw + kw:].astype(o_ref.dtype)


def _gqa_prep(zb, cos, sin, q_norm, k_norm):
    t, nb = zb.shape
    tm = ROW_TILE
    qw = B_Q_HEADS * B_HEAD_DIM
    kw = B_KV_HEADS * B_HEAD_DIM
    tile2 = lambda g: jnp.tile(g.reshape(1, B_HEAD_DIM), (1, LANES // B_HEAD_DIM))
    return pl.pallas_call(
        functools.partial(_gqa_prep_kernel, qw=qw, kw=kw),
        grid=(t // tm,),
        in_specs=[pl.BlockSpec((tm, nb), lambda i: (i, 0)),
                  pl.BlockSpec((tm, LANES), lambda i: (i, 0)),
                  pl.BlockSpec((tm, LANES), lambda i: (i, 0)),
                  pl.BlockSpec((1, LANES), lambda i: (0, 0)),
                  pl.BlockSpec((1, LANES), lambda i: (0, 0))],
        out_specs=pl.BlockSpec((tm, nb), lambda i: (i, 0)),
        out_shape=jax.ShapeDtypeStruct((t, nb), BF16),
        compiler_params=_cp(1), name="gqa_prep",
    )(zb, cos, sin, tile2(q_norm), tile2(k_norm))


def _gqa_attn_kernel(sink_ref, q_ref, c_ref, kp_ref, ko_ref, kn_ref, o_ref, *, nu, seq, qw, kw):
    hd = B_HEAD_DIM
    grp = B_Q_HEADS // B_KV_HEADS
    blk = BLOCK
    j = pl.program_id(0)
    jb = j - nu // blk
    nkeys = nu + 3 * blk
    rows = grp * blk
    qi = _iota((rows, nkeys), 0) % blk
    kc = _iota((rows, nkeys), 1)
    q_pos = jb * blk + qi
    k_pos = (jb - 1) * blk + (kc - nu)
    band_ok = (jnp.abs(k_pos - q_pos) <= WINDOW) & (k_pos >= 0) & (k_pos < seq) & (jb >= 0)
    valid = (kc < nu) | band_ok
    rg = _iota((rows, 1), 0) // blk
    outs = []
    for h in range(B_KV_HEADS):
        ks = slice(qw + h * hd, qw + (h + 1) * hd)
        vs = slice(qw + kw + h * hd, qw + kw + (h + 1) * hd)
        k_all = jnp.concatenate([c_ref[:, ks], kp_ref[:, ks], ko_ref[:, ks], kn_ref[:, ks]], axis=0)
        v_all = jnp.concatenate([c_ref[:, vs], kp_ref[:, vs], ko_ref[:, vs], kn_ref[:, vs]], axis=0)
        q4 = jnp.concatenate([q_ref[:, (h * grp + g) * hd:(h * grp + g + 1) * hd] for g in range(grp)], axis=0)
        s = jnp.where(valid, _dg(q4, k_all, 1, 1), NEG_INF)
        sink = jnp.zeros((rows, 1), F32)
        for g in range(grp):
            sink = jnp.where(rg == g, sink_ref[h * grp + g] * math.log2(math.e), sink)
        m = jnp.maximum(jnp.max(s, axis=1, keepdims=True), sink)
        p = jnp.exp2((s - m).astype(BF16))
        pv = _dg(p, jnp.concatenate([v_all, jnp.ones_like(v_all)], axis=1))
        o = pv[:, :hd] / (pv[:, hd:hd + 1] + jnp.exp2(sink - m))
        outs += [o[g * blk:(g + 1) * blk, :] for g in range(grp)]
    o_ref[...] = jnp.concatenate(outs, axis=1).astype(o_ref.dtype)


def _gqa_attn(qkv, sink, nu, seq):
    t, nb = qkv.shape
    qw = B_Q_HEADS * B_HEAD_DIM
    kw = B_KV_HEADS * B_HEAD_DIM
    blk = BLOCK
    nblk = t // blk
    band = lambda off: pl.BlockSpec((blk, nb), lambda j: (jnp.clip(j + off, 0, nblk - 1), 0))
    return pl.pallas_call(
        functools.partial(_gqa_attn_kernel, nu=nu, seq=seq, qw=qw, kw=kw),
        grid=(nblk,),
        in_specs=[pl.BlockSpec(memory_space=pltpu.SMEM),
                  pl.BlockSpec((blk, nb), lambda j: (j, 0)),
                  pl.BlockSpec((nu, nb), lambda j: (0, 0)),
                  band(-1), band(0), band(1)],
        out_specs=pl.BlockSpec((blk, qw), lambda j: (j, 0)),
        out_shape=jax.ShapeDtypeStruct((t, qw), BF16),
        compiler_params=_cp(1), name="gqa_attn",
    )(sink, qkv, qkv, qkv, qkv, qkv)


def _mla_prep_kernel(q_ref, kv_ref, kr_ref, cos_ref, sin_ref, nn_ref, rn_ref, qo_ref, ko_ref, vo_ref):
    cos = cos_ref[...]
    sin = sin_ref[...]
    dn = C_NOPE_DIM
    hw = 2 * LANES
    scale = (C_NOPE_DIM + C_ROPE_DIM) ** -0.5 * math.log2(math.e)
    ones = jnp.ones((q_ref.shape[0], C_V_DIM), vo_ref.dtype)

    def rms_rope(x, gain):
        ms = jnp.sum(x * x, axis=-1, keepdims=True) * (1.0 / C_ROPE_DIM)
        y = x * lax.rsqrt(ms + NORM_EPS) * gain
        return y * cos + _swap16(y) * sin

    def rms(x, gain):
        return x * lax.rsqrt(jnp.mean(x * x, axis=-1, keepdims=True) + NORM_EPS) * gain

    kr = rms_rope(kr_ref[...], rn_ref[1:2, :]).astype(ko_ref.dtype)
    for h in range(C_HEADS):
        qn = rms(q_ref[:, h * hw:h * hw + dn], nn_ref[0:1, :])
        qr = rms_rope(q_ref[:, h * hw + dn:(h + 1) * hw], rn_ref[0:1, :])
        qo_ref[:, h * hw:h * hw + dn] = (qn * scale).astype(qo_ref.dtype)
        qo_ref[:, h * hw + dn:(h + 1) * hw] = (qr * scale).astype(qo_ref.dtype)
        ko_ref[:, h * hw:h * hw + dn] = rms(kv_ref[:, h * dn:(h + 1) * dn], nn_ref[1:2, :]).astype(ko_ref.dtype)
        ko_ref[:, h * hw + dn:(h + 1) * hw] = kr
        vo_ref[:, h * hw:h * hw + C_V_DIM] = kv_ref[:, (C_HEADS + h) * dn:(C_HEADS + h + 1) * dn].astype(vo_ref.dtype)
        vo_ref[:, h * hw + C_V_DIM:(h + 1) * hw] = ones


def _mla_prep(q, kv, zc, kr_col_block, cos, sin, nope_norm, rope_norm_p):
    t = q.shape[0]
    tm = ROW_TILE
    hw = 2 * LANES
    row = lambda w: pl.BlockSpec((tm, w), lambda i: (i, 0))
    full = lambda a: pl.BlockSpec(a.shape, lambda i: (0, 0))
    return pl.pallas_call(
        _mla_prep_kernel,
        grid=(t // tm,),
        in_specs=[row(q.shape[1]), row(kv.shape[1]),
                  pl.BlockSpec((tm, LANES), lambda i: (i, kr_col_block)),
                  row(LANES), row(LANES), full(nope_norm), full(rope_norm_p)],
        out_specs=(row(C_HEADS * hw),) * 3,
        out_shape=(jax.ShapeDtypeStruct((t, C_HEADS * hw), BF16),) * 3,
        compiler_params=_cp(1), name="mla_prep",
    )(q, kv, zc, cos, sin, nope_norm, rope_norm_p)


def _mla_flash_kernel(q_ref, k_ref, v_ref, o_ref, m_ref, acc_ref, *, nu, tq, tk, sub):
    qi = pl.program_id(1)
    ki = pl.program_id(2)

    @pl.when(ki == 0)
    def _():
        m_ref[...] = jnp.full_like(m_ref, NEG_INF)
        acc_ref[...] = jnp.zeros_like(acc_ref)

    def update(masked):
        for qs in range(tq // sub):
            rs = slice(qs * sub, (qs + 1) * sub)
            s = _dg(q_ref[rs, :], k_ref[...], 1, 1)
            if masked and qs * sub < nu:
                qrow = qi * tq + qs * sub + _iota((sub, tk), 0)
                kcol = ki * tk + _iota((sub, tk), 1)
                s = jnp.where((qrow < nu) & (kcol >= nu), NEG_INF, s)
            m_prev = m_ref[rs, :]
            m_next = jnp.maximum(m_prev, jnp.max(s, axis=1, keepdims=True))
            alpha = jnp.exp2(m_prev - m_next)
            p = jnp.exp2((s - m_next[:, 0:1]).astype(BF16))
            acc_ref[rs, :] = jnp.concatenate([alpha, alpha], axis=1) * acc_ref[rs, :] + _dg(p, v_ref[...])
            m_ref[rs, :] = m_next

    has_ctx_rows = qi * tq < nu

    @pl.when(has_ctx_rows)
    def _():
        update(True)

    @pl.when(jnp.logical_not(has_ctx_rows))
    def _():
        update(False)

    @pl.when(ki == pl.num_programs(2) - 1)
    def _():
        o_ref[...] = (acc_ref[:, :C_V_DIM] / acc_ref[:, C_V_DIM:]).astype(o_ref.dtype)


def _mla_flash(qf, kf, vf, nu):
    t = qf.shape[0]
    hw = 2 * LANES
    tq = _pick(t, (3328, 256))
    tk = _pick(t, (3328, 256))
    return pl.pallas_call(
        functools.partial(_mla_flash_kernel, nu=nu, tq=tq, tk=tk, sub=256),
        grid=(C_HEADS, t // tq, t // tk),
        in_specs=[pl.BlockSpec((tq, hw), lambda h, i, j: (i, h)),
                  pl.BlockSpec((tk, hw), lambda h, i, j: (j, h)),
                  pl.BlockSpec((tk, hw), lambda h, i, j: (j, h))],
        out_specs=pl.BlockSpec((tq, C_V_DIM), lambda h, i, j: (i, h)),
        out_shape=jax.ShapeDtypeStruct((t, C_HEADS * C_V_DIM), BF16),
        scratch_shapes=[pltpu.VMEM((tq, LANES), F32), pltpu.VMEM((tq, hw), F32)],
        compiler_params=_cp(3), name="mla_flash",
    )(qf, kf, vf)


def _merge_kernel(ya_ref, yb_ref, yc_ref, w_ref, g0_ref, g1_ref, g2_ref, o_ref):
    acc = g0_ref[...].astype(F32) * _dg(ya_ref[...], w_ref[0])
    acc = acc + g1_ref[...].astype(F32) * _dg(yb_ref[...], w_ref[1])
    acc = acc + g2_ref[...].astype(F32) * _dg(yc_ref[...], w_ref[2])
    o_ref[...] = acc.astype(o_ref.dtype)


def _merge(ya, yb, yc, w_branch, gates):
    t, bw = ya.shape
    d = w_branch.shape[2]
    tm = _pick(t, (640, 256, 128))
    tn = _pick(d, (1024, 512, 256, 128))
    nj = d // tn
    y_spec = pl.BlockSpec((tm, bw), lambda j, i: (i, 0))
    gate = lambda b: pl.BlockSpec((tm, tn), lambda j, i: (i, b * nj + j))
    return pl.pallas_call(
        _merge_kernel,
        grid=(nj, t // tm),
        in_specs=[y_spec, y_spec, y_spec, pl.BlockSpec((N_BRANCH, bw, tn), lambda j, i: (0, 0, j)),
                  gate(0), gate(1), gate(2)],
        out_specs=pl.BlockSpec((tm, tn), lambda j, i: (i, j)),
        out_shape=jax.ShapeDtypeStruct((t, d), BF16),
        compiler_params=_cp(2), name="merge",
    )(ya, yb, yc, w_branch, gates, gates, gates)


def _moe_select_kernel(aff_ref, loc_ref, base_ref, *, cap, n_exp):
    aff = aff_ref[...]
    bits = pltpu.bitcast(aff, I32)
    count = lambda m: jnp.sum(jnp.sum(jnp.where(m, 1.0, 0.0), axis=1, keepdims=True), axis=2, keepdims=True)
    theta = jnp.zeros((n_exp, 1, 1), I32)
    for bit in range(30, -1, -1):
        cand = theta | (1 << bit)
        theta = jnp.where(count(bits >= cand) >= cap, cand, theta)
    gt = bits > theta
    eq = bits == theta
    need = cap - count(gt)
    r = aff.shape[1]
    upper = jnp.where(_iota((LANES, LANES), 0) <= _iota((LANES, LANES), 1), 1.0, 0.0).astype(BF16)
    lower_strict = jnp.where(_iota((r, r), 1) < _iota((r, r), 0), 1.0, 0.0).astype(BF16)

    def prefix(x):
        incl = _dg(x.astype(BF16), upper)
        tot = jnp.broadcast_to(incl[:, LANES - 1:LANES], (r, LANES))
        return incl, _dg(lower_strict, tot.astype(BF16))

    for e in range(n_exp):
        xe = jnp.where(eq[e], 1.0, 0.0)
        incl, base = prefix(xe)
        take = eq[e] & ((incl - xe + base) < need[e])
        sel = jnp.where(gt[e] | take, 1.0, 0.0)
        incl, base = prefix(sel)
        loc_ref[e] = jnp.where(sel > 0.0, incl - sel, -4096.0)
        base_ref[e] = base


def _moe_select(aff3, cap):
    e, r, _ = aff3.shape
    spec = pl.BlockSpec((e, r, LANES), lambda i: (0, 0, 0))
    shape = jax.ShapeDtypeStruct((e, r, LANES), F32)
    return pl.pallas_call(
        functools.partial(_moe_select_kernel, cap=cap, n_exp=e),
        grid=(1,), in_specs=[spec], out_specs=(spec, spec), out_shape=(shape, shape),
        compiler_params=_cp(1), name="moe_select",
    )(aff3)


def _window_start(base_ref, e, i):
    b = base_ref[e, i]
    a = (b // MOE_ALIGN) * MOE_ALIGN
    return pl.multiple_of(a, MOE_ALIGN), b - a


def _moe_gather_kernel(base_ref, loc_ref, aff_ref, h_ref, xe_ref, ge_ref, *, tiles, group):
    g = pl.program_id(0)
    i = pl.program_id(2)

    @pl.when(i == 0)
    def _():
        xe_ref[...] = jnp.zeros_like(xe_ref)
        ge_ref[...] = jnp.zeros_like(ge_ref)

    win_row = _iota((MOE_WIN, MOE_TILE), 0).astype(F32)
    for k in range(group):
        e = g * group + k
        for r in range(tiles):
            a, off = _window_start(base_ref, e, i * tiles + r)
            slot = loc_ref[k, r] + off.astype(F32)
            onehot = jnp.where(win_row == slot, 1.0, 0.0)
            rows = _dg(onehot.astype(BF16), h_ref[r * MOE_TILE:(r + 1) * MOE_TILE, :])
            head = pl.ds(a, MOE_ALIGN)
            xe_ref[k, head, :] = (xe_ref[k, head, :].astype(F32) + rows[:MOE_ALIGN]).astype(xe_ref.dtype)
            xe_ref[k, pl.ds(a + MOE_ALIGN, MOE_TILE), :] = rows[MOE_ALIGN:].astype(xe_ref.dtype)
            win = pl.ds(a, MOE_WIN)
            gsel = jnp.sum(onehot * aff_ref[k, r], axis=1, keepdims=True)
            ge_ref[k, win, :] = ge_ref[k, win, :] + gsel


def _moe_gather(base_i, loc, aff3, h_all, row0, n, cap_p):
    e = loc.shape[0]
    d = h_all.shape[1]
    nt = n // MOE_TILE
    tiles = _pick(nt, (4, 2, 1))
    rows = tiles * MOE_TILE
    group = _pick(e, (4, 2, 1))
    dcol = min(d, 1024)
    sel_spec = pl.BlockSpec((group, tiles, 1, LANES), lambda g, c, i, b: (g, i, 0, 0))
    grid_spec = pltpu.PrefetchScalarGridSpec(
        num_scalar_prefetch=1, grid=(e // group, d // dcol, nt // tiles),
        in_specs=[sel_spec, sel_spec,
                  pl.BlockSpec((pl.Element(rows), pl.Element(dcol)),
                               lambda g, c, i, b: (pl.multiple_of(row0 + i * rows, MOE_TILE),
                                                   pl.multiple_of(c * dcol, LANES)))],
        out_specs=(pl.BlockSpec((group, cap_p, dcol), lambda g, c, i, b: (g, 0, c)),
                   pl.BlockSpec((group, cap_p, LANES), lambda g, c, i, b: (g, 0, 0))))
    return pl.pallas_call(
        functools.partial(_moe_gather_kernel, tiles=tiles, group=group), grid_spec=grid_spec,
        out_shape=(jax.ShapeDtypeStruct((e, cap_p, d), BF16), jax.ShapeDtypeStruct((e, cap_p, LANES), F32)),
        compiler_params=_cp(3), name="moe_gather",
    )(base_i, loc, aff3, h_all)


def _moe_ffn_kernel(x_ref, g_ref, wg_ref, wu_ref, wd_ref, y_ref):
    x = x_ref[0]
    hg = _dg(x, wg_ref[0])
    hid = hg * _sigmoid(hg) * _dg(x, wu_ref[0])
    y = _dg(hid.astype(BF16), wd_ref[0]) * g_ref[0][:, 0:1]
    y_ref[0] = y.astype(y_ref.dtype)


def _moe_ffn(xe, ge, w_gate, w_up, w_down, tc):
    e, cap_p, d = xe.shape
    f = w_gate.shape[2]
    return pl.pallas_call(
        _moe_ffn_kernel,
        grid=(e, cap_p // tc),
        in_specs=[pl.BlockSpec((1, tc, d), lambda ee, i: (ee, i, 0)),
                  pl.BlockSpec((1, tc, LANES), lambda ee, i: (ee, i, 0)),
                  pl.BlockSpec((1, d, f), lambda ee, i: (ee, 0, 0)),
                  pl.BlockSpec((1, d, f), lambda ee, i: (ee, 0, 0)),
                  pl.BlockSpec((1, f, d), lambda ee, i: (ee, 0, 0))],
        out_specs=pl.BlockSpec((1, tc, d), lambda ee, i: (ee, i, 0)),
        out_shape=jax.ShapeDtypeStruct((e, cap_p, d), BF16),
        compiler_params=_cp(2), name="moe_ffn",
    )(xe, ge, w_gate, w_up, w_down)


def _moe_combine_kernel(base_ref, loc_ref, *rest, n_exp):
    y_refs = rest[:n_exp]
    x_ref, gate_ref, o_ref = rest[n_exp:]
    i = pl.program_id(0)
    win_row = _iota((MOE_TILE, MOE_WIN), 1).astype(F32)
    acc = None
    for e in range(n_exp):
        _, off = _window_start(base_ref, e, i)
        slot_row = jnp.broadcast_to(loc_ref[e, 0] + off.astype(F32), (MOE_TILE, LANES))
        slot_col = slot_row.T
        slot_col = jnp.concatenate([slot_col, slot_col[:, :MOE_WIN - LANES]], axis=1)
        onehot = jnp.where(win_row == slot_col, 1.0, 0.0).astype(BF16)
        part = _dg(onehot, y_refs[e][...])
        acc = part if acc is None else acc + part
    o_ref[...] = x_ref[...] + gate_ref[...] * acc


def _moe_combine(base_i, loc, ye, x_all, gate_row, row0, n, rows_only):
    e, cap_p, d = ye.shape
    nt = n // MOE_TILE
    t0 = row0 // MOE_TILE
    o0 = 0 if rows_only else t0
    dcol = d

    def y_spec(ee):
        def index(i, j, b):
            start = ee * cap_p + (b[ee, i] // MOE_ALIGN) * MOE_ALIGN
            return pl.multiple_of(start, MOE_ALIGN), pl.multiple_of(j * dcol, LANES)
        return pl.BlockSpec((pl.Element(MOE_WIN), pl.Element(dcol)), index)

    grid_spec = pltpu.PrefetchScalarGridSpec(
        num_scalar_prefetch=1, grid=(nt, d // dcol),
        in_specs=[pl.BlockSpec((e, 1, 1, LANES), lambda i, j, b: (0, i, 0, 0))]
                 + [y_spec(ee) for ee in range(e)]
                 + [pl.BlockSpec((MOE_TILE, dcol), lambda i, j, b: (t0 + i, j)),
                    pl.BlockSpec((1, dcol), lambda i, j, b: (0, j))],
        out_specs=pl.BlockSpec((MOE_TILE, dcol), lambda i, j, b: (o0 + i, j)))
    x_index = 2 + e
    return pl.pallas_call(
        functools.partial(_moe_combine_kernel, n_exp=e), grid_spec=grid_spec,
        out_shape=jax.ShapeDtypeStruct((n, d) if rows_only else x_all.shape, F32),
        input_output_aliases={} if rows_only else {x_index: 0},
        compiler_params=_cp(2), name="moe_combine",
    )(base_i, loc, *([ye.reshape(e * cap_p, d)] * e), x_all, gate_row)


def _moe_stream(x_all, h_all, aff, row0, n, gate_row, w_gate, w_up, w_down, rows_only=False):
    e = aff.shape[0]
    cap = (CAPACITY_FACTOR * n) // e
    tc = 256 if cap >= 256 else 64
    cap_p = -(-(cap + MOE_WIN) // tc) * tc
    n_sel = SEL_ROWS * LANES
    aff_s = lax.dynamic_slice_in_dim(aff, row0, n, axis=1)
    aff3 = jnp.pad(aff_s, ((0, 0), (0, n_sel - n))).reshape(e, SEL_ROWS, LANES)
    loc, base = _moe_select(aff3, cap)
    base_i = base[:, :, 0].astype(I32)
    loc4 = loc.reshape(e, SEL_ROWS, 1, LANES)
    aff4 = aff3.reshape(e, SEL_ROWS, 1, LANES)
    xe, ge = _moe_gather(base_i, loc4, aff4, h_all, row0, n, cap_p)
    ye = _moe_ffn(xe, ge, w_gate, w_up, w_down, tc)
    return _moe_combine(base_i, loc4, ye, x_all, gate_row, row0, n, rows_only)


def _rope_tables(nu, seq):
    n_freq = ROPE_DIM // 4
    pos = jnp.arange(seq)
    inv = jnp.power(ROPE_THETA, -jnp.arange(n_freq, dtype=F32) / n_freq)
    ang_r = (pos // GRID_W).astype(F32)[:, None] * inv[None]
    ang_c = (pos % GRID_W).astype(F32)[:, None] * inv[None]
    cos = jnp.concatenate([jnp.cos(ang_r)] * 2 + [jnp.cos(ang_c)] * 2, axis=1)
    sin = jnp.concatenate([-jnp.sin(ang_r), jnp.sin(ang_r), -jnp.sin(ang_c), jnp.sin(ang_c)], axis=1)
    cos = jnp.concatenate([jnp.ones((nu, ROPE_DIM), F32), cos], axis=0)
    sin = jnp.concatenate([jnp.zeros((nu, ROPE_DIM), F32), sin], axis=0)
    rep = LANES // ROPE_DIM
    return jnp.tile(cos, (1, rep)), jnp.tile(sin, (1, rep))


def _pad_cols(w, n):
    return jnp.pad(w, ((0, 0), (0, n - w.shape[1])))


def _round_up(x, m):
    return -(-x // m) * m


def kernel(x, c, ctx, c_ctx, norm_mix, norm_ffn, ada_down, ada_up, ada_bias, w_in, rwkv_mu, rwkv_w0, rwkv_w_up, rwkv_a0, rwkv_a_up, rwkv_g_up, rwkv_k_k, rwkv_k_a, rwkv_r_k, rwkv_lnx_w, rwkv_lnx_b, gqa_q_norm, gqa_k_norm, gqa_sink, mla_q_a_norm, mla_q_up, mla_kv_a_norm, mla_kv_up, mla_nope_norm, mla_rope_norm, w_branch, w_out, moe_router, moe_w_gate, moe_w_up, moe_w_down):
    bsz, seq, d = x.shape
    assert bsz == 1
    nu = ctx.shape[1]
    depth = w_in.shape[0]
    aw = A_HEADS * A_HEAD_DIM
    dr = rwkv_w_up.shape[2]
    gr = rwkv_g_up.shape[1]
    n_a = 3 * aw + 4 * dr + gr
    n_a_p = _round_up(n_a, LANES)
    n_b = (B_Q_HEADS + 2 * B_KV_HEADS) * B_HEAD_DIM
    cq = mla_q_up.shape[1]
    ckv = mla_kv_up.shape[1]
    n_c = cq + ckv + C_ROPE_DIM
    n_c_p = cq + ckv + LANES
    hw = 2 * LANES

    x_all = jnp.concatenate([ctx[0], x[0]], axis=0)
    cond8 = jnp.zeros((8, d), F32).at[0].set(c_ctx).at[1].set(c[0])
    cos, sin = _rope_tables(nu, seq)

    for i in range(depth):
        wi = w_in[i]
        w_pad = jnp.concatenate([_pad_cols(wi[:, :n_a], n_a_p), wi[:, n_a:n_a + n_b],
                                 _pad_cols(wi[:, n_a + n_b:n_a + n_b + n_c], n_c_p), wi[:, n_a + n_b + n_c:]],
                                axis=1).astype(BF16)
        mu_p = _pad_cols(rwkv_mu[i], n_a_p)
        g_up_p = jnp.pad(rwkv_g_up[i], ((0, n_a_p - n_a), (0, 0)))
        qu = mla_q_up[i].reshape(cq, C_HEADS, C_NOPE_DIM + C_ROPE_DIM)
        qu = jnp.pad(qu, ((0, 0), (0, 0), (0, hw - C_NOPE_DIM - C_ROPE_DIM))).reshape(cq, C_HEADS * hw).astype(BF16)
        kvu = mla_kv_up[i].reshape(ckv, C_HEADS, C_NOPE_DIM + C_V_DIM)
        kvu = jnp.concatenate([kvu[:, :, :C_NOPE_DIM].reshape(ckv, -1), kvu[:, :, C_NOPE_DIM:].reshape(ckv, -1)],
                              axis=1).astype(BF16)
        rope_norm_p = _pad_cols(mla_rope_norm[i], LANES)

        mod = _adaln(cond8, ada_down[i], ada_up[i], ada_bias[i])[:2].reshape(2, N_MOD, d)

        h = _norm_mod(x_all, norm_mix[i], mod[:, 0], mod[:, 1], nu)
        za = _matmul(h, w_pad, F32, n=n_a_p, name="w_in_a")
        zb = _matmul(h, w_pad, F32, w_col0=n_a_p, n=n_b, name="w_in_b")
        zc = _matmul(h, w_pad, F32, w_col0=n_a_p + n_b, n=n_c_p, name="w_in_c")
        gates = _matmul(h, w_pad, BF16, w_col0=n_a_p + n_b + n_c_p, n=N_BRANCH * d, act="sigmoid", name="w_in_gate")

        r, k, v, e, a, g = _rwkv_feat(za, mu_p, rwkv_w0[i], rwkv_w_up[i], rwkv_a0[i], rwkv_a_up[i], g_up_p, nu)
        ys = []
        for direction in (0, 1):
            p_, q_, r2, y0 = _rwkv_chunks(r, k, v, e, a, rwkv_k_k[i], rwkv_k_a[i], direction)
            ys.append(_rwkv_seq(p_, q_, r2, y0, direction, nu))
        ya = _rwkv_out(ys[0], ys[1], r, k, v, a, g, rwkv_lnx_w[i], rwkv_lnx_b[i], rwkv_r_k[i], rwkv_k_a[i])

        qkv_b = _gqa_prep(zb, cos, sin, gqa_q_norm[i], gqa_k_norm[i])
        yb = _gqa_attn(qkv_b, gqa_sink[i], nu, seq)

        assert cq % ckv == 0
        q_c = _matmul(zc, qu, F32, rms_gain=mla_q_a_norm[i], name="mla_q_up")
        kv_c = _matmul(zc, kvu, F32, a_col_block=cq // ckv, rms_gain=mla_kv_a_norm[i], name="mla_kv_up")
        qf, kf, vf = _mla_prep(q_c, kv_c, zc, (cq + ckv) // LANES, cos, sin, mla_nope_norm[i], rope_norm_p)
        yc = _mla_flash(qf, kf, vf, nu)

        merged = _merge(ya, yb, yc, w_branch[i].astype(BF16), gates)
        x_all = _matmul(merged, w_out[i].astype(BF16), F32, resid=x_all, gate2=mod[:, 2], nu=nu, name="w_out")

        hf, aff = _norm_mod(x_all, norm_ffn[i], mod[:, 3], mod[:, 4], nu, router_t=moe_router[i].T)
        wg = moe_w_gate[i].astype(BF16)
        wu = moe_w_up[i].astype(BF16)
        wd = moe_w_down[i].astype(BF16)
        last = i == depth - 1
        x_all = _moe_stream(x_all, hf, aff, nu, seq, mod[1:2, 5], wg, wu, wd, rows_only=last)
        if not last:
            x_all = _moe_stream(x_all, hf, aff, 0, nu, mod[0:1, 5], wg, wu, wd)
    return x_all.reshape(bsz, seq, d)
```

```python
import functools
import math

import jax
import jax.numpy as jnp
import numpy as np
from jax import lax
from jax.experimental import pallas as pl
from jax.experimental.pallas import tpu as pltpu

F32 = jnp.float32
BF16 = jnp.bfloat16
I32 = jnp.int32

GRID_W = 64
ROPE_DIM = 64
ROPE_THETA = 10000.0
NORM_EPS = 1e-6
NEG_INF = -1e30
N_MOD = 6
A_HEADS = 16
A_HEAD_DIM = 64
A_LNX_EPS = 64e-5
B_Q_HEADS = 16
B_KV_HEADS = 4
B_HEAD_DIM = 64
WINDOW = 128
BLOCK = 128
C_HEADS = 8
C_NOPE_DIM = 128
C_ROPE_DIM = 64
C_V_DIM = 128
N_BRANCH = 3
CAPACITY_FACTOR = 2

LANES = 128
CHUNK = 64
SEQ_GROUP = 4
ROW_TILE = 256
MOE_TILE = 128
MOE_ALIGN = 16
MOE_WIN = MOE_TILE + MOE_ALIGN
SEL_ROWS = 128
VMEM_MB = 56


def _cp(n_grid, vmem_mb=VMEM_MB):
    return pltpu.CompilerParams(dimension_semantics=("arbitrary",) * n_grid,
                                vmem_limit_bytes=vmem_mb * 1024 * 1024)


def _pick(n, cands):
    for c in cands:
        if n % c == 0:
            return c
    raise ValueError(f"no tile for {n} in {cands}")


def _dg(a, b, ca=1, cb=0):
    return lax.dot_general(a, b, (((ca,), (cb,)), ((), ())), preferred_element_type=F32)


def _split2(x):
    hi = x.astype(BF16)
    lo = (x - hi.astype(F32)).astype(BF16)
    return hi, lo


def _mm(a, b, passes=1, nt=False):
    cb = 1 if nt else 0
    if passes == 1:
        return _dg(a.astype(BF16), b.astype(BF16), 1, cb)
    ah, al = _split2(a)
    bh, bl = _split2(b)
    return _dg(ah, bh, 1, cb) + (_dg(ah, bl, 1, cb) + _dg(al, bh, 1, cb))


def _mm_exact_rhs(x, m_bf16):
    x1 = x.astype(BF16)
    r1 = x - x1.astype(F32)
    x2 = r1.astype(BF16)
    x3 = (r1 - x2.astype(F32)).astype(BF16)
    return _dg(x1, m_bf16) + (_dg(x2, m_bf16) + _dg(x3, m_bf16))


def _sigmoid(x):
    return 1.0 / (1.0 + jnp.exp(-x))


def _iota(shape, dim):
    return lax.broadcasted_iota(I32, shape, dim)


def _block_ones(n, blk):
    i = _iota((n, n), 0) // blk
    j = _iota((n, n), 1) // blk
    return jnp.where(i == j, 1.0, 0.0).astype(BF16)


def _seg_sum(x, ones_bf16):
    return _mm_exact_rhs(x, ones_bf16)


def _swap16(x):
    n = x.shape[-1]
    lane = _iota(x.shape, x.ndim - 1)
    fwd = pltpu.roll(x, n - 16, x.ndim - 1)
    bwd = pltpu.roll(x, 16, x.ndim - 1)
    return jnp.where((lane % 32) < 16, fwd, bwd)


def _adaln_kernel(c_ref, dn_ref, up_ref, b_ref, o_ref):
    c = c_ref[...]
    t = _mm(c * _sigmoid(c), dn_ref[...], 3)
    o_ref[...] = _mm(t, up_ref[...], 3) + b_ref[...]


def _adaln(cond8, down, up, bias):
    d, r = down.shape
    n = up.shape[1]
    tn = _pick(n, (4096, 2048, 1024, 512, 256, 128))
    return pl.pallas_call(
        _adaln_kernel,
        grid=(n // tn,),
        in_specs=[pl.BlockSpec((8, d), lambda j: (0, 0)),
                  pl.BlockSpec((d, r), lambda j: (0, 0)),
                  pl.BlockSpec((r, tn), lambda j: (0, j)),
                  pl.BlockSpec((1, tn), lambda j: (0, j))],
        out_specs=pl.BlockSpec((8, tn), lambda j: (0, j)),
        out_shape=jax.ShapeDtypeStruct((8, n), F32),
        compiler_params=_cp(1),
        name="adaln",
    )(cond8, down, up, bias.reshape(1, n))


def _moe_dcol(d):
    return min(d, 1024)


def _rows_are_ctx(tile_rows, row0, nu):
    return (row0 + _iota((tile_rows, 1), 0)) < nu


def _norm_mod_kernel(x_ref, g_ref, sh_ref, sc_ref, *rest, nu, tm, router):
    if router:
        wr_ref, h_ref, aff_ref = rest
    else:
        (h_ref,) = rest
    x = x_ref[...]
    is_u = pl.program_id(0) * tm < nu
    y = x * lax.rsqrt(jnp.mean(x * x, axis=-1, keepdims=True) + NORM_EPS) * g_ref[...]
    sh = jnp.where(is_u, sh_ref[0:1, :], sh_ref[1:2, :])
    sc = jnp.where(is_u, sc_ref[0:1, :], sc_ref[1:2, :])
    h = y * (1.0 + sc) + sh
    if router:
        dcol = h_ref.shape[2]
        for c in range(h_ref.shape[0]):
            h_ref[c] = h[:, c * dcol:(c + 1) * dcol].astype(h_ref.dtype)
    else:
        h_ref[...] = h.astype(h_ref.dtype)
    if router:
        logits = _mm(wr_ref[...], h, 3, nt=True)
        m = jnp.max(logits, axis=0, keepdims=True)
        p = jnp.exp(logits - m)
        aff_ref[...] = p / jnp.sum(p, axis=0, keepdims=True)


def _norm_mod(x_all, gain, shift2, scale2, nu, router_t=None):
    t, d = x_all.shape
    tm = ROW_TILE
    assert nu % tm == 0
    router = router_t is not None
    in_specs = [pl.BlockSpec((tm, d), lambda i: (i, 0)),
                pl.BlockSpec((1, d), lambda i: (0, 0)),
                pl.BlockSpec((2, d), lambda i: (0, 0)),
                pl.BlockSpec((2, d), lambda i: (0, 0))]
    args = [x_all, gain.reshape(1, d), shift2, scale2]
    out_specs = pl.BlockSpec((tm, d), lambda i: (i, 0))
    out_shape = jax.ShapeDtypeStruct((t, d), BF16)
    if router:
        e = router_t.shape[0]
        dcol = _moe_dcol(d)
        in_specs.append(pl.BlockSpec((e, d), lambda i: (0, 0)))
        args.append(router_t)
        out_specs = (pl.BlockSpec((d // dcol, tm, dcol), lambda i: (0, i, 0)), pl.BlockSpec((e, tm), lambda i: (0, i)))
        out_shape = (jax.ShapeDtypeStruct((d // dcol, t, dcol), BF16), jax.ShapeDtypeStruct((e, t), F32))
    return pl.pallas_call(
        functools.partial(_norm_mod_kernel, nu=nu, tm=tm, router=router),
        grid=(t // tm,), in_specs=in_specs, out_specs=out_specs, out_shape=out_shape,
        compiler_params=_cp(1), name="norm_mod_router" if router else "norm_mod",
    )(*args)


def _matmul_kernel(*refs, nu, tm, act, has_rms, has_resid):
    it = iter(refs)
    a_ref = next(it)
    w_ref = next(it)
    g_ref = next(it) if has_rms else None
    x_ref = next(it) if has_resid else None
    gate_ref = next(it) if has_resid else None
    o_ref = next(it)
    a = a_ref[...]
    if has_rms:
        af = a.astype(F32)
        a = af * lax.rsqrt(jnp.mean(af * af, axis=-1, keepdims=True) + NORM_EPS) * g_ref[...]
    acc = _dg(a.astype(BF16), w_ref[...])
    if act == "sigmoid":
        acc = _sigmoid(acc)
    if has_resid:
        is_u = _rows_are_ctx(tm, pl.program_id(1) * tm, nu)
        gate = jnp.where(is_u, gate_ref[0:1, :], gate_ref[1:2, :])
        acc = x_ref[...] + gate * acc
    o_ref[...] = acc.astype(o_ref.dtype)


def _matmul(a, w, out_dtype, *, a_col_block=0, w_row0=0, k=None, w_col0=0, n=None, act=None, rms_gain=None,
            resid=None, gate2=None, nu=0, name="matmul"):
    m = a.shape[0]
    k = w.shape[0] if k is None else k
    n = w.shape[1] if n is None else n
    tm = _pick(m, (640, 256, 128))
    tn = n if n <= 2048 else _pick(n, (1024, 768, 512, 384, 256, 128))
    has_rms = rms_gain is not None
    has_resid = resid is not None
    in_specs = [pl.BlockSpec((tm, k), lambda j, i: (i, a_col_block)),
                pl.BlockSpec((pl.Element(k), pl.Element(tn)),
                             lambda j, i: (w_row0, pl.multiple_of(w_col0 + j * tn, LANES)))]
    args = [a, w]
    if has_rms:
        in_specs.append(pl.BlockSpec((1, k), lambda j, i: (0, 0)))
        args.append(rms_gain.reshape(1, k))
    if has_resid:
        in_specs += [pl.BlockSpec((tm, tn), lambda j, i: (i, j)),
                     pl.BlockSpec((2, tn), lambda j, i: (0, j))]
        args += [resid, gate2]
    return pl.pallas_call(
        functools.partial(_matmul_kernel, nu=nu, tm=tm, act=act, has_rms=has_rms, has_resid=has_resid),
        grid=(n // tn, m // tm), in_specs=in_specs,
        out_specs=pl.BlockSpec((tm, tn), lambda j, i: (i, j)),
        out_shape=jax.ShapeDtypeStruct((m, n), out_dtype),
        compiler_params=_cp(2), name=name,
    )(*args)


def _rwkv_feat_kernel(z_ref, zp_ref, zn_ref, mu_ref, w0_ref, wup_ref, a0_ref, aup_ref, gup_ref,
                      r_ref, k_ref, v_ref, e_ref, a_ref, g_ref, *, nu, t_all, tm, aw, dr):
    z = z_ref[...]
    row = pl.program_id(0) * tm + _iota((tm, 1), 0)
    ri = _iota((tm, 1), 0)
    up1 = pltpu.roll(z, 1, 0)
    dn1 = pltpu.roll(z, tm - 1, 0)
    zp = jnp.where(ri == 0, zp_ref[7:8, :], up1)
    zn = jnp.where(ri == tm - 1, zn_ref[0:1, :], dn1)
    zp = jnp.where((row == 0) | (row == nu), 0.0, zp)
    zn = jnp.where((row == nu - 1) | (row == t_all - 1), 0.0, zn)
    zs = z + mu_ref[0:1, :] * (zp - z) + mu_ref[1:2, :] * (zn - z)
    r_ref[...] = zs[:, 0:aw]
    k_ref[...] = zs[:, aw:2 * aw]
    v_ref[...] = zs[:, 2 * aw:3 * aw]
    o = 3 * aw
    for n in range(2):
        wd = jnp.tanh(zs[:, o + n * dr:o + (n + 1) * dr])
        w = w0_ref[n:n + 1, :] + _mm(wd, wup_ref[n])
        sp = jnp.maximum(-w, 0.0) + jnp.log(1.0 + jnp.exp(-jnp.abs(w)))
        e_ref[n] = jnp.exp(-sp - 0.5)
    o += 2 * dr
    for n in range(2):
        ad = zs[:, o + n * dr:o + (n + 1) * dr]
        a_ref[n] = _sigmoid(a0_ref[n:n + 1, :] + _mm(ad, aup_ref[n]))
    o += 2 * dr
    g_ref[...] = _mm(_sigmoid(zs[:, o:]), gup_ref[...])


def _rwkv_feat(za, mu_p, w0, w_up, a0, a_up, g_up_p, nu):
    t, na = za.shape
    aw = w0.shape[1]
    dr = w_up.shape[1]
    tm = ROW_TILE
    nb8 = tm // 8
    last8 = t // 8 - 1
    row_spec = lambda w: pl.BlockSpec((tm, w), lambda i: (i, 0))
    full = lambda a: pl.BlockSpec(a.shape, lambda i: (0,) * a.ndim)
    out_rows = jax.ShapeDtypeStruct((t, aw), F32)
    out_dir = jax.ShapeDtypeStruct((2, t, aw), F32)
    dir_spec = pl.BlockSpec((2, tm, aw), lambda i: (0, i, 0))
    return pl.pallas_call(
        functools.partial(_rwkv_feat_kernel, nu=nu, t_all=t, tm=tm, aw=aw, dr=dr),
        grid=(t // tm,),
        in_specs=[row_spec(na),
                  pl.BlockSpec((8, na), lambda i: (jnp.maximum(i * nb8 - 1, 0), 0)),
                  pl.BlockSpec((8, na), lambda i: (jnp.minimum((i + 1) * nb8, last8), 0)),
                  full(mu_p), full(w0), full(w_up), full(a0), full(a_up), full(g_up_p)],
        out_specs=(row_spec(aw), row_spec(aw), row_spec(aw), dir_spec, dir_spec, row_spec(aw)),
        out_shape=(out_rows, out_rows, out_rows, out_dir, out_dir, out_rows),
        compiler_params=_cp(1), name="rwkv_feat",
    )(za, za, za, mu_p, w0, w_up, a0, a_up, g_up_p)


def _bmm(a, b, nt=False):
    cb = 2 if nt else 1
    return lax.dot_general(a.astype(BF16), b.astype(BF16), (((2,), (cb,)), ((0,), (0,))),
                           preferred_element_type=F32)


def _bmm_exact_lhs(m_bf16, x):
    dn = (((2,), (1,)), ((0,), (0,)))
    x1 = x.astype(BF16)
    r1 = x - x1.astype(F32)
    x2 = r1.astype(BF16)
    x3 = (r1 - x2.astype(F32)).astype(BF16)
    d = lambda y: lax.dot_general(m_bf16, y, dn, preferred_element_type=F32)
    return d(x1) + (d(x2) + d(x3))


def _bt(x):
    return jnp.stack([x[g].T for g in range(x.shape[0])], axis=0)


def _rwkv_chunk_math(r, k, v, e, a, kk_gain, ka_gain, reverse, groups):
    c = CHUNK
    hd = A_HEAD_DIM
    lane = _iota((1, LANES), 1)
    m_a = jnp.where(lane < hd, 1.0, 0.0)
    m_b = 1.0 - m_a
    ones_seg = _block_ones(LANES, hd)
    ti = _iota((groups, c, c), 1)
    tj = _iota((groups, c, c), 2)
    tri = jnp.where((tj >= ti) if reverse else (tj <= ti), 1.0, 0.0).astype(BF16)

    kk0 = k * kk_gain
    kk = kk0 / jnp.maximum(jnp.sqrt(_seg_sum(kk0 * kk0, ones_seg)), 1e-12)
    kt = k * (1.0 + (a - 1.0) * ka_gain)
    b = kk * a
    g3 = lambda x: x.reshape(groups, c, LANES)
    e3 = g3(e)
    cl = _bmm_exact_lhs(tri, e3)
    last = 0 if reverse else c - 1
    ctot = cl[:, last:last + 1, :]
    g_in = jnp.exp(-cl)
    g_ex = jnp.exp(e3 - cl)
    g_inv = jnp.exp(cl)
    g_end = jnp.exp(cl - ctot)
    st = lambda x: jnp.concatenate([x * m_a, x * m_b], axis=1)
    kk3, b3, kt3 = g3(kk), g3(b), g3(kt)
    kk2 = st(kk3 * g_ex)
    r2 = st(g3(r) * g_in)
    b2 = st(b3 * g_inv)
    k2 = st(kt3 * g_inv)
    v2 = st(g3(v))
    bg2 = st(b3 * g_end)
    kg2 = st(kt3 * g_end)

    s = _bmm(jnp.concatenate([kk2, r2], axis=1), jnp.concatenate([b2, k2], axis=1), nt=True)
    n2 = 2 * c
    i2 = _iota((n2, n2), 0)
    j2 = _iota((n2, n2), 1)
    il = i2 % c
    jl = j2 % c
    strict = (jl > il) if reverse else (jl < il)
    incl = (jl >= il) if reverse else (jl <= il)
    a_b = jnp.where(strict, s[:, :n2, :n2], 0.0)
    a_k = jnp.where(strict, s[:, :n2, n2:], 0.0)
    l_b = jnp.where(incl, s[:, n2:, :n2], 0.0)
    l_k = jnp.where(incl, s[:, n2:, n2:], 0.0)
    eye = jnp.where(i2 == j2, 1.0, 0.0)

    same = lambda m: (i2 // m) == (j2 // m)
    a_d = jnp.where(same(8), a_b, 0.0)
    a_d2 = _bmm(a_d, a_d)
    a_d4 = _bmm(a_d2, a_d2)
    t_inv = _bmm(_bmm(eye - a_d, eye + a_d2), eye + a_d4)
    m = 8
    while m < c:
        a_off = jnp.where(same(2 * m) & jnp.logical_not(same(m)), a_b, 0.0)
        t_inv = t_inv - _bmm(_bmm(t_inv, a_off), t_inv)
        m *= 2

    av = _bmm(a_k, v2)
    z12 = _bmm(t_inv, jnp.concatenate([kk2, av], axis=2))
    z1 = z12[:, :, :LANES]
    z2 = z12[:, :, LANES:]
    vz = jnp.concatenate([v2, z2], axis=1)
    p_bd = eye * jnp.exp(-ctot) - _bmm(_bt(bg2), z1)
    q_bd = _bmm(_bt(jnp.concatenate([kg2, -bg2], axis=1)), vz)
    r2s = r2 - _bmm(l_b, z1)
    y0s = _bmm(jnp.concatenate([l_k, -l_b], axis=2), vz)
    cp = lambda x: x[:, :x.shape[1] // 2] + x[:, x.shape[1] // 2:]
    return cp(p_bd), cp(q_bd), cp(r2s).reshape(groups * c, LANES), cp(y0s).reshape(groups * c, LANES)


def _rwkv_chunk_kernel(r_ref, k_ref, v_ref, e_ref, a_ref, kkg_ref, kag_ref,
                       p_ref, q_ref, r2_ref, y0_ref, *, reverse, groups):
    p, q, r2, y0 = _rwkv_chunk_math(r_ref[...], k_ref[...], v_ref[...], e_ref[0], a_ref[0],
                                    kkg_ref[...], kag_ref[...], reverse, groups)
    p_ref[0] = p
    q_ref[0] = q
    r2_ref[...] = r2
    y0_ref[...] = y0


def _rwkv_chunks(r, k, v, e, a, k_k, k_a, direction):
    t, aw = r.shape
    npair = aw // LANES
    nc = t // CHUNK
    groups = _pick(nc, (20, 10, 4))
    rows = groups * CHUNK
    row_spec = pl.BlockSpec((rows, LANES), lambda p, i: (i, p))
    dir_spec = pl.BlockSpec((1, rows, LANES), lambda p, i: (direction, i, p))
    par_spec = pl.BlockSpec((1, LANES), lambda p, i: (0, p))
    pq_spec = pl.BlockSpec((1, groups, A_HEAD_DIM, LANES), lambda p, i: (p, i, 0, 0))
    pq_shape = jax.ShapeDtypeStruct((npair, nc, A_HEAD_DIM, LANES), F32)
    ry_shape = jax.ShapeDtypeStruct((t, aw), F32)
    return pl.pallas_call(
        functools.partial(_rwkv_chunk_kernel, reverse=bool(direction), groups=groups),
        grid=(npair, t // rows),
        in_specs=[row_spec, row_spec, row_spec, dir_spec, dir_spec, par_spec, par_spec],
        out_specs=(pq_spec, pq_spec, row_spec, row_spec),
        out_shape=(pq_shape, pq_shape, ry_shape, ry_shape),
        compiler_params=_cp(2), name="rwkv_chunks_bwd" if direction else "rwkv_chunks_fwd",
    )(r, k, v, e, a, k_k.reshape(1, aw), k_a.reshape(1, aw))


def _pair_block_diag(x):
    lane = _iota(x.shape, 1)
    return jnp.concatenate([jnp.where(lane < A_HEAD_DIM, x, 0.0), jnp.where(lane >= A_HEAD_DIM, x, 0.0)], axis=0)


def _rwkv_seq_kernel(p_ref, q_ref, r2_ref, y0_ref, y_ref, h_ref, *, reverse, groups, npair):
    c = CHUNK

    @pl.when(pl.program_id(0) == 0)
    def _():
        h_ref[...] = jnp.zeros_like(h_ref)

    order = range(groups - 1, -1, -1) if reverse else range(groups)
    for g in order:
        sl = slice(g * c, (g + 1) * c)
        for p in range(npair):
            ls = slice(p * LANES, (p + 1) * LANES)
            h = h_ref[p]
            y_ref[sl, ls] = _mm(r2_ref[sl, ls], h) + y0_ref[sl, ls]
            h_ref[p] = _mm(_pair_block_diag(p_ref[p, g]), h) + _pair_block_diag(q_ref[p, g])


def _rwkv_seq(p, q, r2, y0, direction, nu):
    npair, nc = p.shape[:2]
    t, aw = r2.shape
    groups = SEQ_GROUP
    rows = groups * CHUNK
    nb = t // rows
    nbu = nu // rows
    assert nu % rows == 0
    if direction:
        blk = lambda s: jnp.where(s < nbu, nbu - 1 - s, nb + nbu - 1 - s)
    else:
        blk = lambda s: s
    pq_spec = pl.BlockSpec((npair, groups, A_HEAD_DIM, LANES), lambda s: (0, blk(s), 0, 0))
    row_spec = pl.BlockSpec((rows, aw), lambda s: (blk(s), 0))
    return pl.pallas_call(
        functools.partial(_rwkv_seq_kernel, reverse=bool(direction), groups=groups, npair=npair),
        grid=(nb,),
        in_specs=[pq_spec, pq_spec, row_spec, row_spec],
        out_specs=row_spec,
        out_shape=jax.ShapeDtypeStruct((t, aw), F32),
        scratch_shapes=[pltpu.VMEM((npair, LANES, LANES), F32)],
        compiler_params=_cp(1), name="rwkv_seq_bwd" if direction else "rwkv_seq_fwd",
    )(p, q, r2, y0)


def _rwkv_out_kernel(yf_ref, yb_ref, r_ref, k_ref, v_ref, a_ref, g_ref, lw_ref, lb_ref, rk_ref, ka_ref, o_ref, *, aw):
    hd = A_HEAD_DIM
    ones_seg = _block_ones(LANES, hd)
    for p in range(aw // LANES):
        ls = slice(p * LANES, (p + 1) * LANES)
        y = yf_ref[:, ls] + yb_ref[:, ls]
        mean = _seg_sum(y, ones_seg) * (1.0 / hd)
        yc = y - mean
        var = _seg_sum(yc * yc, ones_seg) * (1.0 / hd)
        yn = yc * lax.rsqrt(var + A_LNX_EPS) * lw_ref[:, ls] + lb_ref[:, ls]
        r = r_ref[:, ls]
        k = k_ref[:, ls]
        v = v_ref[:, ls]
        bonus = jnp.zeros_like(y)
        for n in range(2):
            kt = k * (1.0 + (a_ref[n, :, ls] - 1.0) * ka_ref[:, ls])
            bonus = bonus + _seg_sum(r * kt * rk_ref[:, ls], ones_seg) * v
        o_ref[:, ls] = ((yn + bonus) * g_ref[:, ls]).astype(o_ref.dtype)


def _rwkv_out(yf, yb, r, k, v, a, g, lnx_w, lnx_b, r_k, k_a):
    t, aw = r.shape
    tm = ROW_TILE
    row_spec = pl.BlockSpec((tm, aw), lambda i: (i, 0))
    par_spec = pl.BlockSpec((1, aw), lambda i: (0, 0))
    return pl.pallas_call(
        functools.partial(_rwkv_out_kernel, aw=aw),
        grid=(t // tm,),
        in_specs=[row_spec] * 5 + [pl.BlockSpec((2, tm, aw), lambda i: (0, i, 0)), row_spec] + [par_spec] * 4,
        out_specs=row_spec,
        out_shape=jax.ShapeDtypeStruct((t, aw), BF16),
        compiler_params=_cp(1), name="rwkv_out",
    )(yf, yb, r, k, v, a, g, lnx_w.reshape(1, aw), lnx_b.reshape(1, aw), r_k.reshape(1, aw), k_a.reshape(1, aw))


def _rms_rope_slab(x, gain, cos, sin, ones_seg):
    ms = _seg_sum(x * x, ones_seg) * (1.0 / B_HEAD_DIM)
    y = x * lax.rsqrt(ms + NORM_EPS) * gain
    return y * cos + _swap16(y) * sin


def _gqa_prep_kernel(z_ref, cos_ref, sin_ref, qg_ref, kg_ref, o_ref, *, qw, kw):
    ones_seg = _block_ones(LANES, B_HEAD_DIM)
    cos = cos_ref[...]
    sin = sin_ref[...]
    scale = B_HEAD_DIM ** -0.5 * math.log2(math.e)
    for s in range(qw // LANES):
        ls = slice(s * LANES, (s + 1) * LANES)
        o_ref[:, ls] = (_rms_rope_slab(z_ref[:, ls], qg_ref[...], cos, sin, ones_seg) * scale).astype(o_ref.dtype)
    for s in range(kw // LANES):
        ls = slice(qw + s * LANES, qw + (s + 1) * LANES)
        o_ref[:, ls] = _rms_rope_slab(z_ref[:, ls], kg_ref[...], cos, sin, ones_seg).astype(o_ref.dtype)
    o_ref[:, qw + kw:] = z_ref[:, qw + kw:].astype(o_ref.dtype)


def _gqa_prep(zb, cos, sin, q_norm, k_norm):
    t, nb = zb.shape
    tm = ROW_TILE
    qw = B_Q_HEADS * B_HEAD_DIM
    kw = B_KV_HEADS * B_HEAD_DIM
    tile2 = lambda g: jnp.tile(g.reshape(1, B_HEAD_DIM), (1, LANES // B_HEAD_DIM))
    return pl.pallas_call(
        functools.partial(_gqa_prep_kernel, qw=qw, kw=kw),
        grid=(t // tm,),
        in_specs=[pl.BlockSpec((tm, nb), lambda i: (i, 0)),
                  pl.BlockSpec((tm, LANES), lambda i: (i, 0)),
                  pl.BlockSpec((tm, LANES), lambda i: (i, 0)),
                  pl.BlockSpec((1, LANES), lambda i: (0, 0)),
                  pl.BlockSpec((1, LANES), lambda i: (0, 0))],
        out_specs=pl.BlockSpec((tm, nb), lambda i: (i, 0)),
        out_shape=jax.ShapeDtypeStruct((t, nb), BF16),
        compiler_params=_cp(1), name="gqa_prep",
    )(zb, cos, sin, tile2(q_norm), tile2(k_norm))


def _gqa_attn_kernel(sink_ref, q_ref, c_ref, kp_ref, ko_ref, kn_ref, o_ref, *, nu, seq, qw, kw):
    hd = B_HEAD_DIM
    grp = B_Q_HEADS // B_KV_HEADS
    blk = BLOCK
    j = pl.program_id(0)
    jb = j - nu // blk
    nkeys = nu + 3 * blk
    rows = grp * blk
    qi = _iota((rows, nkeys), 0) % blk
    kc = _iota((rows, nkeys), 1)
    q_pos = jb * blk + qi
    k_pos = (jb - 1) * blk + (kc - nu)
    band_ok = (jnp.abs(k_pos - q_pos) <= WINDOW) & (k_pos >= 0) & (k_pos < seq) & (jb >= 0)
    valid = (kc < nu) | band_ok
    rg = _iota((rows, 1), 0) // blk
    outs = []
    for h in range(B_KV_HEADS):
        ks = slice(qw + h * hd, qw + (h + 1) * hd)
        vs = slice(qw + kw + h * hd, qw + kw + (h + 1) * hd)
        k_all = jnp.concatenate([c_ref[:, ks], kp_ref[:, ks], ko_ref[:, ks], kn_ref[:, ks]], axis=0)
        v_all = jnp.concatenate([c_ref[:, vs], kp_ref[:, vs], ko_ref[:, vs], kn_ref[:, vs]], axis=0)
        q4 = jnp.concatenate([q_ref[:, (h * grp + g) * hd:(h * grp + g + 1) * hd] for g in range(grp)], axis=0)
        s = jnp.where(valid, _dg(q4, k_all, 1, 1), NEG_INF)
        sink = jnp.zeros((rows, 1), F32)
        for g in range(grp):
            sink = jnp.where(rg == g, sink_ref[h * grp + g] * math.log2(math.e), sink)
        m = jnp.maximum(jnp.max(s, axis=1, keepdims=True), sink)
        p = jnp.exp2((s - m).astype(BF16))
        pv = _dg(p, jnp.concatenate([v_all, jnp.ones_like(v_all)], axis=1))
        o = pv[:, :hd] / (pv[:, hd:hd + 1] + jnp.exp2(sink - m))
        outs += [o[g * blk:(g + 1) * blk, :] for g in range(grp)]
    o_ref[...] = jnp.concatenate(outs, axis=1).astype(o_ref.dtype)


def _gqa_attn(qkv, sink, nu, seq):
    t, nb = qkv.shape
    qw = B_Q_HEADS * B_HEAD_DIM
    kw = B_KV_HEADS * B_HEAD_DIM
    blk = BLOCK
    nblk = t // blk
    band = lambda off: pl.BlockSpec((blk, nb), lambda j: (jnp.clip(j + off, 0, nblk - 1), 0))
    return pl.pallas_call(
        functools.partial(_gqa_attn_kernel, nu=nu, seq=seq, qw=qw, kw=kw),
        grid=(nblk,),
        in_specs=[pl.BlockSpec(memory_space=pltpu.SMEM),
                  pl.BlockSpec((blk, nb), lambda j: (j, 0)),
                  pl.BlockSpec((nu, nb), lambda j: (0, 0)),
                  band(-1), band(0), band(1)],
        out_specs=pl.BlockSpec((blk, qw), lambda j: (j, 0)),
        out_shape=jax.ShapeDtypeStruct((t, qw), BF16),
        compiler_params=_cp(1), name="gqa_attn",
    )(sink, qkv, qkv, qkv, qkv, qkv)


def _mla_prep_kernel(q_ref, kv_ref, kr_ref, cos_ref, sin_ref, nn_ref, rn_ref, qo_ref, ko_ref, vo_ref):
    cos = cos_ref[...]
    sin = sin_ref[...]
    dn = C_NOPE_DIM
    hw = 2 * LANES
    scale = (C_NOPE_DIM + C_ROPE_DIM) ** -0.5 * math.log2(math.e)
    ones = jnp.ones((q_ref.shape[0], C_V_DIM), vo_ref.dtype)

    def rms_rope(x, gain):
        ms = jnp.sum(x * x, axis=-1, keepdims=True) * (1.0 / C_ROPE_DIM)
        y = x * lax.rsqrt(ms + NORM_EPS) * gain
        return y * cos + _swap16(y) * sin

    def rms(x, gain):
        return x * lax.rsqrt(jnp.mean(x * x, axis=-1, keepdims=True) + NORM_EPS) * gain

    kr = rms_rope(kr_ref[...], rn_ref[1:2, :]).astype(ko_ref.dtype)
    for h in range(C_HEADS):
        qn = rms(q_ref[:, h * hw:h * hw + dn], nn_ref[0:1, :])
        qr = rms_rope(q_ref[:, h * hw + dn:(h + 1) * hw], rn_ref[0:1, :])
        qo_ref[:, h * hw:h * hw + dn] = (qn * scale).astype(qo_ref.dtype)
        qo_ref[:, h * hw + dn:(h + 1) * hw] = (qr * scale).astype(qo_ref.dtype)
        ko_ref[:, h * hw:h * hw + dn] = rms(kv_ref[:, h * dn:(h + 1) * dn], nn_ref[1:2, :]).astype(ko_ref.dtype)
        ko_ref[:, h * hw + dn:(h + 1) * hw] = kr
        vo_ref[:, h * hw:h * hw + C_V_DIM] = kv_ref[:, (C_HEADS + h) * dn:(C_HEADS + h + 1) * dn].astype(vo_ref.dtype)
        vo_ref[:, h * hw + C_V_DIM:(h + 1) * hw] = ones


def _mla_prep(q, kv, zc, kr_col_block, cos, sin, nope_norm, rope_norm_p):
    t = q.shape[0]
    tm = ROW_TILE
    hw = 2 * LANES
    row = lambda w: pl.BlockSpec((tm, w), lambda i: (i, 0))
    full = lambda a: pl.BlockSpec(a.shape, lambda i: (0, 0))
    return pl.pallas_call(
        _mla_prep_kernel,
        grid=(t // tm,),
        in_specs=[row(q.shape[1]), row(kv.shape[1]),
                  pl.BlockSpec((tm, LANES), lambda i: (i, kr_col_block)),
                  row(LANES), row(LANES), full(nope_norm), full(rope_norm_p)],
        out_specs=(row(C_HEADS * hw),) * 3,
        out_shape=(jax.ShapeDtypeStruct((t, C_HEADS * hw), BF16),) * 3,
        compiler_params=_cp(1), name="mla_prep",
    )(q, kv, zc, cos, sin, nope_norm, rope_norm_p)


def _mla_flash_kernel(q_ref, k_ref, v_ref, o_ref, m_ref, acc_ref, *, nu, tq, tk, sub):
    qi = pl.program_id(1)
    ki = pl.program_id(2)

    @pl.when(ki == 0)
    def _():
        m_ref[...] = jnp.full_like(m_ref, NEG_INF)
        acc_ref[...] = jnp.zeros_like(acc_ref)

    def update(masked):
        for qs in range(tq // sub):
            rs = slice(qs * sub, (qs + 1) * sub)
            s = _dg(q_ref[rs, :], k_ref[...], 1, 1)
            if masked and qs * sub < nu:
                qrow = qi * tq + qs * sub + _iota((sub, tk), 0)
                kcol = ki * tk + _iota((sub, tk), 1)
                s = jnp.where((qrow < nu) & (kcol >= nu), NEG_INF, s)
            m_prev = m_ref[rs, :]
            m_next = jnp.maximum(m_prev, jnp.max(s, axis=1, keepdims=True))
            alpha = jnp.exp2(m_prev - m_next)
            p = jnp.exp2((s - m_next[:, 0:1]).astype(BF16))
            acc_ref[rs, :] = jnp.concatenate([alpha, alpha], axis=1) * acc_ref[rs, :] + _dg(p, v_ref[...])
            m_ref[rs, :] = m_next

    has_ctx_rows = qi * tq < nu

    @pl.when(has_ctx_rows)
    def _():
        update(True)

    @pl.when(jnp.logical_not(has_ctx_rows))
    def _():
        update(False)

    @pl.when(ki == pl.num_programs(2) - 1)
    def _():
        o_ref[...] = (acc_ref[:, :C_V_DIM] / acc_ref[:, C_V_DIM:]).astype(o_ref.dtype)


def _mla_flash(qf, kf, vf, nu):
    t = qf.shape[0]
    hw = 2 * LANES
    tq = _pick(t, (3328, 256))
    tk = _pick(t, (3328, 256))
    return pl.pallas_call(
        functools.partial(_mla_flash_kernel, nu=nu, tq=tq, tk=tk, sub=256),
        grid=(C_HEADS, t // tq, t // tk),
        in_specs=[pl.BlockSpec((tq, hw), lambda h, i, j: (i, h)),
                  pl.BlockSpec((tk, hw), lambda h, i, j: (j, h)),
                  pl.BlockSpec((tk, hw), lambda h, i, j: (j, h))],
        out_specs=pl.BlockSpec((tq, C_V_DIM), lambda h, i, j: (i, h)),
        out_shape=jax.ShapeDtypeStruct((t, C_HEADS * C_V_DIM), BF16),
        scratch_shapes=[pltpu.VMEM((tq, LANES), F32), pltpu.VMEM((tq, hw), F32)],
        compiler_params=_cp(3), name="mla_flash",
    )(qf, kf, vf)


def _merge_kernel(ya_ref, yb_ref, yc_ref, w_ref, g0_ref, g1_ref, g2_ref, o_ref):
    acc = g0_ref[...].astype(F32) * _dg(ya_ref[...], w_ref[0])
    acc = acc + g1_ref[...].astype(F32) * _dg(yb_ref[...], w_ref[1])
    acc = acc + g2_ref[...].astype(F32) * _dg(yc_ref[...], w_ref[2])
    o_ref[...] = acc.astype(o_ref.dtype)


def _merge(ya, yb, yc, w_branch, layer, gates):
    t, bw = ya.shape
    d = w_branch.shape[2]
    tm = _pick(t, (640, 256, 128))
    tn = _pick(d, (1024, 512, 256, 128))
    nj = d // tn
    y_spec = pl.BlockSpec((tm, bw), lambda j, i: (i, 0))
    gate = lambda b: pl.BlockSpec((tm, tn), lambda j, i: (i, b * nj + j))
    return pl.pallas_call(
        _merge_kernel,
        grid=(nj, t // tm),
        in_specs=[y_spec, y_spec, y_spec, pl.BlockSpec((N_BRANCH, bw, tn), lambda j, i: (layer, 0, j)),
                  gate(0), gate(1), gate(2)],
        out_specs=pl.BlockSpec((tm, tn), lambda j, i: (i, j)),
        out_shape=jax.ShapeDtypeStruct((t, d), BF16),
        compiler_params=_cp(2), name="merge",
    )(ya, yb, yc, w_branch, gates, gates, gates)


def _moe_select_kernel(aff_ref, loc_ref, base_ref, *, cap, n_exp):
    aff = aff_ref[...]
    bits = pltpu.bitcast(aff, I32)
    count = lambda m: jnp.sum(jnp.sum(jnp.where(m, 1.0, 0.0), axis=1, keepdims=True), axis=2, keepdims=True)
    theta = jnp.zeros((n_exp, 1, 1), I32)
    for bit in range(30, -1, -1):
        cand = theta | (1 << bit)
        theta = jnp.where(count(bits >= cand) >= cap, cand, theta)
    gt = bits > theta
    eq = bits == theta
    need = cap - count(gt)
    r = aff.shape[1]
    upper = jnp.where(_iota((LANES, LANES), 0) <= _iota((LANES, LANES), 1), 1.0, 0.0).astype(BF16)
    lower_strict = jnp.where(_iota((r, r), 1) < _iota((r, r), 0), 1.0, 0.0).astype(BF16)

    def prefix(x):
        incl = _dg(x.astype(BF16), upper)
        tot = jnp.broadcast_to(incl[:, LANES - 1:LANES], (r, LANES))
        return incl, _dg(lower_strict, tot.astype(BF16))

    for e in range(n_exp):
        xe = jnp.where(eq[e], 1.0, 0.0)
        incl, base = prefix(xe)
        take = eq[e] & ((incl - xe + base) < need[e])
        sel = jnp.where(gt[e] | take, 1.0, 0.0)
        incl, base = prefix(sel)
        loc_ref[e] = jnp.where(sel > 0.0, incl - sel, -4096.0)
        base_ref[e] = base


def _moe_select(aff3, cap):
    e, r, _ = aff3.shape
    spec = pl.BlockSpec((e, r, LANES), lambda i: (0, 0, 0))
    shape = jax.ShapeDtypeStruct((e, r, LANES), F32)
    return pl.pallas_call(
        functools.partial(_moe_select_kernel, cap=cap, n_exp=e),
        grid=(1,), in_specs=[spec], out_specs=(spec, spec), out_shape=(shape, shape),
        compiler_params=_cp(1), name="moe_select",
    )(aff3)


def _window_start(base_ref, e, i):
    b = base_ref[e, i]
    a = (b // MOE_ALIGN) * MOE_ALIGN
    return pl.multiple_of(a, MOE_ALIGN), b - a


def _moe_gather_kernel(base_ref, loc_ref, aff_ref, h_ref, xe_ref, ge_ref, *, tiles, group):
    g = pl.program_id(0)
    i = pl.program_id(2)

    @pl.when(i == 0)
    def _():
        xe_ref[...] = jnp.zeros_like(xe_ref)
        ge_ref[...] = jnp.zeros_like(ge_ref)

    win_row = _iota((MOE_WIN, MOE_TILE), 0).astype(F32)
    for k in range(group):
        e = g * group + k
        for r in range(tiles):
            a, off = _window_start(base_ref, e, i * tiles + r)
            slot = loc_ref[k, r] + off.astype(F32)
            onehot = jnp.where(win_row == slot, 1.0, 0.0)
            rows = _dg(onehot.astype(BF16), h_ref[r * MOE_TILE:(r + 1) * MOE_TILE, :])
            head = pl.ds(a, MOE_ALIGN)
            xe_ref[k, head, :] = (xe_ref[k, head, :].astype(F32) + rows[:MOE_ALIGN]).astype(xe_ref.dtype)
            xe_ref[k, pl.ds(a + MOE_ALIGN, MOE_TILE), :] = rows[MOE_ALIGN:].astype(xe_ref.dtype)
            win = pl.ds(a, MOE_WIN)
            gsel = jnp.sum(onehot * aff_ref[k, r], axis=1, keepdims=True)
            ge_ref[k, win, :] = ge_ref[k, win, :] + gsel


def _moe_gather(base_i, loc, aff3, h_chunks, t, row0, n, cap_p):
    e = loc.shape[0]
    nchunk, _, dcol = h_chunks.shape
    d = nchunk * dcol
    nt = n // MOE_TILE
    tiles = _pick(nt, (4, 2, 1))
    rows = tiles * MOE_TILE
    group = _pick(e, (4, 2, 1))
    sel_spec = pl.BlockSpec((group, tiles, 1, LANES), lambda g, c, i, b: (g, i, 0, 0))
    grid_spec = pltpu.PrefetchScalarGridSpec(
        num_scalar_prefetch=1, grid=(e // group, d // dcol, nt // tiles),
        in_specs=[sel_spec, sel_spec,
                  pl.BlockSpec((pl.Element(rows), pl.Element(dcol)),
                               lambda g, c, i, b: (pl.multiple_of(c * t + row0 + i * rows, MOE_TILE), 0))],
        out_specs=(pl.BlockSpec((group, cap_p, dcol), lambda g, c, i, b: (g, 0, c)),
                   pl.BlockSpec((group, cap_p, LANES), lambda g, c, i, b: (g, 0, 0))))
    return pl.pallas_call(
        functools.partial(_moe_gather_kernel, tiles=tiles, group=group), grid_spec=grid_spec,
        out_shape=(jax.ShapeDtypeStruct((e, cap_p, d), BF16), jax.ShapeDtypeStruct((e, cap_p, LANES), F32)),
        compiler_params=_cp(3), name="moe_gather",
    )(base_i, loc, aff3, h_chunks.reshape(nchunk * t, dcol))


def _moe_ffn_kernel(x_ref, g_ref, wg_ref, wu_ref, wd_ref, y_ref):
    x = x_ref[0]
    hg = _dg(x, wg_ref[0])
    hid = hg * _sigmoid(hg) * _dg(x, wu_ref[0])
    y = _dg(hid.astype(BF16), wd_ref[0]) * g_ref[0][:, 0:1]
    y_ref[0] = y.astype(y_ref.dtype)


def _moe_ffn(xe, ge, w_gate, w_up, w_down, layer, tc):
    e, cap_p, d = xe.shape
    f = w_gate.shape[2]
    return pl.pallas_call(
        _moe_ffn_kernel,
        grid=(e, cap_p // tc),
        in_specs=[pl.BlockSpec((1, tc, d), lambda ee, i: (ee, i, 0)),
                  pl.BlockSpec((1, tc, LANES), lambda ee, i: (ee, i, 0)),
                  pl.BlockSpec((1, d, f), lambda ee, i: (layer * e + ee, 0, 0)),
                  pl.BlockSpec((1, d, f), lambda ee, i: (layer * e + ee, 0, 0)),
                  pl.BlockSpec((1, f, d), lambda ee, i: (layer * e + ee, 0, 0))],
        out_specs=pl.BlockSpec((1, tc, d), lambda ee, i: (ee, i, 0)),
        out_shape=jax.ShapeDtypeStruct((e, cap_p, d), BF16),
        compiler_params=_cp(2), name="moe_ffn",
    )(xe, ge, w_gate, w_up, w_down)


def _moe_combine_kernel(base_ref, loc_ref, *rest, n_exp):
    y_refs = rest[:n_exp]
    x_ref, gate_ref, o_ref = rest[n_exp:]
    i = pl.program_id(0)
    win_row = _iota((MOE_TILE, MOE_WIN), 1).astype(F32)
    acc = None
    for e in range(n_exp):
        _, off = _window_start(base_ref, e, i)
        slot_row = jnp.broadcast_to(loc_ref[e, 0] + off.astype(F32), (MOE_TILE, LANES))
        slot_col = slot_row.T
        slot_col = jnp.concatenate([slot_col, slot_col[:, :MOE_WIN - LANES]], axis=1)
        onehot = jnp.where(win_row == slot_col, 1.0, 0.0).astype(BF16)
        part = _dg(onehot, y_refs[e][...])
        acc = part if acc is None else acc + part
    o_ref[...] = x_ref[...] + gate_ref[...] * acc


def _moe_combine(base_i, loc, ye, x_all, gate_row, row0, n, rows_only):
    e, cap_p, d = ye.shape
    nt = n // MOE_TILE
    t0 = row0 // MOE_TILE
    o0 = 0 if rows_only else t0
    dcol = d

    def y_spec(ee):
        def index(i, j, b):
            start = ee * cap_p + (b[ee, i] // MOE_ALIGN) * MOE_ALIGN
            return pl.multiple_of(start, MOE_ALIGN), pl.multiple_of(j * dcol, LANES)
        return pl.BlockSpec((pl.Element(MOE_WIN), pl.Element(dcol)), index)

    grid_spec = pltpu.PrefetchScalarGridSpec(
        num_scalar_prefetch=1, grid=(nt, d // dcol),
        in_specs=[pl.BlockSpec((e, 1, 1, LANES), lambda i, j, b: (0, i, 0, 0))]
                 + [y_spec(ee) for ee in range(e)]
                 + [pl.BlockSpec((MOE_TILE, dcol), lambda i, j, b: (t0 + i, j)),
                    pl.BlockSpec((1, dcol), lambda i, j, b: (0, j))],
        out_specs=pl.BlockSpec((MOE_TILE, dcol), lambda i, j, b: (o0 + i, j)))
    x_index = 2 + e
    return pl.pallas_call(
        functools.partial(_moe_combine_kernel, n_exp=e), grid_spec=grid_spec,
        out_shape=jax.ShapeDtypeStruct((n, d) if rows_only else x_all.shape, F32),
        input_output_aliases={} if rows_only else {x_index: 0},
        compiler_params=_cp(2), name="moe_combine",
    )(base_i, loc, *([ye.reshape(e * cap_p, d)] * e), x_all, gate_row)


def _moe_stream(x_all, h_chunks, aff, row0, n, gate_row, w_gate, w_up, w_down, layer, rows_only=False):
    e = aff.shape[0]
    cap = (CAPACITY_FACTOR * n) // e
    tc = 256 if cap >= 256 else 64
    cap_p = -(-(cap + MOE_WIN) // tc) * tc
    n_sel = SEL_ROWS * LANES
    aff_s = lax.dynamic_slice_in_dim(aff, row0, n, axis=1)
    aff3 = jnp.pad(aff_s, ((0, 0), (0, n_sel - n))).reshape(e, SEL_ROWS, LANES)
    loc, base = _moe_select(aff3, cap)
    base_i = base[:, :, 0].astype(I32)
    loc4 = loc.reshape(e, SEL_ROWS, 1, LANES)
    aff4 = aff3.reshape(e, SEL_ROWS, 1, LANES)
    xe, ge = _moe_gather(base_i, loc4, aff4, h_chunks, x_all.shape[0], row0, n, cap_p)
    ye = _moe_ffn(xe, ge, w_gate, w_up, w_down, layer, tc)
    return _moe_combine(base_i, loc4, ye, x_all, gate_row, row0, n, rows_only)


def _rope_tables(nu, seq):
    n_freq = ROPE_DIM // 4
    pos = jnp.arange(seq)
    inv = jnp.power(ROPE_THETA, -jnp.arange(n_freq, dtype=F32) / n_freq)
    ang_r = (pos // GRID_W).astype(F32)[:, None] * inv[None]
    ang_c = (pos % GRID_W).astype(F32)[:, None] * inv[None]
    cos = jnp.concatenate([jnp.cos(ang_r)] * 2 + [jnp.cos(ang_c)] * 2, axis=1)
    sin = jnp.concatenate([-jnp.sin(ang_r), jnp.sin(ang_r), -jnp.sin(ang_c), jnp.sin(ang_c)], axis=1)
    cos = jnp.concatenate([jnp.ones((nu, ROPE_DIM), F32), cos], axis=0)
    sin = jnp.concatenate([jnp.zeros((nu, ROPE_DIM), F32), sin], axis=0)
    rep = LANES // ROPE_DIM
    return jnp.tile(cos, (1, rep)), jnp.tile(sin, (1, rep))


def _pad_cols(w, n):
    return jnp.pad(w, ((0, 0), (0, n - w.shape[1])))


def _round_up(x, m):
    return -(-x // m) * m


def kernel(x, c, ctx, c_ctx, norm_mix, norm_ffn, ada_down, ada_up, ada_bias, w_in, rwkv_mu, rwkv_w0, rwkv_w_up, rwkv_a0, rwkv_a_up, rwkv_g_up, rwkv_k_k, rwkv_k_a, rwkv_r_k, rwkv_lnx_w, rwkv_lnx_b, gqa_q_norm, gqa_k_norm, gqa_sink, mla_q_a_norm, mla_q_up, mla_kv_a_norm, mla_kv_up, mla_nope_norm, mla_rope_norm, w_branch, w_out, moe_router, moe_w_gate, moe_w_up, moe_w_down):
    bsz, seq, d = x.shape
    assert bsz == 1
    nu = ctx.shape[1]
    depth = w_in.shape[0]
    aw = A_HEADS * A_HEAD_DIM
    dr = rwkv_w_up.shape[2]
    gr = rwkv_g_up.shape[1]
    n_a = 3 * aw + 4 * dr + gr
    n_a_p = _round_up(n_a, LANES)
    n_b = (B_Q_HEADS + 2 * B_KV_HEADS) * B_HEAD_DIM
    cq = mla_q_up.shape[1]
    ckv = mla_kv_up.shape[1]
    n_c = cq + ckv + C_ROPE_DIM
    n_c_p = cq + ckv + LANES
    hw = 2 * LANES

    x_all = jnp.concatenate([ctx[0], x[0]], axis=0)
    cond8 = jnp.zeros((8, d), F32).at[0].set(c_ctx).at[1].set(c[0])
    cos, sin = _rope_tables(nu, seq)

    zcols = lambda w_: jnp.zeros((depth, d, w_), BF16)
    wb = w_in.astype(BF16)
    w_pad = jnp.concatenate([wb[:, :, :n_a], zcols(n_a_p - n_a), wb[:, :, n_a:n_a + n_b + n_c],
                             zcols(n_c_p - n_c), wb[:, :, n_a + n_b + n_c:]], axis=2).reshape(depth * d, -1)
    n_exp = moe_router.shape[2]
    w_branch_b = w_branch.astype(BF16).reshape(depth * N_BRANCH, aw, d)
    w_out_b = w_out.astype(BF16).reshape(depth * d, d)
    wg = moe_w_gate.astype(BF16).reshape(depth * n_exp, d, -1)
    wu = moe_w_up.astype(BF16).reshape(depth * n_exp, d, -1)
    wd = moe_w_down.astype(BF16).reshape(depth * n_exp, -1, d)

    for i in range(depth):
        mu_p = _pad_cols(rwkv_mu[i], n_a_p)
        g_up_p = jnp.pad(rwkv_g_up[i], ((0, n_a_p - n_a), (0, 0)))
        qu = mla_q_up[i].reshape(cq, C_HEADS, C_NOPE_DIM + C_ROPE_DIM)
        qu = jnp.pad(qu, ((0, 0), (0, 0), (0, hw - C_NOPE_DIM - C_ROPE_DIM))).reshape(cq, C_HEADS * hw).astype(BF16)
        kvu = mla_kv_up[i].reshape(ckv, C_HEADS, C_NOPE_DIM + C_V_DIM)
        kvu = jnp.concatenate([kvu[:, :, :C_NOPE_DIM].reshape(ckv, -1), kvu[:, :, C_NOPE_DIM:].reshape(ckv, -1)],
                              axis=1).astype(BF16)
        rope_norm_p = _pad_cols(mla_rope_norm[i], LANES)

        mod = _adaln(cond8, ada_down[i], ada_up[i], ada_bias[i])[:2].reshape(2, N_MOD, d)

        h = _norm_mod(x_all, norm_mix[i], mod[:, 0], mod[:, 1], nu)
        w_in_i = functools.partial(_matmul, h, w_pad, w_row0=i * d, k=d)
        za = w_in_i(F32, n=n_a_p, name="w_in_a")
        zb = w_in_i(F32, w_col0=n_a_p, n=n_b, name="w_in_b")
        zc = w_in_i(F32, w_col0=n_a_p + n_b, n=n_c_p, name="w_in_c")
        gates = w_in_i(BF16, w_col0=n_a_p + n_b + n_c_p, n=N_BRANCH * d, act="sigmoid", name="w_in_gate")

        r, k, v, e, a, g = _rwkv_feat(za, mu_p, rwkv_w0[i], rwkv_w_up[i], rwkv_a0[i], rwkv_a_up[i], g_up_p, nu)
        ys = []
        for direction in (0, 1):
            p_, q_, r2, y0 = _rwkv_chunks(r, k, v, e, a, rwkv_k_k[i], rwkv_k_a[i], direction)
            ys.append(_rwkv_seq(p_, q_, r2, y0, direction, nu))
        ya = _rwkv_out(ys[0], ys[1], r, k, v, a, g, rwkv_lnx_w[i], rwkv_lnx_b[i], rwkv_r_k[i], rwkv_k_a[i])

        qkv_b = _gqa_prep(zb, cos, sin, gqa_q_norm[i], gqa_k_norm[i])
        yb = _gqa_attn(qkv_b, gqa_sink[i], nu, seq)

        assert cq % ckv == 0
        q_c = _matmul(zc, qu, F32, rms_gain=mla_q_a_norm[i], name="mla_q_up")
        kv_c = _matmul(zc, kvu, F32, a_col_block=cq // ckv, rms_gain=mla_kv_a_norm[i], name="mla_kv_up")
        qf, kf, vf = _mla_prep(q_c, kv_c, zc, (cq + ckv) // LANES, cos, sin, mla_nope_norm[i], rope_norm_p)
        yc = _mla_flash(qf, kf, vf, nu)

        merged = _merge(ya, yb, yc, w_branch_b, i, gates)
        x_all = _matmul(merged, w_out_b, F32, w_row0=i * d, k=d, resid=x_all, gate2=mod[:, 2], nu=nu, name="w_out")

        hf, aff = _norm_mod(x_all, norm_ffn[i], mod[:, 3], mod[:, 4], nu, router_t=moe_router[i].T)
        last = i == depth - 1
        x_all = _moe_stream(x_all, hf, aff, nu, seq, mod[1:2, 5], wg, wu, wd, i, rows_only=last)
        if not last:
            x_all = _moe_stream(x_all, hf, aff, 0, nu, mod[0:1, 5], wg, wu, wd, i)
    return x_all.reshape(bsz, seq, d)
```

```python
import functools
import math

import jax
import jax.numpy as jnp
from jax import lax
from jax.experimental import pallas as pl
from jax.experimental.pallas import tpu as pltpu

F32 = jnp.float32
BF16 = jnp.bfloat16
I32 = jnp.int32

GRID_W = 64
ROPE_DIM = 64
ROPE_THETA = 10000.0
NORM_EPS = 1e-6
NEG_INF = -1e30
N_MOD = 6
A_HEADS = 16
A_HEAD_DIM = 64
A_LNX_EPS = 64e-5
B_Q_HEADS = 16
B_KV_HEADS = 4
B_HEAD_DIM = 64
WINDOW = 128
BLOCK = 128
C_HEADS = 8
C_NOPE_DIM = 128
C_ROPE_DIM = 64
C_V_DIM = 128
N_BRANCH = 3
CAPACITY_FACTOR = 2

LANES = 128
CHUNK = 64
SEQ_GROUP = 4
ROW_TILE = 256
MOE_TILE = 128
MOE_ALIGN = 16
MOE_WIN = MOE_TILE + MOE_ALIGN
SEL_ROWS = 128
VMEM_MB = 56


def _cp(n_grid, vmem_mb=VMEM_MB):
    return pltpu.CompilerParams(dimension_semantics=("arbitrary",) * n_grid,
                                vmem_limit_bytes=vmem_mb * 1024 * 1024)


def _pick(n, cands):
    for c in cands:
        if n % c == 0:
            return c
    raise ValueError(f"no tile for {n} in {cands}")


def _dg(a, b, ca=1, cb=0):
    return lax.dot_general(a, b, (((ca,), (cb,)), ((), ())), preferred_element_type=F32)


def _split2(x):
    hi = x.astype(BF16)
    lo = (x - hi.astype(F32)).astype(BF16)
    return hi, lo


def _mm(a, b, passes=1, nt=False):
    cb = 1 if nt else 0
    if passes == 1:
        return _dg(a.astype(BF16), b.astype(BF16), 1, cb)
    ah, al = _split2(a)
    bh, bl = _split2(b)
    return _dg(ah, bh, 1, cb) + (_dg(ah, bl, 1, cb) + _dg(al, bh, 1, cb))


def _mm_exact_rhs(x, m_bf16):
    x1 = x.astype(BF16)
    r1 = x - x1.astype(F32)
    x2 = r1.astype(BF16)
    x3 = (r1 - x2.astype(F32)).astype(BF16)
    return _dg(x1, m_bf16) + (_dg(x2, m_bf16) + _dg(x3, m_bf16))


def _sigmoid(x):
    return 1.0 / (1.0 + jnp.exp(-x))


def _iota(shape, dim):
    return lax.broadcasted_iota(I32, shape, dim)


def _block_ones(n, blk):
    i = _iota((n, n), 0) // blk
    j = _iota((n, n), 1) // blk
    return jnp.where(i == j, 1.0, 0.0).astype(BF16)


def _seg_sum(x, ones_bf16):
    return _mm_exact_rhs(x, ones_bf16)


def _swap16(x):
    n = x.shape[-1]
    lane = _iota(x.shape, x.ndim - 1)
    fwd = pltpu.roll(x, n - 16, x.ndim - 1)
    bwd = pltpu.roll(x, 16, x.ndim - 1)
    return jnp.where((lane % 32) < 16, fwd, bwd)


def _adaln_kernel(c_ref, dn_ref, up_ref, b_ref, o_ref):
    c = c_ref[...]
    t = _mm(c * _sigmoid(c), dn_ref[...], 3)
    o_ref[...] = _mm(t, up_ref[...], 3) + b_ref[...]


def _adaln(cond8, down, up, bias):
    d, r = down.shape
    n = up.shape[1]
    tn = _pick(n, (4096, 2048, 1024, 512, 256, 128))
    return pl.pallas_call(
        _adaln_kernel,
        grid=(n // tn,),
        in_specs=[pl.BlockSpec((8, d), lambda j: (0, 0)),
                  pl.BlockSpec((d, r), lambda j: (0, 0)),
                  pl.BlockSpec((r, tn), lambda j: (0, j)),
                  pl.BlockSpec((1, tn), lambda j: (0, j))],
        out_specs=pl.BlockSpec((8, tn), lambda j: (0, j)),
        out_shape=jax.ShapeDtypeStruct((8, n), F32),
        compiler_params=_cp(1),
        name="adaln",
    )(cond8, down, up, bias.reshape(1, n))


def _moe_dcol(d):
    return min(d, 1024)


def _rows_are_ctx(tile_rows, row0, nu):
    return (row0 + _iota((tile_rows, 1), 0)) < nu


def _norm_mod_kernel(x_ref, g_ref, sh_ref, sc_ref, *rest, nu, tm, router):
    if router:
        wr_ref, h_ref, aff_ref = rest
    else:
        (h_ref,) = rest
    x = x_ref[...]
    is_u = pl.program_id(0) * tm < nu
    y = x * lax.rsqrt(jnp.mean(x * x, axis=-1, keepdims=True) + NORM_EPS) * g_ref[...]
    sh = jnp.where(is_u, sh_ref[0:1, :], sh_ref[1:2, :])
    sc = jnp.where(is_u, sc_ref[0:1, :], sc_ref[1:2, :])
    h = y * (1.0 + sc) + sh
    if router:
        dcol = h_ref.shape[2]
        for c in range(h_ref.shape[0]):
            h_ref[c] = h[:, c * dcol:(c + 1) * dcol].astype(h_ref.dtype)
    else:
        h_ref[...] = h.astype(h_ref.dtype)
    if router:
        logits = _mm(wr_ref[...], h, 3, nt=True)
        m = jnp.max(logits, axis=0, keepdims=True)
        p = jnp.exp(logits - m)
        aff_ref[...] = p / jnp.sum(p, axis=0, keepdims=True)


def _norm_mod(x_all, gain, shift2, scale2, nu, router_t=None):
    t, d = x_all.shape
    tm = ROW_TILE
    assert nu % tm == 0
    router = router_t is not None
    in_specs = [pl.BlockSpec((tm, d), lambda i: (i, 0)),
                pl.BlockSpec((1, d), lambda i: (0, 0)),
                pl.BlockSpec((2, d), lambda i: (0, 0)),
                pl.BlockSpec((2, d), lambda i: (0, 0))]
    args = [x_all, gain.reshape(1, d), shift2, scale2]
    out_specs = pl.BlockSpec((tm, d), lambda i: (i, 0))
    out_shape = jax.ShapeDtypeStruct((t, d), BF16)
    if router:
        e = router_t.shape[0]
        dcol = _moe_dcol(d)
        in_specs.append(pl.BlockSpec((e, d), lambda i: (0, 0)))
        args.append(router_t)
        out_specs = (pl.BlockSpec((d // dcol, tm, dcol), lambda i: (0, i, 0)), pl.BlockSpec((e, tm), lambda i: (0, i)))
        out_shape = (jax.ShapeDtypeStruct((d // dcol, t, dcol), BF16), jax.ShapeDtypeStruct((e, t), F32))
    return pl.pallas_call(
        functools.partial(_norm_mod_kernel, nu=nu, tm=tm, router=router),
        grid=(t // tm,), in_specs=in_specs, out_specs=out_specs, out_shape=out_shape,
        compiler_params=_cp(1), name="norm_mod_router" if router else "norm_mod",
    )(*args)


def _matmul_kernel(*refs, nu, tm, act, has_rms, has_resid):
    it = iter(refs)
    a_ref = next(it)
    w_ref = next(it)
    g_ref = next(it) if has_rms else None
    x_ref = next(it) if has_resid else None
    gate_ref = next(it) if has_resid else None
    o_ref = next(it)
    a = a_ref[...]
    if has_rms:
        af = a.astype(F32)
        a = af * lax.rsqrt(jnp.mean(af * af, axis=-1, keepdims=True) + NORM_EPS) * g_ref[...]
    acc = _dg(a.astype(BF16), w_ref[...])
    if act == "sigmoid":
        acc = _sigmoid(acc)
    if has_resid:
        is_u = _rows_are_ctx(tm, pl.program_id(1) * tm, nu)
        gate = jnp.where(is_u, gate_ref[0:1, :], gate_ref[1:2, :])
        acc = x_ref[...] + gate * acc
    o_ref[...] = acc.astype(o_ref.dtype)


def _matmul(a, w, out_dtype, *, a_col_block=0, w_row0=0, k=None, w_col0=0, n=None, act=None, rms_gain=None,
            resid=None, gate2=None, nu=0, name="matmul"):
    m = a.shape[0]
    k = w.shape[0] if k is None else k
    n = w.shape[1] if n is None else n
    tm = _pick(m, (640, 256, 128))
    tn = n if n <= 2048 else _pick(n, (1024, 768, 512, 384, 256, 128))
    has_rms = rms_gain is not None
    has_resid = resid is not None
    in_specs = [pl.BlockSpec((tm, k), lambda j, i: (i, a_col_block)),
                pl.BlockSpec((pl.Element(k), pl.Element(tn)),
                             lambda j, i: (w_row0, pl.multiple_of(w_col0 + j * tn, LANES)))]
    args = [a, w]
    if has_rms:
        in_specs.append(pl.BlockSpec((1, k), lambda j, i: (0, 0)))
        args.append(rms_gain.reshape(1, k))
    if has_resid:
        in_specs += [pl.BlockSpec((tm, tn), lambda j, i: (i, j)),
                     pl.BlockSpec((2, tn), lambda j, i: (0, j))]
        args += [resid, gate2]
    return pl.pallas_call(
        functools.partial(_matmul_kernel, nu=nu, tm=tm, act=act, has_rms=has_rms, has_resid=has_resid),
        grid=(n // tn, m // tm), in_specs=in_specs,
        out_specs=pl.BlockSpec((tm, tn), lambda j, i: (i, j)),
        out_shape=jax.ShapeDtypeStruct((m, n), out_dtype),
        compiler_params=_cp(2), name=name,
    )(*args)


def _rwkv_feat_kernel(z_ref, zp_ref, zn_ref, mu_ref, w0_ref, wup_ref, a0_ref, aup_ref, gup_ref,
                      r_ref, k_ref, v_ref, e_ref, a_ref, g_ref, *, nu, t_all, tm, aw, dr):
    z = z_ref[...]
    row = pl.program_id(0) * tm + _iota((tm, 1), 0)
    ri = _iota((tm, 1), 0)
    up1 = pltpu.roll(z, 1, 0)
    dn1 = pltpu.roll(z, tm - 1, 0)
    zp = jnp.where(ri == 0, zp_ref[7:8, :], up1)
    zn = jnp.where(ri == tm - 1, zn_ref[0:1, :], dn1)
    zp = jnp.where((row == 0) | (row == nu), 0.0, zp)
    zn = jnp.where((row == nu - 1) | (row == t_all - 1), 0.0, zn)
    zs = z + mu_ref[0:1, :] * (zp - z) + mu_ref[1:2, :] * (zn - z)
    r_ref[...] = zs[:, 0:aw]
    k_ref[...] = zs[:, aw:2 * aw]
    v_ref[...] = zs[:, 2 * aw:3 * aw]
    o = 3 * aw
    for n in range(2):
        wd = jnp.tanh(zs[:, o + n * dr:o + (n + 1) * dr])
        w = w0_ref[n:n + 1, :] + _mm(wd, wup_ref[n])
        sp = jnp.maximum(-w, 0.0) + jnp.log(1.0 + jnp.exp(-jnp.abs(w)))
        e_ref[n] = jnp.exp(-sp - 0.5)
    o += 2 * dr
    for n in range(2):
        ad = zs[:, o + n * dr:o + (n + 1) * dr]
        a_ref[n] = _sigmoid(a0_ref[n:n + 1, :] + _mm(ad, aup_ref[n]))
    o += 2 * dr
    g_ref[...] = _mm(_sigmoid(zs[:, o:]), gup_ref[...])


def _rwkv_feat(za, mu_p, w0, w_up, a0, a_up, g_up_p, nu):
    t, na = za.shape
    aw = w0.shape[1]
    dr = w_up.shape[1]
    tm = ROW_TILE
    nb8 = tm // 8
    last8 = t // 8 - 1
    row_spec = lambda w: pl.BlockSpec((tm, w), lambda i: (i, 0))
    full = lambda a: pl.BlockSpec(a.shape, lambda i: (0,) * a.ndim)
    out_rows = jax.ShapeDtypeStruct((t, aw), F32)
    out_dir = jax.ShapeDtypeStruct((2, t, aw), F32)
    dir_spec = pl.BlockSpec((2, tm, aw), lambda i: (0, i, 0))
    return pl.pallas_call(
        functools.partial(_rwkv_feat_kernel, nu=nu, t_all=t, tm=tm, aw=aw, dr=dr),
        grid=(t // tm,),
        in_specs=[row_spec(na),
                  pl.BlockSpec((8, na), lambda i: (jnp.maximum(i * nb8 - 1, 0), 0)),
                  pl.BlockSpec((8, na), lambda i: (jnp.minimum((i + 1) * nb8, last8), 0)),
                  full(mu_p), full(w0), full(w_up), full(a0), full(a_up), full(g_up_p)],
        out_specs=(row_spec(aw), row_spec(aw), row_spec(aw), dir_spec, dir_spec, row_spec(aw)),
        out_shape=(out_rows, out_rows, out_rows, out_dir, out_dir, out_rows),
        compiler_params=_cp(1), name="rwkv_feat",
    )(za, za, za, mu_p, w0, w_up, a0, a_up, g_up_p)


def _bmm(a, b, nt=False):
    cb = 2 if nt else 1
    return lax.dot_general(a.astype(BF16), b.astype(BF16), (((2,), (cb,)), ((0,), (0,))),
                           preferred_element_type=F32)


def _bmm_exact_lhs(m_bf16, x):
    dn = (((2,), (1,)), ((0,), (0,)))
    x1 = x.astype(BF16)
    r1 = x - x1.astype(F32)
    x2 = r1.astype(BF16)
    x3 = (r1 - x2.astype(F32)).astype(BF16)
    d = lambda y: lax.dot_general(m_bf16, y, dn, preferred_element_type=F32)
    return d(x1) + (d(x2) + d(x3))


def _bt(x):
    return jnp.stack([x[g].T for g in range(x.shape[0])], axis=0)


def _rwkv_chunk_math(r, k, v, e, a, kk_gain, ka_gain, reverse, groups):
    c = CHUNK
    hd = A_HEAD_DIM
    lane = _iota((1, LANES), 1)
    m_a = jnp.where(lane < hd, 1.0, 0.0)
    m_b = 1.0 - m_a
    ones_seg = _block_ones(LANES, hd)
    ti = _iota((groups, c, c), 1)
    tj = _iota((groups, c, c), 2)
    tri = jnp.where((tj >= ti) if reverse else (tj <= ti), 1.0, 0.0).astype(BF16)

    kk0 = k * kk_gain
    kk = kk0 / jnp.maximum(jnp.sqrt(_seg_sum(kk0 * kk0, ones_seg)), 1e-12)
    kt = k * (1.0 + (a - 1.0) * ka_gain)
    b = kk * a
    g3 = lambda x: x.reshape(groups, c, LANES)
    e3 = g3(e)
    cl = _bmm_exact_lhs(tri, e3)
    last = 0 if reverse else c - 1
    ctot = cl[:, last:last + 1, :]
    g_in = jnp.exp(-cl)
    g_ex = jnp.exp(e3 - cl)
    g_inv = jnp.exp(cl)
    g_end = jnp.exp(cl - ctot)
    st = lambda x: jnp.concatenate([x * m_a, x * m_b], axis=1)
    kk3, b3, kt3 = g3(kk), g3(b), g3(kt)
    kk2 = st(kk3 * g_ex)
    r2 = st(g3(r) * g_in)
    b2 = st(b3 * g_inv)
    k2 = st(kt3 * g_inv)
    v2 = st(g3(v))
    bg2 = st(b3 * g_end)
    kg2 = st(kt3 * g_end)

    s = _bmm(jnp.concatenate([kk2, r2], axis=1), jnp.concatenate([b2, k2], axis=1), nt=True)
    n2 = 2 * c
    i2 = _iota((n2, n2), 0)
    j2 = _iota((n2, n2), 1)
    il = i2 % c
    jl = j2 % c
    strict = (jl > il) if reverse else (jl < il)
    incl = (jl >= il) if reverse else (jl <= il)
    a_b = jnp.where(strict, s[:, :n2, :n2], 0.0)
    a_k = jnp.where(strict, s[:, :n2, n2:], 0.0)
    l_b = jnp.where(incl, s[:, n2:, :n2], 0.0)
    l_k = jnp.where(incl, s[:, n2:, n2:], 0.0)
    eye = jnp.where(i2 == j2, 1.0, 0.0)

    same = lambda m: (i2 // m) == (j2 // m)
    a_d = jnp.where(same(8), a_b, 0.0)
    a_d2 = _bmm(a_d, a_d)
    a_d4 = _bmm(a_d2, a_d2)
    t_inv = _bmm(_bmm(eye - a_d, eye + a_d2), eye + a_d4)
    m = 8
    while m < c:
        a_off = jnp.where(same(2 * m) & jnp.logical_not(same(m)), a_b, 0.0)
        t_inv = t_inv - _bmm(_bmm(t_inv, a_off), t_inv)
        m *= 2

    av = _bmm(a_k, v2)
    z12 = _bmm(t_inv, jnp.concatenate([kk2, av], axis=2))
    z1 = z12[:, :, :LANES]
    z2 = z12[:, :, LANES:]
    vz = jnp.concatenate([v2, z2], axis=1)
    p_bd = eye * jnp.exp(-ctot) - _bmm(_bt(bg2), z1)
    q_bd = _bmm(_bt(jnp.concatenate([kg2, -bg2], axis=1)), vz)
    r2s = r2 - _bmm(l_b, z1)
    y0s = _bmm(jnp.concatenate([l_k, -l_b], axis=2), vz)
    cp = lambda x: x[:, :x.shape[1] // 2] + x[:, x.shape[1] // 2:]
    return cp(p_bd), cp(q_bd), cp(r2s).reshape(groups * c, LANES), cp(y0s).reshape(groups * c, LANES)


def _rwkv_chunk_kernel(r_ref, k_ref, v_ref, e_ref, a_ref, kkg_ref, kag_ref,
                       p_ref, q_ref, r2_ref, y0_ref, *, reverse, groups):
    p, q, r2, y0 = _rwkv_chunk_math(r_ref[...], k_ref[...], v_ref[...], e_ref[0], a_ref[0],
                                    kkg_ref[...], kag_ref[...], reverse, groups)
    p_ref[0] = p
    q_ref[0] = q
    r2_ref[...] = r2
    y0_ref[...] = y0


def _rwkv_chunks(r, k, v, e, a, k_k, k_a, direction):
    t, aw = r.shape
    npair = aw // LANES
    nc = t // CHUNK
    groups = _pick(nc, (20, 10, 4))
    rows = groups * CHUNK
    row_spec = pl.BlockSpec((rows, LANES), lambda p, i: (i, p))
    dir_spec = pl.BlockSpec((1, rows, LANES), lambda p, i: (direction, i, p))
    par_spec = pl.BlockSpec((1, LANES), lambda p, i: (0, p))
    pq_spec = pl.BlockSpec((1, groups, A_HEAD_DIM, LANES), lambda p, i: (p, i, 0, 0))
    pq_shape = jax.ShapeDtypeStruct((npair, nc, A_HEAD_DIM, LANES), F32)
    ry_shape = jax.ShapeDtypeStruct((t, aw), F32)
    return pl.pallas_call(
        functools.partial(_rwkv_chunk_kernel, reverse=bool(direction), groups=groups),
        grid=(npair, t // rows),
        in_specs=[row_spec, row_spec, row_spec, dir_spec, dir_spec, par_spec, par_spec],
        out_specs=(pq_spec, pq_spec, row_spec, row_spec),
        out_shape=(pq_shape, pq_shape, ry_shape, ry_shape),
        compiler_params=_cp(2), name="rwkv_chunks_bwd" if direction else "rwkv_chunks_fwd",
    )(r, k, v, e, a, k_k.reshape(1, aw), k_a.reshape(1, aw))


def _pair_block_diag(x):
    lane = _iota(x.shape, 1)
    return jnp.concatenate([jnp.where(lane < A_HEAD_DIM, x, 0.0), jnp.where(lane >= A_HEAD_DIM, x, 0.0)], axis=0)


def _rwkv_seq_kernel(p_ref, q_ref, r2_ref, y0_ref, y_ref, h_ref, *, reverse, groups, npair):
    c = CHUNK

    @pl.when(pl.program_id(0) == 0)
    def _():
        h_ref[...] = jnp.zeros_like(h_ref)

    order = range(groups - 1, -1, -1) if reverse else range(groups)
    for g in order:
        sl = slice(g * c, (g + 1) * c)
        for p in range(npair):
            ls = slice(p * LANES, (p + 1) * LANES)
            h = h_ref[p]
            y_ref[sl, ls] = _mm(r2_ref[sl, ls], h) + y0_ref[sl, ls]
            h_ref[p] = _mm(_pair_block_diag(p_ref[p, g]), h) + _pair_block_diag(q_ref[p, g])


def _rwkv_seq(p, q, r2, y0, direction, nu):
    npair, nc = p.shape[:2]
    t, aw = r2.shape
    groups = SEQ_GROUP
    rows = groups * CHUNK
    nb = t // rows
    nbu = nu // rows
    assert nu % rows == 0
    if direction:
        blk = lambda s: jnp.where(s < nbu, nbu - 1 - s, nb + nbu - 1 - s)
    else:
        blk = lambda s: s
    pq_spec = pl.BlockSpec((npair, groups, A_HEAD_DIM, LANES), lambda s: (0, blk(s), 0, 0))
    row_spec = pl.BlockSpec((rows, aw), lambda s: (blk(s), 0))
    return pl.pallas_call(
        functools.partial(_rwkv_seq_kernel, reverse=bool(direction), groups=groups, npair=npair),
        grid=(nb,),
        in_specs=[pq_spec, pq_spec, row_spec, row_spec],
        out_specs=row_spec,
        out_shape=jax.ShapeDtypeStruct((t, aw), F32),
        scratch_shapes=[pltpu.VMEM((npair, LANES, LANES), F32)],
        compiler_params=_cp(1), name="rwkv_seq_bwd" if direction else "rwkv_seq_fwd",
    )(p, q, r2, y0)


def _rwkv_out_kernel(yf_ref, yb_ref, r_ref, k_ref, v_ref, a_ref, g_ref, lw_ref, lb_ref, rk_ref, ka_ref, o_ref, *, aw):
    hd = A_HEAD_DIM
    ones_seg = _block_ones(LANES, hd)
    for p in range(aw // LANES):
        ls = slice(p * LANES, (p + 1) * LANES)
        y = yf_ref[:, ls] + yb_ref[:, ls]
        mean = _seg_sum(y, ones_seg) * (1.0 / hd)
        yc = y - mean
        var = _seg_sum(yc * yc, ones_seg) * (1.0 / hd)
        yn = yc * lax.rsqrt(var + A_LNX_EPS) * lw_ref[:, ls] + lb_ref[:, ls]
        r = r_ref[:, ls]
        k = k_ref[:, ls]
        v = v_ref[:, ls]
        bonus = jnp.zeros_like(y)
        for n in range(2):
            kt = k * (1.0 + (a_ref[n, :, ls] - 1.0) * ka_ref[:, ls])
            bonus = bonus + _seg_sum(r * kt * rk_ref[:, ls], ones_seg) * v
        o_ref[:, ls] = ((yn + bonus) * g_ref[:, ls]).astype(o_ref.dtype)


def _rwkv_out(yf, yb, r, k, v, a, g, lnx_w, lnx_b, r_k, k_a):
    t, aw = r.shape
    tm = ROW_TILE
    row_spec = pl.BlockSpec((tm, aw), lambda i: (i, 0))
    par_spec = pl.BlockSpec((1, aw), lambda i: (0, 0))
    return pl.pallas_call(
        functools.partial(_rwkv_out_kernel, aw=aw),
        grid=(t // tm,),
        in_specs=[row_spec] * 5 + [pl.BlockSpec((2, tm, aw), lambda i: (0, i, 0)), row_spec] + [par_spec] * 4,
        out_specs=row_spec,
        out_shape=jax.ShapeDtypeStruct((t, aw), BF16),
        compiler_params=_cp(1), name="rwkv_out",
    )(yf, yb, r, k, v, a, g, lnx_w.reshape(1, aw), lnx_b.reshape(1, aw), r_k.reshape(1, aw), k_a.reshape(1, aw))


def _rms_rope_slab(x, gain, cos, sin, ones_seg):
    ms = _seg_sum(x * x, ones_seg) * (1.0 / B_HEAD_DIM)
    y = x * lax.rsqrt(ms + NORM_EPS) * gain
    return y * cos + _swap16(y) * sin


def _gqa_prep_kernel(z_ref, cos_ref, sin_ref, qg_ref, kg_ref, o_ref, *, qw, kw):
    ones_seg = _block_ones(LANES, B_HEAD_DIM)
    cos = cos_ref[...]
    sin = sin_ref[...]
    scale = B_HEAD_DIM ** -0.5 * math.log2(math.e)
    for s in range(qw // LANES):
        ls = slice(s * LANES, (s + 1) * LANES)
        o_ref[:, ls] = (_rms_rope_slab(z_ref[:, ls], qg_ref[...], cos, sin, ones_seg) * scale).astype(o_ref.dtype)
    for s in range(kw // LANES):
        ls = slice(qw + s * LANES, qw + (s + 1) * LANES)
        o_ref[:, ls] = _rms_rope_slab(z_ref[:, ls], kg_ref[...], cos, sin, ones_seg).astype(o_ref.dtype)
    o_ref[:, qw + kw:] = z_ref[:, qw + kw:].astype(o_ref.dtype)


def _gqa_prep(zb, cos, sin, q_norm, k_norm):
    t, nb = zb.shape
    tm = ROW_TILE
    qw = B_Q_HEADS * B_HEAD_DIM
    kw = B_KV_HEADS * B_HEAD_DIM
    tile2 = lambda g: jnp.tile(g.reshape(1, B_HEAD_DIM), (1, LANES // B_HEAD_DIM))
    return pl.pallas_call(
        functools.partial(_gqa_prep_kernel, qw=qw, kw=kw),
        grid=(t // tm,),
        in_specs=[pl.BlockSpec((tm, nb), lambda i: (i, 0)),
                  pl.BlockSpec((tm, LANES), lambda i: (i, 0)),
                  pl.BlockSpec((tm, LANES), lambda i: (i, 0)),
                  pl.BlockSpec((1, LANES), lambda i: (0, 0)),
                  pl.BlockSpec((1, LANES), lambda i: (0, 0))],
        out_specs=pl.BlockSpec((tm, nb), lambda i: (i, 0)),
        out_shape=jax.ShapeDtypeStruct((t, nb), BF16),
        compiler_params=_cp(1), name="gqa_prep",
    )(zb, cos, sin, tile2(q_norm), tile2(k_norm))


def _gqa_attn_kernel(sink_ref, q_ref, c_ref, kp_ref, ko_ref, kn_ref, o_ref, *, nu, seq, qw, kw):
    hd = B_HEAD_DIM
    grp = B_Q_HEADS // B_KV_HEADS
    blk = BLOCK
    j = pl.program_id(0)
    jb = j - nu // blk
    nkeys = nu + 3 * blk
    rows = grp * blk
    qi = _iota((rows, nkeys), 0) % blk
    kc = _iota((rows, nkeys), 1)
    q_pos = jb * blk + qi
    k_pos = (jb - 1) * blk + (kc - nu)
    band_ok = (jnp.abs(k_pos - q_pos) <= WINDOW) & (k_pos >= 0) & (k_pos < seq) & (jb >= 0)
    valid = (kc < nu) | band_ok
    rg = _iota((rows, 1), 0) // blk
    outs = []
    for h in range(B_KV_HEADS):
        ks = slice(qw + h * hd, qw + (h + 1) * hd)
        vs = slice(qw + kw + h * hd, qw + kw + (h + 1) * hd)
        k_all = jnp.concatenate([c_ref[:, ks], kp_ref[:, ks], ko_ref[:, ks], kn_ref[:, ks]], axis=0)
        v_all = jnp.concatenate([c_ref[:, vs], kp_ref[:, vs], ko_ref[:, vs], kn_ref[:, vs]], axis=0)
        q4 = jnp.concatenate([q_ref[:, (h * grp + g) * hd:(h * grp + g + 1) * hd] for g in range(grp)], axis=0)
        s = jnp.where(valid, _dg(q4, k_all, 1, 1), NEG_INF)
        sink = jnp.zeros((rows, 1), F32)
        for g in range(grp):
            sink = jnp.where(rg == g, sink_ref[h * grp + g] * math.log2(math.e), sink)
        m = jnp.maximum(jnp.max(s, axis=1, keepdims=True), sink)
        p = jnp.exp2((s - m).astype(BF16))
        pv = _dg(p, jnp.concatenate([v_all, jnp.ones_like(v_all)], axis=1))
        o = pv[:, :hd] / (pv[:, hd:hd + 1] + jnp.exp2(sink - m))
        outs += [o[g * blk:(g + 1) * blk, :] for g in range(grp)]
    o_ref[...] = jnp.concatenate(outs, axis=1).astype(o_ref.dtype)


def _gqa_attn(qkv, sink, nu, seq):
    t, nb = qkv.shape
    qw = B_Q_HEADS * B_HEAD_DIM
    kw = B_KV_HEADS * B_HEAD_DIM
    blk = BLOCK
    nblk = t // blk
    band = lambda off: pl.BlockSpec((blk, nb), lambda j: (jnp.clip(j + off, 0, nblk - 1), 0))
    return pl.pallas_call(
        functools.partial(_gqa_attn_kernel, nu=nu, seq=seq, qw=qw, kw=kw),
        grid=(nblk,),
        in_specs=[pl.BlockSpec(memory_space=pltpu.SMEM),
                  pl.BlockSpec((blk, nb), lambda j: (j, 0)),
                  pl.BlockSpec((nu, nb), lambda j: (0, 0)),
                  band(-1), band(0), band(1)],
        out_specs=pl.BlockSpec((blk, qw), lambda j: (j, 0)),
        out_shape=jax.ShapeDtypeStruct((t, qw), BF16),
        compiler_params=_cp(1), name="gqa_attn",
    )(sink, qkv, qkv, qkv, qkv, qkv)


def _mla_prep_kernel(q_ref, kv_ref, kr_ref, cos_ref, sin_ref, nn_ref, rn_ref, qo_ref, ko_ref, vo_ref):
    cos = cos_ref[...]
    sin = sin_ref[...]
    dn = C_NOPE_DIM
    hw = 2 * LANES
    scale = (C_NOPE_DIM + C_ROPE_DIM) ** -0.5 * math.log2(math.e)
    ones = jnp.ones((q_ref.shape[0], C_V_DIM), vo_ref.dtype)

    def rms_rope(x, gain):
        ms = jnp.sum(x * x, axis=-1, keepdims=True) * (1.0 / C_ROPE_DIM)
        y = x * lax.rsqrt(ms + NORM_EPS) * gain
        return y * cos + _swap16(y) * sin

    def rms(x, gain):
        return x * lax.rsqrt(jnp.mean(x * x, axis=-1, keepdims=True) + NORM_EPS) * gain

    kr = rms_rope(kr_ref[...], rn_ref[1:2, :]).astype(ko_ref.dtype)
    for h in range(C_HEADS):
        qn = rms(q_ref[:, h * hw:h * hw + dn], nn_ref[0:1, :])
        qr = rms_rope(q_ref[:, h * hw + dn:(h + 1) * hw], rn_ref[0:1, :])
        qo_ref[:, h * hw:h * hw + dn] = (qn * scale).astype(qo_ref.dtype)
        qo_ref[:, h * hw + dn:(h + 1) * hw] = (qr * scale).astype(qo_ref.dtype)
        ko_ref[:, h * hw:h * hw + dn] = rms(kv_ref[:, h * dn:(h + 1) * dn], nn_ref[1:2, :]).astype(ko_ref.dtype)
        ko_ref[:, h * hw + dn:(h + 1) * hw] = kr
        vo_ref[:, h * hw:h * hw + C_V_DIM] = kv_ref[:, (C_HEADS + h) * dn:(C_HEADS + h + 1) * dn].astype(vo_ref.dtype)
        vo_ref[:, h * hw + C_V_DIM:(h + 1) * hw] = ones


def _mla_prep(q, kv, zc, kr_col_block, cos, sin, nope_norm, rope_norm_p):
    t = q.shape[0]
    tm = ROW_TILE
    hw = 2 * LANES
    row = lambda w: pl.BlockSpec((tm, w), lambda i: (i, 0))
    full = lambda a: pl.BlockSpec(a.shape, lambda i: (0, 0))
    return pl.pallas_call(
        _mla_prep_kernel,
        grid=(t // tm,),
        in_specs=[row(q.shape[1]), row(kv.shape[1]),
                  pl.BlockSpec((tm, LANES), lambda i: (i, kr_col_block)),
                  row(LANES), row(LANES), full(nope_norm), full(rope_norm_p)],
        out_specs=(row(C_HEADS * hw),) * 3,
        out_shape=(jax.ShapeDtypeStruct((t, C_HEADS * hw), BF16),) * 3,
        compiler_params=_cp(1), name="mla_prep",
    )(q, kv, zc, cos, sin, nope_norm, rope_norm_p)


def _mla_flash_kernel(q_ref, k_ref, v_ref, o_ref, m_ref, acc_ref, *, nu, tq, tk, sub):
    qi = pl.program_id(1)
    ki = pl.program_id(2)

    @pl.when(ki == 0)
    def _():
        m_ref[...] = jnp.full_like(m_ref, NEG_INF)
        acc_ref[...] = jnp.zeros_like(acc_ref)

    def update(masked):
        for qs in range(tq // sub):
            rs = slice(qs * sub, (qs + 1) * sub)
            s = _dg(q_ref[rs, :], k_ref[...], 1, 1)
            if masked and qs * sub < nu:
                qrow = qi * tq + qs * sub + _iota((sub, tk), 0)
                kcol = ki * tk + _iota((sub, tk), 1)
                s = jnp.where((qrow < nu) & (kcol >= nu), NEG_INF, s)
            m_prev = m_ref[rs, :]
            m_next = jnp.maximum(m_prev, jnp.max(s, axis=1, keepdims=True))
            alpha = jnp.exp2(m_prev - m_next)
            p = jnp.exp2((s - m_next[:, 0:1]).astype(BF16))
            acc_ref[rs, :] = jnp.concatenate([alpha, alpha], axis=1) * acc_ref[rs, :] + _dg(p, v_ref[...])
            m_ref[rs, :] = m_next

    has_ctx_rows = qi * tq < nu

    @pl.when(has_ctx_rows)
    def _():
        update(True)

    @pl.when(jnp.logical_not(has_ctx_rows))
    def _():
        update(False)

    @pl.when(ki == pl.num_programs(2) - 1)
    def _():
        o_ref[...] = (acc_ref[:, :C_V_DIM] / acc_ref[:, C_V_DIM:]).astype(o_ref.dtype)


def _mla_flash(qf, kf, vf, nu):
    t = qf.shape[0]
    hw = 2 * LANES
    tq = _pick(t, (3328, 256))
    tk = _pick(t, (3328, 256))
    return pl.pallas_call(
        functools.partial(_mla_flash_kernel, nu=nu, tq=tq, tk=tk, sub=256),
        grid=(C_HEADS, t // tq, t // tk),
        in_specs=[pl.BlockSpec((tq, hw), lambda h, i, j: (i, h)),
                  pl.BlockSpec((tk, hw), lambda h, i, j: (j, h)),
                  pl.BlockSpec((tk, hw), lambda h, i, j: (j, h))],
        out_specs=pl.BlockSpec((tq, C_V_DIM), lambda h, i, j: (i, h)),
        out_shape=jax.ShapeDtypeStruct((t, C_HEADS * C_V_DIM), BF16),
        scratch_shapes=[pltpu.VMEM((tq, LANES), F32), pltpu.VMEM((tq, hw), F32)],
        compiler_params=_cp(3), name="mla_flash",
    )(qf, kf, vf)


def _merge_kernel(ya_ref, yb_ref, yc_ref, w_ref, g0_ref, g1_ref, g2_ref, o_ref):
    acc = g0_ref[...].astype(F32) * _dg(ya_ref[...], w_ref[0])
    acc = acc + g1_ref[...].astype(F32) * _dg(yb_ref[...], w_ref[1])
    acc = acc + g2_ref[...].astype(F32) * _dg(yc_ref[...], w_ref[2])
    o_ref[...] = acc.astype(o_ref.dtype)


def _merge(ya, yb, yc, w_branch, layer, gates):
    t, bw = ya.shape
    d = w_branch.shape[2]
    tm = _pick(t, (640, 256, 128))
    tn = _pick(d, (1024, 512, 256, 128))
    nj = d // tn
    y_spec = pl.BlockSpec((tm, bw), lambda j, i: (i, 0))
    gate = lambda b: pl.BlockSpec((tm, tn), lambda j, i: (i, b * nj + j))
    return pl.pallas_call(
        _merge_kernel,
        grid=(nj, t // tm),
        in_specs=[y_spec, y_spec, y_spec, pl.BlockSpec((N_BRANCH, bw, tn), lambda j, i: (layer, 0, j)),
                  gate(0), gate(1), gate(2)],
        out_specs=pl.BlockSpec((tm, tn), lambda j, i: (i, j)),
        out_shape=jax.ShapeDtypeStruct((t, d), BF16),
        compiler_params=_cp(2), name="merge",
    )(ya, yb, yc, w_branch, gates, gates, gates)


def _moe_select_kernel(aff_ref, loc_ref, base_ref, *, cap, n_exp):
    aff = aff_ref[...]
    bits = pltpu.bitcast(aff, I32)
    count = lambda m: jnp.sum(jnp.sum(jnp.where(m, 1.0, 0.0), axis=1, keepdims=True), axis=2, keepdims=True)
    theta = jnp.zeros((n_exp, 1, 1), I32)
    for bit in range(30, -1, -1):
        cand = theta | (1 << bit)
        theta = jnp.where(count(bits >= cand) >= cap, cand, theta)
    gt = bits > theta
    eq = bits == theta
    need = cap - count(gt)
    r = aff.shape[1]
    upper = jnp.where(_iota((LANES, LANES), 0) <= _iota((LANES, LANES), 1), 1.0, 0.0).astype(BF16)
    lower_strict = jnp.where(_iota((r, r), 1) < _iota((r, r), 0), 1.0, 0.0).astype(BF16)

    def prefix(x):
        incl = _dg(x.astype(BF16), upper)
        tot = jnp.broadcast_to(incl[:, LANES - 1:LANES], (r, LANES))
        return incl, _dg(lower_strict, tot.astype(BF16))

    for e in range(n_exp):
        xe = jnp.where(eq[e], 1.0, 0.0)
        incl, base = prefix(xe)
        take = eq[e] & ((incl - xe + base) < need[e])
        sel = jnp.where(gt[e] | take, 1.0, 0.0)
        incl, base = prefix(sel)
        loc_ref[e] = jnp.where(sel > 0.0, incl - sel, -4096.0)
        base_ref[e] = base


def _moe_select(aff3, cap):
    e, r, _ = aff3.shape
    spec = pl.BlockSpec((e, r, LANES), lambda i: (0, 0, 0))
    shape = jax.ShapeDtypeStruct((e, r, LANES), F32)
    return pl.pallas_call(
        functools.partial(_moe_select_kernel, cap=cap, n_exp=e),
        grid=(1,), in_specs=[spec], out_specs=(spec, spec), out_shape=(shape, shape),
        compiler_params=_cp(1), name="moe_select",
    )(aff3)


def _window_start(base_ref, e, i):
    b = base_ref[e, i]
    a = (b // MOE_ALIGN) * MOE_ALIGN
    return pl.multiple_of(a, MOE_ALIGN), b - a


def _moe_gather_kernel(base_ref, loc_ref, aff_ref, h_ref, xe_ref, ge_ref, *, tiles, group):
    g = pl.program_id(0)
    i = pl.program_id(2)

    @pl.when(i == 0)
    def _():
        xe_ref[...] = jnp.zeros_like(xe_ref)
        ge_ref[...] = jnp.zeros_like(ge_ref)

    win_row = _iota((MOE_WIN, MOE_TILE), 0).astype(F32)
    for k in range(group):
        e = g * group + k
        for r in range(tiles):
            a, off = _window_start(base_ref, e, i * tiles + r)
            slot = loc_ref[k, r] + off.astype(F32)
            onehot = jnp.where(win_row == slot, 1.0, 0.0)
            rows = _dg(onehot.astype(BF16), h_ref[r * MOE_TILE:(r + 1) * MOE_TILE, :])
            head = pl.ds(a, MOE_ALIGN)
            xe_ref[k, head, :] = (xe_ref[k, head, :].astype(F32) + rows[:MOE_ALIGN]).astype(xe_ref.dtype)
            xe_ref[k, pl.ds(a + MOE_ALIGN, MOE_TILE), :] = rows[MOE_ALIGN:].astype(xe_ref.dtype)
            win = pl.ds(a, MOE_WIN)
            gsel = jnp.sum(onehot * aff_ref[k, r], axis=1, keepdims=True)
            ge_ref[k, win, :] = ge_ref[k, win, :] + gsel


def _moe_gather(base_i, loc, aff3, h_chunks, t, row0, n, cap_p):
    e = loc.shape[0]
    nchunk, _, dcol = h_chunks.shape
    d = nchunk * dcol
    nt = n // MOE_TILE
    tiles = _pick(nt, (8, 4, 2, 1))
    rows = tiles * MOE_TILE
    group = _pick(e, (4, 2, 1))
    sel_spec = pl.BlockSpec((group, tiles, 1, LANES), lambda g, c, i, b: (g, i, 0, 0))
    grid_spec = pltpu.PrefetchScalarGridSpec(
        num_scalar_prefetch=1, grid=(e // group, d // dcol, nt // tiles),
        in_specs=[sel_spec, sel_spec,
                  pl.BlockSpec((pl.Element(rows), pl.Element(dcol)),
                               lambda g, c, i, b: (pl.multiple_of(c * t + row0 + i * rows, MOE_TILE), 0))],
        out_specs=(pl.BlockSpec((group, cap_p, dcol), lambda g, c, i, b: (g, 0, c)),
                   pl.BlockSpec((group, cap_p, LANES), lambda g, c, i, b: (g, 0, 0))))
    return pl.pallas_call(
        functools.partial(_moe_gather_kernel, tiles=tiles, group=group), grid_spec=grid_spec,
        out_shape=(jax.ShapeDtypeStruct((e, cap_p, d), BF16), jax.ShapeDtypeStruct((e, cap_p, LANES), F32)),
        compiler_params=_cp(3), name="moe_gather",
    )(base_i, loc, aff3, h_chunks.reshape(nchunk * t, dcol))


def _moe_ffn_kernel(x_ref, g_ref, wg_ref, wu_ref, wd_ref, y_ref):
    x = x_ref[0]
    hg = _dg(x, wg_ref[0])
    hid = hg * _sigmoid(hg) * _dg(x, wu_ref[0])
    y = _dg(hid.astype(BF16), wd_ref[0]) * g_ref[0][:, 0:1]
    y_ref[0] = y.astype(y_ref.dtype)


def _moe_ffn(xe, ge, w_gate, w_up, w_down, layer, tc):
    e, cap_p, d = xe.shape
    f = w_gate.shape[2]
    return pl.pallas_call(
        _moe_ffn_kernel,
        grid=(e, cap_p // tc),
        in_specs=[pl.BlockSpec((1, tc, d), lambda ee, i: (ee, i, 0)),
                  pl.BlockSpec((1, tc, LANES), lambda ee, i: (ee, i, 0)),
                  pl.BlockSpec((1, d, f), lambda ee, i: (layer * e + ee, 0, 0)),
                  pl.BlockSpec((1, d, f), lambda ee, i: (layer * e + ee, 0, 0)),
                  pl.BlockSpec((1, f, d), lambda ee, i: (layer * e + ee, 0, 0))],
        out_specs=pl.BlockSpec((1, tc, d), lambda ee, i: (ee, i, 0)),
        out_shape=jax.ShapeDtypeStruct((e, cap_p, d), BF16),
        compiler_params=_cp(2), name="moe_ffn",
    )(xe, ge, w_gate, w_up, w_down)


def _moe_combine_kernel(base_ref, loc_ref, *rest, n_exp):
    y_refs = rest[:n_exp]
    x_ref, gate_ref, o_ref = rest[n_exp:]
    i = pl.program_id(0)
    win_row = _iota((MOE_TILE, MOE_WIN), 1).astype(F32)
    acc = None
    for e in range(n_exp):
        _, off = _window_start(base_ref, e, i)
        slot_row = jnp.broadcast_to(loc_ref[e, 0] + off.astype(F32), (MOE_TILE, LANES))
        slot_col = slot_row.T
        slot_col = jnp.concatenate([slot_col, slot_col[:, :MOE_WIN - LANES]], axis=1)
        onehot = jnp.where(win_row == slot_col, 1.0, 0.0).astype(BF16)
        part = _dg(onehot, y_refs[e][...])
        acc = part if acc is None else acc + part
    o_ref[...] = x_ref[...] + gate_ref[...] * acc


def _moe_combine(base_i, loc, ye, x_all, gate_row, row0, n, rows_only):
    e, cap_p, d = ye.shape
    nt = n // MOE_TILE
    t0 = row0 // MOE_TILE
    o0 = 0 if rows_only else t0
    dcol = d

    def y_spec(ee):
        def index(i, j, b):
            start = ee * cap_p + (b[ee, i] // MOE_ALIGN) * MOE_ALIGN
            return pl.multiple_of(start, MOE_ALIGN), pl.multiple_of(j * dcol, LANES)
        return pl.BlockSpec((pl.Element(MOE_WIN), pl.Element(dcol)), index)

    grid_spec = pltpu.PrefetchScalarGridSpec(
        num_scalar_prefetch=1, grid=(nt, d // dcol),
        in_specs=[pl.BlockSpec((e, 1, 1, LANES), lambda i, j, b: (0, i, 0, 0))]
                 + [y_spec(ee) for ee in range(e)]
                 + [pl.BlockSpec((MOE_TILE, dcol), lambda i, j, b: (t0 + i, j)),
                    pl.BlockSpec((1, dcol), lambda i, j, b: (0, j))],
        out_specs=pl.BlockSpec((MOE_TILE, dcol), lambda i, j, b: (o0 + i, j)))
    x_index = 2 + e
    return pl.pallas_call(
        functools.partial(_moe_combine_kernel, n_exp=e), grid_spec=grid_spec,
        out_shape=jax.ShapeDtypeStruct((n, d) if rows_only else x_all.shape, F32),
        input_output_aliases={} if rows_only else {x_index: 0},
        compiler_params=_cp(2), name="moe_combine",
    )(base_i, loc, *([ye.reshape(e * cap_p, d)] * e), x_all, gate_row)


def _moe_stream(x_all, h_chunks, aff, row0, n, gate_row, w_gate, w_up, w_down, layer, rows_only=False):
    e = aff.shape[0]
    cap = (CAPACITY_FACTOR * n) // e
    tc = 256 if cap >= 256 else 64
    cap_p = -(-(cap + MOE_WIN) // tc) * tc
    n_sel = SEL_ROWS * LANES
    aff_s = lax.dynamic_slice_in_dim(aff, row0, n, axis=1)
    aff3 = jnp.pad(aff_s, ((0, 0), (0, n_sel - n))).reshape(e, SEL_ROWS, LANES)
    loc, base = _moe_select(aff3, cap)
    base_i = base[:, :, 0].astype(I32)
    loc4 = loc.reshape(e, SEL_ROWS, 1, LANES)
    aff4 = aff3.reshape(e, SEL_ROWS, 1, LANES)
    xe, ge = _moe_gather(base_i, loc4, aff4, h_chunks, x_all.shape[0], row0, n, cap_p)
    ye = _moe_ffn(xe, ge, w_gate, w_up, w_down, layer, tc)
    return _moe_combine(base_i, loc4, ye, x_all, gate_row, row0, n, rows_only)


def _rope_tables(nu, seq):
    n_freq = ROPE_DIM // 4
    pos = jnp.arange(seq)
    inv = jnp.power(ROPE_THETA, -jnp.arange(n_freq, dtype=F32) / n_freq)
    ang_r = (pos // GRID_W).astype(F32)[:, None] * inv[None]
    ang_c = (pos % GRID_W).astype(F32)[:, None] * inv[None]
    cos = jnp.concatenate([jnp.cos(ang_r)] * 2 + [jnp.cos(ang_c)] * 2, axis=1)
    sin = jnp.concatenate([-jnp.sin(ang_r), jnp.sin(ang_r), -jnp.sin(ang_c), jnp.sin(ang_c)], axis=1)
    cos = jnp.concatenate([jnp.ones((nu, ROPE_DIM), F32), cos], axis=0)
    sin = jnp.concatenate([jnp.zeros((nu, ROPE_DIM), F32), sin], axis=0)
    rep = LANES // ROPE_DIM
    return jnp.tile(cos, (1, rep)), jnp.tile(sin, (1, rep))


def _pad_cols(w, n):
    return jnp.pad(w, ((0, 0), (0, n - w.shape[1])))


def _round_up(x, m):
    return -(-x // m) * m


def kernel(x, c, ctx, c_ctx, norm_mix, norm_ffn, ada_down, ada_up, ada_bias, w_in, rwkv_mu, rwkv_w0, rwkv_w_up, rwkv_a0, rwkv_a_up, rwkv_g_up, rwkv_k_k, rwkv_k_a, rwkv_r_k, rwkv_lnx_w, rwkv_lnx_b, gqa_q_norm, gqa_k_norm, gqa_sink, mla_q_a_norm, mla_q_up, mla_kv_a_norm, mla_kv_up, mla_nope_norm, mla_rope_norm, w_branch, w_out, moe_router, moe_w_gate, moe_w_up, moe_w_down):
    bsz, seq, d = x.shape
    assert bsz == 1
    nu = ctx.shape[1]
    depth = w_in.shape[0]
    aw = A_HEADS * A_HEAD_DIM
    dr = rwkv_w_up.shape[2]
    gr = rwkv_g_up.shape[1]
    n_a = 3 * aw + 4 * dr + gr
    n_a_p = _round_up(n_a, LANES)
    n_b = (B_Q_HEADS + 2 * B_KV_HEADS) * B_HEAD_DIM
    cq = mla_q_up.shape[1]
    ckv = mla_kv_up.shape[1]
    n_c = cq + ckv + C_ROPE_DIM
    n_c_p = cq + ckv + LANES
    hw = 2 * LANES

    x_all = jnp.concatenate([ctx[0], x[0]], axis=0)
    cond8 = jnp.zeros((8, d), F32).at[0].set(c_ctx).at[1].set(c[0])
    cos, sin = _rope_tables(nu, seq)

    zcols = lambda w_: jnp.zeros((depth, d, w_), BF16)
    wb = w_in.astype(BF16)
    w_pad = jnp.concatenate([wb[:, :, :n_a], zcols(n_a_p - n_a), wb[:, :, n_a:n_a + n_b + n_c],
                             zcols(n_c_p - n_c), wb[:, :, n_a + n_b + n_c:]], axis=2).reshape(depth * d, -1)
    n_exp = moe_router.shape[2]
    w_branch_b = w_branch.astype(BF16).reshape(depth * N_BRANCH, aw, d)
    w_out_b = w_out.astype(BF16).reshape(depth * d, d)
    wg = moe_w_gate.astype(BF16).reshape(depth * n_exp, d, -1)
    wu = moe_w_up.astype(BF16).reshape(depth * n_exp, d, -1)
    wd = moe_w_down.astype(BF16).reshape(depth * n_exp, -1, d)

    for i in range(depth):
        mu_p = _pad_cols(rwkv_mu[i], n_a_p)
        g_up_p = jnp.pad(rwkv_g_up[i], ((0, n_a_p - n_a), (0, 0)))
        qu = mla_q_up[i].reshape(cq, C_HEADS, C_NOPE_DIM + C_ROPE_DIM)
        qu = jnp.pad(qu, ((0, 0), (0, 0), (0, hw - C_NOPE_DIM - C_ROPE_DIM))).reshape(cq, C_HEADS * hw).astype(BF16)
        kvu = mla_kv_up[i].reshape(ckv, C_HEADS, C_NOPE_DIM + C_V_DIM)
        kvu = jnp.concatenate([kvu[:, :, :C_NOPE_DIM].reshape(ckv, -1), kvu[:, :, C_NOPE_DIM:].reshape(ckv, -1)],
                              axis=1).astype(BF16)
        rope_norm_p = _pad_cols(mla_rope_norm[i], LANES)

        mod = _adaln(cond8, ada_down[i], ada_up[i], ada_bias[i])[:2].reshape(2, N_MOD, d)

        h = _norm_mod(x_all, norm_mix[i], mod[:, 0], mod[:, 1], nu)
        w_in_i = functools.partial(_matmul, h, w_pad, w_row0=i * d, k=d)
        za = w_in_i(F32, n=n_a_p, name="w_in_a")
        zb = w_in_i(F32, w_col0=n_a_p, n=n_b, name="w_in_b")
        zc = w_in_i(F32, w_col0=n_a_p + n_b, n=n_c_p, name="w_in_c")
        gates = w_in_i(BF16, w_col0=n_a_p + n_b + n_c_p, n=N_BRANCH * d, act="sigmoid", name="w_in_gate")

        r, k, v, e, a, g = _rwkv_feat(za, mu_p, rwkv_w0[i], rwkv_w_up[i], rwkv_a0[i], rwkv_a_up[i], g_up_p, nu)
        ys = []
        for direction in (0, 1):
            p_, q_, r2, y0 = _rwkv_chunks(r, k, v, e, a, rwkv_k_k[i], rwkv_k_a[i], direction)
            ys.append(_rwkv_seq(p_, q_, r2, y0, direction, nu))
        ya = _rwkv_out(ys[0], ys[1], r, k, v, a, g, rwkv_lnx_w[i], rwkv_lnx_b[i], rwkv_r_k[i], rwkv_k_a[i])

        qkv_b = _gqa_prep(zb, cos, sin, gqa_q_norm[i], gqa_k_norm[i])
        yb = _gqa_attn(qkv_b, gqa_sink[i], nu, seq)

        assert cq % ckv == 0
        q_c = _matmul(zc, qu, F32, rms_gain=mla_q_a_norm[i], name="mla_q_up")
        kv_c = _matmul(zc, kvu, F32, a_col_block=cq // ckv, rms_gain=mla_kv_a_norm[i], name="mla_kv_up")
        qf, kf, vf = _mla_prep(q_c, kv_c, zc, (cq + ckv) // LANES, cos, sin, mla_nope_norm[i], rope_norm_p)
        yc = _mla_flash(qf, kf, vf, nu)

        merged = _merge(ya, yb, yc, w_branch_b, i, gates)
        x_all = _matmul(merged, w_out_b, F32, w_row0=i * d, k=d, resid=x_all, gate2=mod[:, 2], nu=nu, name="w_out")

        hf, aff = _norm_mod(x_all, norm_ffn[i], mod[:, 3], mod[:, 4], nu, router_t=moe_router[i].T)
        last = i == depth - 1
        x_all = _moe_stream(x_all, hf, aff, nu, seq, mod[1:2, 5], wg, wu, wd, i, rows_only=last)
        if not last:
            x_all = _moe_stream(x_all, hf, aff, 0, nu, mod[0:1, 5], wg, wu, wd, i)
    return x_all.reshape(bsz, seq, d)
```

```python
import functools
import math

import jax
import jax.numpy as jnp
from jax import lax
from jax.experimental import pallas as pl
from jax.experimental.pallas import tpu as pltpu

F32 = jnp.float32
BF16 = jnp.bfloat16
I32 = jnp.int32

GRID_W = 64
ROPE_DIM = 64
ROPE_THETA = 10000.0
NORM_EPS = 1e-6
NEG_INF = -1e30
N_MOD = 6
A_HEADS = 16
A_HEAD_DIM = 64
A_LNX_EPS = 64e-5
B_Q_HEADS = 16
B_KV_HEADS = 4
B_HEAD_DIM = 64
WINDOW = 128
BLOCK = 128
C_HEADS = 8
C_NOPE_DIM = 128
C_ROPE_DIM = 64
C_V_DIM = 128
N_BRANCH = 3
CAPACITY_FACTOR = 2

LANES = 128
CHUNK = 64
SEQ_GROUP = 4
ROW_TILE = 256
MOE_TILE = 128
MOE_ALIGN = 16
MOE_WIN = MOE_TILE + MOE_ALIGN
SEL_ROWS = 128
VMEM_MB = 56


def _cp(n_grid, vmem_mb=VMEM_MB):
    return pltpu.CompilerParams(dimension_semantics=("arbitrary",) * n_grid,
                                vmem_limit_bytes=vmem_mb * 1024 * 1024)


def _pick(n, cands):
    for c in cands:
        if n % c == 0:
            return c
    raise ValueError(f"no tile for {n} in {cands}")


def _dg(a, b, ca=1, cb=0):
    return lax.dot_general(a, b, (((ca,), (cb,)), ((), ())), preferred_element_type=F32)


def _split2(x):
    hi = x.astype(BF16)
    lo = (x - hi.astype(F32)).astype(BF16)
    return hi, lo


def _mm(a, b, passes=1, nt=False):
    cb = 1 if nt else 0
    if passes == 1:
        return _dg(a.astype(BF16), b.astype(BF16), 1, cb)
    ah, al = _split2(a)
    bh, bl = _split2(b)
    return _dg(ah, bh, 1, cb) + (_dg(ah, bl, 1, cb) + _dg(al, bh, 1, cb))


def _mm_exact_rhs(x, m_bf16):
    x1 = x.astype(BF16)
    r1 = x - x1.astype(F32)
    x2 = r1.astype(BF16)
    x3 = (r1 - x2.astype(F32)).astype(BF16)
    return _dg(x1, m_bf16) + (_dg(x2, m_bf16) + _dg(x3, m_bf16))


def _sigmoid(x):
    return 1.0 / (1.0 + jnp.exp(-x))


def _iota(shape, dim):
    return lax.broadcasted_iota(I32, shape, dim)


def _block_ones(n, blk):
    i = _iota((n, n), 0) // blk
    j = _iota((n, n), 1) // blk
    return jnp.where(i == j, 1.0, 0.0).astype(BF16)


def _seg_sum(x, ones_bf16):
    return _mm_exact_rhs(x, ones_bf16)


def _swap16(x):
    n = x.shape[-1]
    lane = _iota(x.shape, x.ndim - 1)
    fwd = pltpu.roll(x, n - 16, x.ndim - 1)
    bwd = pltpu.roll(x, 16, x.ndim - 1)
    return jnp.where((lane % 32) < 16, fwd, bwd)


def _adaln_kernel(c_ref, dn_ref, up_ref, b_ref, o_ref):
    c = c_ref[...]
    t = _mm(c * _sigmoid(c), dn_ref[...], 3)
    o_ref[...] = _mm(t, up_ref[...], 3) + b_ref[...]


def _adaln(cond8, down, up, bias):
    d, r = down.shape
    n = up.shape[1]
    tn = _pick(n, (4096, 2048, 1024, 512, 256, 128))
    return pl.pallas_call(
        _adaln_kernel,
        grid=(n // tn,),
        in_specs=[pl.BlockSpec((8, d), lambda j: (0, 0)),
                  pl.BlockSpec((d, r), lambda j: (0, 0)),
                  pl.BlockSpec((r, tn), lambda j: (0, j)),
                  pl.BlockSpec((1, tn), lambda j: (0, j))],
        out_specs=pl.BlockSpec((8, tn), lambda j: (0, j)),
        out_shape=jax.ShapeDtypeStruct((8, n), F32),
        compiler_params=_cp(1),
        name="adaln",
    )(cond8, down, up, bias.reshape(1, n))


def _moe_dcol(d):
    return min(d, 1024)


def _rows_are_ctx(tile_rows, row0, nu):
    return (row0 + _iota((tile_rows, 1), 0)) < nu


def _norm_mod_kernel(x_ref, g_ref, sh_ref, sc_ref, *rest, nu, tm, router):
    if router:
        wr_ref, h_ref, aff_ref = rest
    else:
        (h_ref,) = rest
    x = x_ref[...]
    is_u = pl.program_id(0) * tm < nu
    y = x * lax.rsqrt(jnp.mean(x * x, axis=-1, keepdims=True) + NORM_EPS) * g_ref[...]
    sh = jnp.where(is_u, sh_ref[0:1, :], sh_ref[1:2, :])
    sc = jnp.where(is_u, sc_ref[0:1, :], sc_ref[1:2, :])
    h = y * (1.0 + sc) + sh
    if router:
        dcol = h_ref.shape[2]
        for c in range(h_ref.shape[0]):
            h_ref[c] = h[:, c * dcol:(c + 1) * dcol].astype(h_ref.dtype)
    else:
        h_ref[...] = h.astype(h_ref.dtype)
    if router:
        logits = _mm(wr_ref[...], h, 3, nt=True)
        m = jnp.max(logits, axis=0, keepdims=True)
        p = jnp.exp(logits - m)
        aff_ref[...] = p / jnp.sum(p, axis=0, keepdims=True)


def _norm_mod(x_all, gain, shift2, scale2, nu, router_t=None):
    t, d = x_all.shape
    tm = ROW_TILE
    assert nu % tm == 0
    router = router_t is not None
    in_specs = [pl.BlockSpec((tm, d), lambda i: (i, 0)),
                pl.BlockSpec((1, d), lambda i: (0, 0)),
                pl.BlockSpec((2, d), lambda i: (0, 0)),
                pl.BlockSpec((2, d), lambda i: (0, 0))]
    args = [x_all, gain.reshape(1, d), shift2, scale2]
    out_specs = pl.BlockSpec((tm, d), lambda i: (i, 0))
    out_shape = jax.ShapeDtypeStruct((t, d), BF16)
    if router:
        e = router_t.shape[0]
        dcol = _moe_dcol(d)
        in_specs.append(pl.BlockSpec((e, d), lambda i: (0, 0)))
        args.append(router_t)
        out_specs = (pl.BlockSpec((d // dcol, tm, dcol), lambda i: (0, i, 0)), pl.BlockSpec((e, tm), lambda i: (0, i)))
        out_shape = (jax.ShapeDtypeStruct((d // dcol, t, dcol), BF16), jax.ShapeDtypeStruct((e, t), F32))
    return pl.pallas_call(
        functools.partial(_norm_mod_kernel, nu=nu, tm=tm, router=router),
        grid=(t // tm,), in_specs=in_specs, out_specs=out_specs, out_shape=out_shape,
        compiler_params=_cp(1), name="norm_mod_router" if router else "norm_mod",
    )(*args)


def _matmul_kernel(*refs, nu, tm, act, has_rms, has_resid):
    it = iter(refs)
    a_ref = next(it)
    w_ref = next(it)
    g_ref = next(it) if has_rms else None
    x_ref = next(it) if has_resid else None
    gate_ref = next(it) if has_resid else None
    o_ref = next(it)
    a = a_ref[...]
    if has_rms:
        af = a.astype(F32)
        a = af * lax.rsqrt(jnp.mean(af * af, axis=-1, keepdims=True) + NORM_EPS) * g_ref[...]
    acc = _dg(a.astype(BF16), w_ref[...])
    if act == "sigmoid":
        acc = _sigmoid(acc)
    if has_resid:
        is_u = _rows_are_ctx(tm, pl.program_id(1) * tm, nu)
        gate = jnp.where(is_u, gate_ref[0:1, :], gate_ref[1:2, :])
        acc = x_ref[...] + gate * acc
    o_ref[...] = acc.astype(o_ref.dtype)


def _matmul(a, w, out_dtype, *, a_col_block=0, w_row0=0, k=None, w_col0=0, n=None, act=None, rms_gain=None,
            resid=None, gate2=None, nu=0, name="matmul"):
    m = a.shape[0]
    k = w.shape[0] if k is None else k
    n = w.shape[1] if n is None else n
    tm = _pick(m, (640, 256, 128))
    tn = n if n <= 2048 else _pick(n, (1024, 768, 512, 384, 256, 128))
    has_rms = rms_gain is not None
    has_resid = resid is not None
    in_specs = [pl.BlockSpec((tm, k), lambda j, i: (i, a_col_block)),
                pl.BlockSpec((pl.Element(k), pl.Element(tn)),
                             lambda j, i: (w_row0, pl.multiple_of(w_col0 + j * tn, LANES)))]
    args = [a, w]
    if has_rms:
        in_specs.append(pl.BlockSpec((1, k), lambda j, i: (0, 0)))
        args.append(rms_gain.reshape(1, k))
    if has_resid:
        in_specs += [pl.BlockSpec((tm, tn), lambda j, i: (i, j)),
                     pl.BlockSpec((2, tn), lambda j, i: (0, j))]
        args += [resid, gate2]
    return pl.pallas_call(
        functools.partial(_matmul_kernel, nu=nu, tm=tm, act=act, has_rms=has_rms, has_resid=has_resid),
        grid=(n // tn, m // tm), in_specs=in_specs,
        out_specs=pl.BlockSpec((tm, tn), lambda j, i: (i, j)),
        out_shape=jax.ShapeDtypeStruct((m, n), out_dtype),
        compiler_params=_cp(2), name=name,
    )(*args)


def _rwkv_feat_kernel(z_ref, zp_ref, zn_ref, mu_ref, w0_ref, wup_ref, a0_ref, aup_ref, gup_ref,
                      r_ref, k_ref, v_ref, e_ref, a_ref, g_ref, *, nu, t_all, tm, aw, dr):
    z = z_ref[...]
    i = pl.program_id(0)
    ri = _iota((tm, 1), 0)
    first = (i == 0) | (i == nu // tm)
    last = (i == nu // tm - 1) | (i == t_all // tm - 1)
    prev_row = jnp.where(first, 0.0, zp_ref[7:8, :])
    next_row = jnp.where(last, 0.0, zn_ref[0:1, :])
    zp = jnp.where(ri == 0, prev_row, pltpu.roll(z, 1, 0))
    zn = jnp.where(ri == tm - 1, next_row, pltpu.roll(z, tm - 1, 0))
    zs = z + mu_ref[0:1, :] * (zp - z) + mu_ref[1:2, :] * (zn - z)
    r_ref[...] = zs[:, 0:aw]
    k_ref[...] = zs[:, aw:2 * aw]
    v_ref[...] = zs[:, 2 * aw:3 * aw]
    o = 3 * aw
    for n in range(2):
        wd = jnp.tanh(zs[:, o + n * dr:o + (n + 1) * dr])
        w = w0_ref[n:n + 1, :] + _mm(wd, wup_ref[n])
        sp = jnp.maximum(-w, 0.0) + jnp.log(1.0 + jnp.exp(-jnp.abs(w)))
        e_ref[n] = jnp.exp(-sp - 0.5)
    o += 2 * dr
    for n in range(2):
        ad = zs[:, o + n * dr:o + (n + 1) * dr]
        a_ref[n] = _sigmoid(a0_ref[n:n + 1, :] + _mm(ad, aup_ref[n]))
    o += 2 * dr
    g_ref[...] = _mm(_sigmoid(zs[:, o:]), gup_ref[...])


def _rwkv_feat(za, mu_p, w0, w_up, a0, a_up, g_up_p, nu):
    t, na = za.shape
    aw = w0.shape[1]
    dr = w_up.shape[1]
    tm = ROW_TILE
    assert nu % tm == 0
    nb8 = tm // 8
    last8 = t // 8 - 1
    row_spec = lambda w: pl.BlockSpec((tm, w), lambda i: (i, 0))
    full = lambda a: pl.BlockSpec(a.shape, lambda i: (0,) * a.ndim)
    out_rows = jax.ShapeDtypeStruct((t, aw), F32)
    out_dir = jax.ShapeDtypeStruct((2, t, aw), F32)
    dir_spec = pl.BlockSpec((2, tm, aw), lambda i: (0, i, 0))
    return pl.pallas_call(
        functools.partial(_rwkv_feat_kernel, nu=nu, t_all=t, tm=tm, aw=aw, dr=dr),
        grid=(t // tm,),
        in_specs=[row_spec(na),
                  pl.BlockSpec((8, na), lambda i: (jnp.maximum(i * nb8 - 1, 0), 0)),
                  pl.BlockSpec((8, na), lambda i: (jnp.minimum((i + 1) * nb8, last8), 0)),
                  full(mu_p), full(w0), full(w_up), full(a0), full(a_up), full(g_up_p)],
        out_specs=(row_spec(aw), row_spec(aw), row_spec(aw), dir_spec, dir_spec, row_spec(aw)),
        out_shape=(out_rows, out_rows, out_rows, out_dir, out_dir, out_rows),
        compiler_params=_cp(1), name="rwkv_feat",
    )(za, za, za, mu_p, w0, w_up, a0, a_up, g_up_p)


def _bmm(a, b, nt=False):
    cb = 2 if nt else 1
    return lax.dot_general(a.astype(BF16), b.astype(BF16), (((2,), (cb,)), ((0,), (0,))),
                           preferred_element_type=F32)


def _bmm_exact_lhs(m_bf16, x):
    dn = (((2,), (1,)), ((0,), (0,)))
    x1 = x.astype(BF16)
    r1 = x - x1.astype(F32)
    x2 = r1.astype(BF16)
    x3 = (r1 - x2.astype(F32)).astype(BF16)
    d = lambda y: lax.dot_general(m_bf16, y, dn, preferred_element_type=F32)
    return d(x1) + (d(x2) + d(x3))


def _bt(x):
    return jnp.stack([x[g].T for g in range(x.shape[0])], axis=0)


def _rwkv_chunk_math(r, k, v, e, a, kk_gain, ka_gain, reverse, groups):
    c = CHUNK
    hd = A_HEAD_DIM
    lane = _iota((1, LANES), 1)
    m_a = jnp.where(lane < hd, 1.0, 0.0)
    m_b = 1.0 - m_a
    ones_seg = _block_ones(LANES, hd)
    ti = _iota((groups, c, c), 1)
    tj = _iota((groups, c, c), 2)
    tri = jnp.where((tj >= ti) if reverse else (tj <= ti), 1.0, 0.0).astype(BF16)

    kk0 = k * kk_gain
    kk = kk0 / jnp.maximum(jnp.sqrt(_seg_sum(kk0 * kk0, ones_seg)), 1e-12)
    kt = k * (1.0 + (a - 1.0) * ka_gain)
    b = kk * a
    g3 = lambda x: x.reshape(groups, c, LANES)
    e3 = g3(e)
    cl = _bmm_exact_lhs(tri, e3)
    last = 0 if reverse else c - 1
    ctot = cl[:, last:last + 1, :]
    g_in = jnp.exp(-cl)
    g_ex = jnp.exp(e3 - cl)
    g_inv = jnp.exp(cl)
    g_end = jnp.exp(cl - ctot)
    st = lambda x: jnp.concatenate([x * m_a, x * m_b], axis=1)
    kk3, b3, kt3 = g3(kk), g3(b), g3(kt)
    kk2 = st(kk3 * g_ex)
    r2 = st(g3(r) * g_in)
    b2 = st(b3 * g_inv)
    k2 = st(kt3 * g_inv)
    v2 = st(g3(v))
    bg2 = st(b3 * g_end)
    kg2 = st(kt3 * g_end)

    s = _bmm(jnp.concatenate([kk2, r2], axis=1), jnp.concatenate([b2, k2], axis=1), nt=True)
    n2 = 2 * c
    i2 = _iota((n2, n2), 0)
    j2 = _iota((n2, n2), 1)
    il = i2 % c
    jl = j2 % c
    strict = (jl > il) if reverse else (jl < il)
    incl = (jl >= il) if reverse else (jl <= il)
    a_b = jnp.where(strict, s[:, :n2, :n2], 0.0)
    a_k = jnp.where(strict, s[:, :n2, n2:], 0.0)
    l_b = jnp.where(incl, s[:, n2:, :n2], 0.0)
    l_k = jnp.where(incl, s[:, n2:, n2:], 0.0)
    eye = jnp.where(i2 == j2, 1.0, 0.0)

    same = lambda m: (i2 // m) == (j2 // m)
    a_d = jnp.where(same(8), a_b, 0.0)
    a_d2 = _bmm(a_d, a_d)
    a_d4 = _bmm(a_d2, a_d2)
    t_inv = _bmm(_bmm(eye - a_d, eye + a_d2), eye + a_d4)
    m = 8
    while m < c:
        a_off = jnp.where(same(2 * m) & jnp.logical_not(same(m)), a_b, 0.0)
        t_inv = t_inv - _bmm(_bmm(t_inv, a_off), t_inv)
        m *= 2

    av = _bmm(a_k, v2)
    z12 = _bmm(t_inv, jnp.concatenate([kk2, av], axis=2))
    z1 = z12[:, :, :LANES]
    z2 = z12[:, :, LANES:]
    vz = jnp.concatenate([v2, z2], axis=1)
    p_bd = eye * jnp.exp(-ctot) - _bmm(_bt(bg2), z1)
    q_bd = _bmm(_bt(jnp.concatenate([kg2, -bg2], axis=1)), vz)
    r2s = r2 - _bmm(l_b, z1)
    y0s = _bmm(jnp.concatenate([l_k, -l_b], axis=2), vz)
    cp = lambda x: x[:, :x.shape[1] // 2] + x[:, x.shape[1] // 2:]
    return cp(p_bd), cp(q_bd), cp(r2s).reshape(groups * c, LANES), cp(y0s).reshape(groups * c, LANES)


def _rwkv_chunk_kernel(r_ref, k_ref, v_ref, e_ref, a_ref, kkg_ref, kag_ref,
                       p_ref, q_ref, r2_ref, y0_ref, *, reverse, groups):
    p, q, r2, y0 = _rwkv_chunk_math(r_ref[...], k_ref[...], v_ref[...], e_ref[0], a_ref[0],
                                    kkg_ref[...], kag_ref[...], reverse, groups)
    p_ref[0] = p
    q_ref[0] = q
    r2_ref[...] = r2
    y0_ref[...] = y0


def _rwkv_chunks(r, k, v, e, a, k_k, k_a, direction):
    t, aw = r.shape
    npair = aw // LANES
    nc = t // CHUNK
    groups = _pick(nc, (20, 10, 4))
    rows = groups * CHUNK
    row_spec = pl.BlockSpec((rows, LANES), lambda p, i: (i, p))
    dir_spec = pl.BlockSpec((1, rows, LANES), lambda p, i: (direction, i, p))
    par_spec = pl.BlockSpec((1, LANES), lambda p, i: (0, p))
    pq_spec = pl.BlockSpec((1, groups, A_HEAD_DIM, LANES), lambda p, i: (p, i, 0, 0))
    pq_shape = jax.ShapeDtypeStruct((npair, nc, A_HEAD_DIM, LANES), F32)
    ry_shape = jax.ShapeDtypeStruct((t, aw), F32)
    return pl.pallas_call(
        functools.partial(_rwkv_chunk_kernel, reverse=bool(direction), groups=groups),
        grid=(npair, t // rows),
        in_specs=[row_spec, row_spec, row_spec, dir_spec, dir_spec, par_spec, par_spec],
        out_specs=(pq_spec, pq_spec, row_spec, row_spec),
        out_shape=(pq_shape, pq_shape, ry_shape, ry_shape),
        compiler_params=_cp(2), name="rwkv_chunks_bwd" if direction else "rwkv_chunks_fwd",
    )(r, k, v, e, a, k_k.reshape(1, aw), k_a.reshape(1, aw))


def _pair_block_diag(x):
    lane = _iota(x.shape, 1)
    return jnp.concatenate([jnp.where(lane < A_HEAD_DIM, x, 0.0), jnp.where(lane >= A_HEAD_DIM, x, 0.0)], axis=0)


def _rwkv_seq_kernel(p_ref, q_ref, r2_ref, y0_ref, y_ref, h_ref, *, reverse, groups, npair):
    c = CHUNK

    @pl.when(pl.program_id(0) == 0)
    def _():
        h_ref[...] = jnp.zeros_like(h_ref)

    order = range(groups - 1, -1, -1) if reverse else range(groups)
    for g in order:
        sl = slice(g * c, (g + 1) * c)
        for p in range(npair):
            ls = slice(p * LANES, (p + 1) * LANES)
            h = h_ref[p]
            y_ref[sl, ls] = _mm(r2_ref[sl, ls], h) + y0_ref[sl, ls]
            h_ref[p] = _mm(_pair_block_diag(p_ref[p, g]), h) + _pair_block_diag(q_ref[p, g])


def _rwkv_seq(p, q, r2, y0, direction, nu):
    npair, nc = p.shape[:2]
    t, aw = r2.shape
    groups = SEQ_GROUP
    rows = groups * CHUNK
    nb = t // rows
    nbu = nu // rows
    assert nu % rows == 0
    if direction:
        blk = lambda s: jnp.where(s < nbu, nbu - 1 - s, nb + nbu - 1 - s)
    else:
        blk = lambda s: s
    pq_spec = pl.BlockSpec((npair, groups, A_HEAD_DIM, LANES), lambda s: (0, blk(s), 0, 0))
    row_spec = pl.BlockSpec((rows, aw), lambda s: (blk(s), 0))
    return pl.pallas_call(
        functools.partial(_rwkv_seq_kernel, reverse=bool(direction), groups=groups, npair=npair),
        grid=(nb,),
        in_specs=[pq_spec, pq_spec, row_spec, row_spec],
        out_specs=row_spec,
        out_shape=jax.ShapeDtypeStruct((t, aw), F32),
        scratch_shapes=[pltpu.VMEM((npair, LANES, LANES), F32)],
        compiler_params=_cp(1), name="rwkv_seq_bwd" if direction else "rwkv_seq_fwd",
    )(p, q, r2, y0)


def _rwkv_out_kernel(yf_ref, yb_ref, r_ref, k_ref, v_ref, a_ref, g_ref, lw_ref, lb_ref, rk_ref, ka_ref, o_ref, *, aw):
    hd = A_HEAD_DIM
    ones_seg = _block_ones(LANES, hd)
    for p in range(aw // LANES):
        ls = slice(p * LANES, (p + 1) * LANES)
        y = yf_ref[:, ls] + yb_ref[:, ls]
        mean = _seg_sum(y, ones_seg) * (1.0 / hd)
        yc = y - mean
        var = _seg_sum(yc * yc, ones_seg) * (1.0 / hd)
        yn = yc * lax.rsqrt(var + A_LNX_EPS) * lw_ref[:, ls] + lb_ref[:, ls]
        r = r_ref[:, ls]
        k = k_ref[:, ls]
        v = v_ref[:, ls]
        bonus = jnp.zeros_like(y)
        for n in range(2):
            kt = k * (1.0 + (a_ref[n, :, ls] - 1.0) * ka_ref[:, ls])
            bonus = bonus + _seg_sum(r * kt * rk_ref[:, ls], ones_seg) * v
        o_ref[:, ls] = ((yn + bonus) * g_ref[:, ls]).astype(o_ref.dtype)


def _rwkv_out(yf, yb, r, k, v, a, g, lnx_w, lnx_b, r_k, k_a):
    t, aw = r.shape
    tm = ROW_TILE
    row_spec = pl.BlockSpec((tm, aw), lambda i: (i, 0))
    par_spec = pl.BlockSpec((1, aw), lambda i: (0, 0))
    return pl.pallas_call(
        functools.partial(_rwkv_out_kernel, aw=aw),
        grid=(t // tm,),
        in_specs=[row_spec] * 5 + [pl.BlockSpec((2, tm, aw), lambda i: (0, i, 0)), row_spec] + [par_spec] * 4,
        out_specs=row_spec,
        out_shape=jax.ShapeDtypeStruct((t, aw), BF16),
        compiler_params=_cp(1), name="rwkv_out",
    )(yf, yb, r, k, v, a, g, lnx_w.reshape(1, aw), lnx_b.reshape(1, aw), r_k.reshape(1, aw), k_a.reshape(1, aw))


def _rms_rope_slab(x, gain, cos, sin, ones_seg):
    ms = _seg_sum(x * x, ones_seg) * (1.0 / B_HEAD_DIM)
    y = x * lax.rsqrt(ms + NORM_EPS) * gain
    return y * cos + _swap16(y) * sin


def _gqa_prep_kernel(z_ref, cos_ref, sin_ref, qg_ref, kg_ref, o_ref, *, qw, kw):
    ones_seg = _block_ones(LANES, B_HEAD_DIM)
    cos = cos_ref[...]
    sin = sin_ref[...]
    scale = B_HEAD_DIM ** -0.5 * math.log2(math.e)
    for s in range(qw // LANES):
        ls = slice(s * LANES, (s + 1) * LANES)
        o_ref[:, ls] = (_rms_rope_slab(z_ref[:, ls], qg_ref[...], cos, sin, ones_seg) * scale).astype(o_ref.dtype)
    for s in range(kw // LANES):
        ls = slice(qw + s * LANES, qw + (s + 1) * LANES)
        o_ref[:, ls] = _rms_rope_slab(z_ref[:, ls], kg_ref[...], cos, sin, ones_seg).astype(o_ref.dtype)
    o_ref[:, qw + kw:] = z_ref[:, qw + kw:].astype(o_ref.dtype)


def _gqa_prep(zb, cos, sin, q_norm, k_norm):
    t, nb = zb.shape
    tm = ROW_TILE
    qw = B_Q_HEADS * B_HEAD_DIM
    kw = B_KV_HEADS * B_HEAD_DIM
    tile2 = lambda g: jnp.tile(g.reshape(1, B_HEAD_DIM), (1, LANES // B_HEAD_DIM))
    return pl.pallas_call(
        functools.partial(_gqa_prep_kernel, qw=qw, kw=kw),
        grid=(t // tm,),
        in_specs=[pl.BlockSpec((tm, nb), lambda i: (i, 0)),
                  pl.BlockSpec((tm, LANES), lambda i: (i, 0)),
                  pl.BlockSpec((tm, LANES), lambda i: (i, 0)),
                  pl.BlockSpec((1, LANES), lambda i: (0, 0)),
                  pl.BlockSpec((1, LANES), lambda i: (0, 0))],
        out_specs=pl.BlockSpec((tm, nb), lambda i: (i, 0)),
        out_shape=jax.ShapeDtypeStruct((t, nb), BF16),
        compiler_params=_cp(1), name="gqa_prep",
    )(zb, cos, sin, tile2(q_norm), tile2(k_norm))


def _gqa_attn_kernel(sink_ref, q_ref, c_ref, kp_ref, ko_ref, kn_ref, o_ref, *, nu, seq, qw, kw):
    hd = B_HEAD_DIM
    grp = B_Q_HEADS // B_KV_HEADS
    blk = BLOCK
    j = pl.program_id(0)
    jb = j - nu // blk
    nkeys = nu + 3 * blk
    rows = grp * blk
    qi = _iota((rows, nkeys), 0) % blk
    kc = _iota((rows, nkeys), 1)
    q_pos = jb * blk + qi
    k_pos = (jb - 1) * blk + (kc - nu)
    band_ok = (jnp.abs(k_pos - q_pos) <= WINDOW) & (k_pos >= 0) & (k_pos < seq) & (jb >= 0)
    valid = (kc < nu) | band_ok
    rg = _iota((rows, 1), 0) // blk
    outs = []
    for h in range(B_KV_HEADS):
        ks = slice(qw + h * hd, qw + (h + 1) * hd)
        vs = slice(qw + kw + h * hd, qw + kw + (h + 1) * hd)
        k_all = jnp.concatenate([c_ref[:, ks], kp_ref[:, ks], ko_ref[:, ks], kn_ref[:, ks]], axis=0)
        v_all = jnp.concatenate([c_ref[:, vs], kp_ref[:, vs], ko_ref[:, vs], kn_ref[:, vs]], axis=0)
        q4 = jnp.concatenate([q_ref[:, (h * grp + g) * hd:(h * grp + g + 1) * hd] for g in range(grp)], axis=0)
        s = jnp.where(valid, _dg(q4, k_all, 1, 1), NEG_INF)
        sink = jnp.zeros((rows, 1), F32)
        for g in range(grp):
            sink = jnp.where(rg == g, sink_ref[h * grp + g] * math.log2(math.e), sink)
        m = jnp.maximum(jnp.max(s, axis=1, keepdims=True), sink)
        p = jnp.exp2((s - m).astype(BF16))
        pv = _dg(p, jnp.concatenate([v_all, jnp.ones_like(v_all)], axis=1))
        o = pv[:, :hd] / (pv[:, hd:hd + 1] + jnp.exp2(sink - m))
        outs += [o[g * blk:(g + 1) * blk, :] for g in range(grp)]
    o_ref[...] = jnp.concatenate(outs, axis=1).astype(o_ref.dtype)


def _gqa_attn(qkv, sink, nu, seq):
    t, nb = qkv.shape
    qw = B_Q_HEADS * B_HEAD_DIM
    kw = B_KV_HEADS * B_HEAD_DIM
    blk = BLOCK
    nblk = t // blk
    band = lambda off: pl.BlockSpec((blk, nb), lambda j: (jnp.clip(j + off, 0, nblk - 1), 0))
    return pl.pallas_call(
        functools.partial(_gqa_attn_kernel, nu=nu, seq=seq, qw=qw, kw=kw),
        grid=(nblk,),
        in_specs=[pl.BlockSpec(memory_space=pltpu.SMEM),
                  pl.BlockSpec((blk, nb), lambda j: (j, 0)),
                  pl.BlockSpec((nu, nb), lambda j: (0, 0)),
                  band(-1), band(0), band(1)],
        out_specs=pl.BlockSpec((blk, qw), lambda j: (j, 0)),
        out_shape=jax.ShapeDtypeStruct((t, qw), BF16),
        compiler_params=_cp(1), name="gqa_attn",
    )(sink, qkv, qkv, qkv, qkv, qkv)


def _mla_prep_kernel(q_ref, kv_ref, kr_ref, cos_ref, sin_ref, nn_ref, rn_ref, qo_ref, ko_ref, vo_ref):
    cos = cos_ref[...]
    sin = sin_ref[...]
    dn = C_NOPE_DIM
    hw = 2 * LANES
    scale = (C_NOPE_DIM + C_ROPE_DIM) ** -0.5 * math.log2(math.e)
    ones = jnp.ones((q_ref.shape[0], C_V_DIM), vo_ref.dtype)

    def rms_rope(x, gain):
        ms = jnp.sum(x * x, axis=-1, keepdims=True) * (1.0 / C_ROPE_DIM)
        y = x * lax.rsqrt(ms + NORM_EPS) * gain
        return y * cos + _swap16(y) * sin

    def rms(x, gain):
        return x * lax.rsqrt(jnp.mean(x * x, axis=-1, keepdims=True) + NORM_EPS) * gain

    kr = rms_rope(kr_ref[...], rn_ref[1:2, :]).astype(ko_ref.dtype)
    for h in range(C_HEADS):
        qn = rms(q_ref[:, h * hw:h * hw + dn], nn_ref[0:1, :])
        qr = rms_rope(q_ref[:, h * hw + dn:(h + 1) * hw], rn_ref[0:1, :])
        qo_ref[:, h * hw:h * hw + dn] = (qn * scale).astype(qo_ref.dtype)
        qo_ref[:, h * hw + dn:(h + 1) * hw] = (qr * scale).astype(qo_ref.dtype)
        ko_ref[:, h * hw:h * hw + dn] = rms(kv_ref[:, h * dn:(h + 1) * dn], nn_ref[1:2, :]).astype(ko_ref.dtype)
        ko_ref[:, h * hw + dn:(h + 1) * hw] = kr
        vo_ref[:, h * hw:h * hw + C_V_DIM] = kv_ref[:, (C_HEADS + h) * dn:(C_HEADS + h + 1) * dn].astype(vo_ref.dtype)
        vo_ref[:, h * hw + C_V_DIM:(h + 1) * hw] = ones


def _mla_prep(q, kv, zc, kr_col_block, cos, sin, nope_norm, rope_norm_p):
    t = q.shape[0]
    tm = ROW_TILE
    hw = 2 * LANES
    row = lambda w: pl.BlockSpec((tm, w), lambda i: (i, 0))
    full = lambda a: pl.BlockSpec(a.shape, lambda i: (0, 0))
    return pl.pallas_call(
        _mla_prep_kernel,
        grid=(t // tm,),
        in_specs=[row(q.shape[1]), row(kv.shape[1]),
                  pl.BlockSpec((tm, LANES), lambda i: (i, kr_col_block)),
                  row(LANES), row(LANES), full(nope_norm), full(rope_norm_p)],
        out_specs=(row(C_HEADS * hw),) * 3,
        out_shape=(jax.ShapeDtypeStruct((t, C_HEADS * hw), BF16),) * 3,
        compiler_params=_cp(1), name="mla_prep",
    )(q, kv, zc, cos, sin, nope_norm, rope_norm_p)


def _mla_flash_kernel(q_ref, k_ref, v_ref, o_ref, m_ref, acc_ref, *, nu, tq, tk, sub):
    qi = pl.program_id(1)
    ki = pl.program_id(2)

    @pl.when(ki == 0)
    def _():
        m_ref[...] = jnp.full_like(m_ref, NEG_INF)
        acc_ref[...] = jnp.zeros_like(acc_ref)

    def update(masked):
        for qs in range(tq // sub):
            rs = slice(qs * sub, (qs + 1) * sub)
            s = _dg(q_ref[rs, :], k_ref[...], 1, 1)
            if masked and qs * sub < nu:
                qrow = qi * tq + qs * sub + _iota((sub, tk), 0)
                kcol = ki * tk + _iota((sub, tk), 1)
                s = jnp.where((qrow < nu) & (kcol >= nu), NEG_INF, s)
            m_prev = m_ref[rs, :]
            m_next = jnp.maximum(m_prev, jnp.max(s, axis=1, keepdims=True))
            alpha = jnp.exp2(m_prev - m_next)
            p = jnp.exp2((s - m_next[:, 0:1]).astype(BF16))
            acc_ref[rs, :] = jnp.concatenate([alpha, alpha], axis=1) * acc_ref[rs, :] + _dg(p, v_ref[...])
            m_ref[rs, :] = m_next

    has_ctx_rows = qi * tq < nu

    @pl.when(has_ctx_rows)
    def _():
        update(True)

    @pl.when(jnp.logical_not(has_ctx_rows))
    def _():
        update(False)

    @pl.when(ki == pl.num_programs(2) - 1)
    def _():
        o_ref[...] = (acc_ref[:, :C_V_DIM] / acc_ref[:, C_V_DIM:]).astype(o_ref.dtype)


def _mla_flash(qf, kf, vf, nu):
    t = qf.shape[0]
    hw = 2 * LANES
    tq = _pick(t, (3328, 256))
    tk = _pick(t, (3328, 256))
    return pl.pallas_call(
        functools.partial(_mla_flash_kernel, nu=nu, tq=tq, tk=tk, sub=256),
        grid=(C_HEADS, t // tq, t // tk),
        in_specs=[pl.BlockSpec((tq, hw), lambda h, i, j: (i, h)),
                  pl.BlockSpec((tk, hw), lambda h, i, j: (j, h)),
                  pl.BlockSpec((tk, hw), lambda h, i, j: (j, h))],
        out_specs=pl.BlockSpec((tq, C_V_DIM), lambda h, i, j: (i, h)),
        out_shape=jax.ShapeDtypeStruct((t, C_HEADS * C_V_DIM), BF16),
        scratch_shapes=[pltpu.VMEM((tq, LANES), F32), pltpu.VMEM((tq, hw), F32)],
        compiler_params=_cp(3), name="mla_flash",
    )(qf, kf, vf)


def _merge_kernel(ya_ref, yb_ref, yc_ref, w_ref, g0_ref, g1_ref, g2_ref, o_ref):
    acc = g0_ref[...].astype(F32) * _dg(ya_ref[...], w_ref[0])
    acc = acc + g1_ref[...].astype(F32) * _dg(yb_ref[...], w_ref[1])
    acc = acc + g2_ref[...].astype(F32) * _dg(yc_ref[...], w_ref[2])
    o_ref[...] = acc.astype(o_ref.dtype)


def _merge(ya, yb, yc, w_branch, layer, gates):
    t, bw = ya.shape
    d = w_branch.shape[2]
    tm = _pick(t, (640, 256, 128))
    tn = _pick(d, (1024, 512, 256, 128))
    nj = d // tn
    y_spec = pl.BlockSpec((tm, bw), lambda j, i: (i, 0))
    gate = lambda b: pl.BlockSpec((tm, tn), lambda j, i: (i, b * nj + j))
    return pl.pallas_call(
        _merge_kernel,
        grid=(nj, t // tm),
        in_specs=[y_spec, y_spec, y_spec, pl.BlockSpec((N_BRANCH, bw, tn), lambda j, i: (layer, 0, j)),
                  gate(0), gate(1), gate(2)],
        out_specs=pl.BlockSpec((tm, tn), lambda j, i: (i, j)),
        out_shape=jax.ShapeDtypeStruct((t, d), BF16),
        compiler_params=_cp(2), name="merge",
    )(ya, yb, yc, w_branch, gates, gates, gates)


def _moe_select_kernel(aff_ref, loc_ref, base_ref, *, cap, n_exp):
    aff = aff_ref[...]
    bits = pltpu.bitcast(aff, I32)
    count = lambda m: jnp.sum(jnp.sum(jnp.where(m, 1.0, 0.0), axis=1, keepdims=True), axis=2, keepdims=True)
    theta = jnp.zeros((n_exp, 1, 1), I32)
    for bit in range(30, -1, -1):
        cand = theta | (1 << bit)
        theta = jnp.where(count(bits >= cand) >= cap, cand, theta)
    gt = bits > theta
    eq = bits == theta
    need = cap - count(gt)
    r = aff.shape[1]
    upper = jnp.where(_iota((LANES, LANES), 0) <= _iota((LANES, LANES), 1), 1.0, 0.0).astype(BF16)
    lower_strict = jnp.where(_iota((r, r), 1) < _iota((r, r), 0), 1.0, 0.0).astype(BF16)

    def prefix(x):
        incl = _dg(x.astype(BF16), upper)
        tot = jnp.broadcast_to(incl[:, LANES - 1:LANES], (r, LANES))
        return incl, _dg(lower_strict, tot.astype(BF16))

    for e in range(n_exp):
        xe = jnp.where(eq[e], 1.0, 0.0)
        incl, base = prefix(xe)
        take = eq[e] & ((incl - xe + base) < need[e])
        sel = jnp.where(gt[e] | take, 1.0, 0.0)
        incl, base = prefix(sel)
        loc_ref[e] = jnp.where(sel > 0.0, incl - sel, -4096.0)
        base_ref[e] = base


def _moe_select(aff3, cap):
    e, r, _ = aff3.shape
    spec = pl.BlockSpec((e, r, LANES), lambda i: (0, 0, 0))
    shape = jax.ShapeDtypeStruct((e, r, LANES), F32)
    return pl.pallas_call(
        functools.partial(_moe_select_kernel, cap=cap, n_exp=e),
        grid=(1,), in_specs=[spec], out_specs=(spec, spec), out_shape=(shape, shape),
        compiler_params=_cp(1), name="moe_select",
    )(aff3)


def _window_start(base_ref, e, i):
    b = base_ref[e, i]
    a = (b // MOE_ALIGN) * MOE_ALIGN
    return pl.multiple_of(a, MOE_ALIGN), b - a


def _moe_gather_kernel(base_ref, loc_ref, aff_ref, h_ref, xe_ref, ge_ref, *, tiles, group):
    g = pl.program_id(0)
    c = pl.program_id(1)
    i = pl.program_id(2)
    win_row = _iota((MOE_WIN, MOE_TILE), 0).astype(F32)

    def onehot_window(k, r):
        a, off = _window_start(base_ref, g * group + k, i * tiles + r)
        slot = loc_ref[k, r] + off.astype(F32)
        return a, jnp.where(win_row == slot, 1.0, 0.0)

    @pl.when(i == 0)
    def _():
        xe_ref[...] = jnp.zeros_like(xe_ref)

    for k in range(group):
        for r in range(tiles):
            a, onehot = onehot_window(k, r)
            rows = _dg(onehot.astype(BF16), h_ref[r * MOE_TILE:(r + 1) * MOE_TILE, :])
            head = pl.ds(a, MOE_ALIGN)
            xe_ref[k, head, :] = (xe_ref[k, head, :].astype(F32) + rows[:MOE_ALIGN]).astype(xe_ref.dtype)
            xe_ref[k, pl.ds(a + MOE_ALIGN, MOE_TILE), :] = rows[MOE_ALIGN:].astype(xe_ref.dtype)

    @pl.when(c == 0)
    def _():
        @pl.when(i == 0)
        def _():
            ge_ref[...] = jnp.zeros_like(ge_ref)

        for k in range(group):
            for r in range(tiles):
                a, onehot = onehot_window(k, r)
                win = pl.ds(a, MOE_WIN)
                gsel = jnp.sum(onehot * aff_ref[k, r], axis=1, keepdims=True)
                ge_ref[k, win, :] = ge_ref[k, win, :] + gsel


def _moe_gather(base_i, loc, aff3, h_chunks, t, row0, n, cap_p):
    e = loc.shape[0]
    nchunk, _, dcol = h_chunks.shape
    d = nchunk * dcol
    nt = n // MOE_TILE
    tiles = _pick(nt, (8, 4, 2, 1))
    rows = tiles * MOE_TILE
    group = _pick(e, (4, 2, 1))
    sel_spec = pl.BlockSpec((group, tiles, 1, LANES), lambda g, c, i, b: (g, i, 0, 0))
    grid_spec = pltpu.PrefetchScalarGridSpec(
        num_scalar_prefetch=1, grid=(e // group, d // dcol, nt // tiles),
        in_specs=[sel_spec, sel_spec,
                  pl.BlockSpec((pl.Element(rows), pl.Element(dcol)),
                               lambda g, c, i, b: (pl.multiple_of(c * t + row0 + i * rows, MOE_TILE), 0))],
        out_specs=(pl.BlockSpec((group, cap_p, dcol), lambda g, c, i, b: (g, 0, c)),
                   pl.BlockSpec((group, cap_p, LANES), lambda g, c, i, b: (g, 0, 0))))
    return pl.pallas_call(
        functools.partial(_moe_gather_kernel, tiles=tiles, group=group), grid_spec=grid_spec,
        out_shape=(jax.ShapeDtypeStruct((e, cap_p, d), BF16), jax.ShapeDtypeStruct((e, cap_p, LANES), F32)),
        compiler_params=_cp(3), name="moe_gather",
    )(base_i, loc, aff3, h_chunks.reshape(nchunk * t, dcol))


def _moe_ffn_kernel(x_ref, g_ref, wg_ref, wu_ref, wd_ref, y_ref):
    x = x_ref[0]
    hg = _dg(x, wg_ref[0])
    hid = hg * _sigmoid(hg) * _dg(x, wu_ref[0])
    y = _dg(hid.astype(BF16), wd_ref[0]) * g_ref[0][:, 0:1]
    y_ref[0] = y.astype(y_ref.dtype)


def _moe_ffn(xe, ge, w_gate, w_up, w_down, layer, tc):
    e, cap_p, d = xe.shape
    f = w_gate.shape[2]
    return pl.pallas_call(
        _moe_ffn_kernel,
        grid=(e, cap_p // tc),
        in_specs=[pl.BlockSpec((1, tc, d), lambda ee, i: (ee, i, 0)),
                  pl.BlockSpec((1, tc, LANES), lambda ee, i: (ee, i, 0)),
                  pl.BlockSpec((1, d, f), lambda ee, i: (layer * e + ee, 0, 0)),
                  pl.BlockSpec((1, d, f), lambda ee, i: (layer * e + ee, 0, 0)),
                  pl.BlockSpec((1, f, d), lambda ee, i: (layer * e + ee, 0, 0))],
        out_specs=pl.BlockSpec((1, tc, d), lambda ee, i: (ee, i, 0)),
        out_shape=jax.ShapeDtypeStruct((e, cap_p, d), BF16),
        compiler_params=_cp(2), name="moe_ffn",
    )(xe, ge, w_gate, w_up, w_down)


def _moe_combine_kernel(base_ref, loc_ref, *rest, n_exp):
    y_refs = rest[:n_exp]
    x_ref, gate_ref, o_ref = rest[n_exp:]
    i = pl.program_id(0)
    win_row = _iota((MOE_TILE, MOE_WIN), 1).astype(F32)
    acc = None
    for e in range(n_exp):
        _, off = _window_start(base_ref, e, i)
        slot_row = jnp.broadcast_to(loc_ref[e, 0] + off.astype(F32), (MOE_TILE, LANES))
        slot_col = slot_row.T
        slot_col = jnp.concatenate([slot_col, slot_col[:, :MOE_WIN - LANES]], axis=1)
        onehot = jnp.where(win_row == slot_col, 1.0, 0.0).astype(BF16)
        part = _dg(onehot, y_refs[e][...])
        acc = part if acc is None else acc + part
    o_ref[...] = x_ref[...] + gate_ref[...] * acc


def _moe_combine(base_i, loc, ye, x_all, gate_row, row0, n, rows_only):
    e, cap_p, d = ye.shape
    nt = n // MOE_TILE
    t0 = row0 // MOE_TILE
    o0 = 0 if rows_only else t0
    dcol = d

    def y_spec(ee):
        def index(i, j, b):
            start = ee * cap_p + (b[ee, i] // MOE_ALIGN) * MOE_ALIGN
            return pl.multiple_of(start, MOE_ALIGN), pl.multiple_of(j * dcol, LANES)
        return pl.BlockSpec((pl.Element(MOE_WIN), pl.Element(dcol)), index)

    grid_spec = pltpu.PrefetchScalarGridSpec(
        num_scalar_prefetch=1, grid=(nt, d // dcol),
        in_specs=[pl.BlockSpec((e, 1, 1, LANES), lambda i, j, b: (0, i, 0, 0))]
                 + [y_spec(ee) for ee in range(e)]
                 + [pl.BlockSpec((MOE_TILE, dcol), lambda i, j, b: (t0 + i, j)),
                    pl.BlockSpec((1, dcol), lambda i, j, b: (0, j))],
        out_specs=pl.BlockSpec((MOE_TILE, dcol), lambda i, j, b: (o0 + i, j)))
    x_index = 2 + e
    return pl.pallas_call(
        functools.partial(_moe_combine_kernel, n_exp=e), grid_spec=grid_spec,
        out_shape=jax.ShapeDtypeStruct((n, d) if rows_only else x_all.shape, F32),
        input_output_aliases={} if rows_only else {x_index: 0},
        compiler_params=_cp(2), name="moe_combine",
    )(base_i, loc, *([ye.reshape(e * cap_p, d)] * e), x_all, gate_row)


def _moe_stream(x_all, h_chunks, aff, row0, n, gate_row, w_gate, w_up, w_down, layer, rows_only=False):
    e = aff.shape[0]
    cap = (CAPACITY_FACTOR * n) // e
    tc = 256 if cap >= 256 else 64
    cap_p = -(-(cap + MOE_WIN) // tc) * tc
    n_sel = SEL_ROWS * LANES
    aff_s = lax.dynamic_slice_in_dim(aff, row0, n, axis=1)
    aff3 = jnp.pad(aff_s, ((0, 0), (0, n_sel - n))).reshape(e, SEL_ROWS, LANES)
    loc, base = _moe_select(aff3, cap)
    base_i = base[:, :, 0].astype(I32)
    loc4 = loc.reshape(e, SEL_ROWS, 1, LANES)
    aff4 = aff3.reshape(e, SEL_ROWS, 1, LANES)
    xe, ge = _moe_gather(base_i, loc4, aff4, h_chunks, x_all.shape[0], row0, n, cap_p)
    ye = _moe_ffn(xe, ge, w_gate, w_up, w_down, layer, tc)
    return _moe_combine(base_i, loc4, ye, x_all, gate_row, row0, n, rows_only)


def _rope_tables(nu, seq):
    n_freq = ROPE_DIM // 4
    pos = jnp.arange(seq)
    inv = jnp.power(ROPE_THETA, -jnp.arange(n_freq, dtype=F32) / n_freq)
    ang_r = (pos // GRID_W).astype(F32)[:, None] * inv[None]
    ang_c = (pos % GRID_W).astype(F32)[:, None] * inv[None]
    cos = jnp.concatenate([jnp.cos(ang_r)] * 2 + [jnp.cos(ang_c)] * 2, axis=1)
    sin = jnp.concatenate([-jnp.sin(ang_r), jnp.sin(ang_r), -jnp.sin(ang_c), jnp.sin(ang_c)], axis=1)
    cos = jnp.concatenate([jnp.ones((nu, ROPE_DIM), F32), cos], axis=0)
    sin = jnp.concatenate([jnp.zeros((nu, ROPE_DIM), F32), sin], axis=0)
    rep = LANES // ROPE_DIM
    return jnp.tile(cos, (1, rep)), jnp.tile(sin, (1, rep))


def _pad_cols(w, n):
    return jnp.pad(w, ((0, 0), (0, n - w.shape[1])))


def _round_up(x, m):
    return -(-x // m) * m


def kernel(x, c, ctx, c_ctx, norm_mix, norm_ffn, ada_down, ada_up, ada_bias, w_in, rwkv_mu, rwkv_w0, rwkv_w_up, rwkv_a0, rwkv_a_up, rwkv_g_up, rwkv_k_k, rwkv_k_a, rwkv_r_k, rwkv_lnx_w, rwkv_lnx_b, gqa_q_norm, gqa_k_norm, gqa_sink, mla_q_a_norm, mla_q_up, mla_kv_a_norm, mla_kv_up, mla_nope_norm, mla_rope_norm, w_branch, w_out, moe_router, moe_w_gate, moe_w_up, moe_w_down):
    bsz, seq, d = x.shape
    assert bsz == 1
    nu = ctx.shape[1]
    depth = w_in.shape[0]
    aw = A_HEADS * A_HEAD_DIM
    dr = rwkv_w_up.shape[2]
    gr = rwkv_g_up.shape[1]
    n_a = 3 * aw + 4 * dr + gr
    n_a_p = _round_up(n_a, LANES)
    n_b = (B_Q_HEADS + 2 * B_KV_HEADS) * B_HEAD_DIM
    cq = mla_q_up.shape[1]
    ckv = mla_kv_up.shape[1]
    n_c = cq + ckv + C_ROPE_DIM
    n_c_p = cq + ckv + LANES
    hw = 2 * LANES

    x_all = jnp.concatenate([ctx[0], x[0]], axis=0)
    cond8 = jnp.zeros((8, d), F32).at[0].set(c_ctx).at[1].set(c[0])
    cos, sin = _rope_tables(nu, seq)

    zcols = lambda w_: jnp.zeros((depth, d, w_), BF16)
    wb = w_in.astype(BF16)
    w_pad = jnp.concatenate([wb[:, :, :n_a], zcols(n_a_p - n_a), wb[:, :, n_a:n_a + n_b + n_c],
                             zcols(n_c_p - n_c), wb[:, :, n_a + n_b + n_c:]], axis=2).reshape(depth * d, -1)
    n_exp = moe_router.shape[2]
    w_branch_b = w_branch.astype(BF16).reshape(depth * N_BRANCH, aw, d)
    w_out_b = w_out.astype(BF16).reshape(depth * d, d)
    wg = moe_w_gate.astype(BF16).reshape(depth * n_exp, d, -1)
    wu = moe_w_up.astype(BF16).reshape(depth * n_exp, d, -1)
    wd = moe_w_down.astype(BF16).reshape(depth * n_exp, -1, d)

    for i in range(depth):
        mu_p = _pad_cols(rwkv_mu[i], n_a_p)
        g_up_p = jnp.pad(rwkv_g_up[i], ((0, n_a_p - n_a), (0, 0)))
        qu = mla_q_up[i].reshape(cq, C_HEADS, C_NOPE_DIM + C_ROPE_DIM)
        qu = jnp.pad(qu, ((0, 0), (0, 0), (0, hw - C_NOPE_DIM - C_ROPE_DIM))).reshape(cq, C_HEADS * hw).astype(BF16)
        kvu = mla_kv_up[i].reshape(ckv, C_HEADS, C_NOPE_DIM + C_V_DIM)
        kvu = jnp.concatenate([kvu[:, :, :C_NOPE_DIM].reshape(ckv, -1), kvu[:, :, C_NOPE_DIM:].reshape(ckv, -1)],
                              axis=1).astype(BF16)
        rope_norm_p = _pad_cols(mla_rope_norm[i], LANES)

        mod = _adaln(cond8, ada_down[i], ada_up[i], ada_bias[i])[:2].reshape(2, N_MOD, d)

        h = _norm_mod(x_all, norm_mix[i], mod[:, 0], mod[:, 1], nu)
        w_in_i = functools.partial(_matmul, h, w_pad, w_row0=i * d, k=d)
        za = w_in_i(F32, n=n_a_p, name="w_in_a")
        zb = w_in_i(F32, w_col0=n_a_p, n=n_b, name="w_in_b")
        zc = w_in_i(F32, w_col0=n_a_p + n_b, n=n_c_p, name="w_in_c")
        gates = w_in_i(BF16, w_col0=n_a_p + n_b + n_c_p, n=N_BRANCH * d, act="sigmoid", name="w_in_gate")

        r, k, v, e, a, g = _rwkv_feat(za, mu_p, rwkv_w0[i], rwkv_w_up[i], rwkv_a0[i], rwkv_a_up[i], g_up_p, nu)
        ys = []
        for direction in (0, 1):
            p_, q_, r2, y0 = _rwkv_chunks(r, k, v, e, a, rwkv_k_k[i], rwkv_k_a[i], direction)
            ys.append(_rwkv_seq(p_, q_, r2, y0, direction, nu))
        ya = _rwkv_out(ys[0], ys[1], r, k, v, a, g, rwkv_lnx_w[i], rwkv_lnx_b[i], rwkv_r_k[i], rwkv_k_a[i])

        qkv_b = _gqa_prep(zb, cos, sin, gqa_q_norm[i], gqa_k_norm[i])
        yb = _gqa_attn(qkv_b, gqa_sink[i], nu, seq)

        assert cq % ckv == 0
        q_c = _matmul(zc, qu, F32, rms_gain=mla_q_a_norm[i], name="mla_q_up")
        kv_c = _matmul(zc, kvu, F32, a_col_block=cq // ckv, rms_gain=mla_kv_a_norm[i], name="mla_kv_up")
        qf, kf, vf = _mla_prep(q_c, kv_c, zc, (cq + ckv) // LANES, cos, sin, mla_nope_norm[i], rope_norm_p)
        yc = _mla_flash(qf, kf, vf, nu)

        merged = _merge(ya, yb, yc, w_branch_b, i, gates)
        x_all = _matmul(merged, w_out_b, F32, w_row0=i * d, k=d, resid=x_all, gate2=mod[:, 2], nu=nu, name="w_out")

        hf, aff = _norm_mod(x_all, norm_ffn[i], mod[:, 3], mod[:, 4], nu, router_t=moe_router[i].T)
        last = i == depth - 1
        x_all = _moe_stream(x_all, hf, aff, nu, seq, mod[1:2, 5], wg, wu, wd, i, rows_only=last)
        if not last:
            x_all = _moe_stream(x_all, hf, aff, 0, nu, mod[0:1, 5], wg, wu, wd, i)
    return x_all.reshape(bsz, seq, d)
```

```python
import functools
import math

import jax
import jax.numpy as jnp
from jax import lax
from jax.experimental import pallas as pl
from jax.experimental.pallas import tpu as pltpu

F32 = jnp.float32
BF16 = jnp.bfloat16
I32 = jnp.int32

GRID_W = 64
ROPE_DIM = 64
ROPE_THETA = 10000.0
NORM_EPS = 1e-6
NEG_INF = -1e30
N_MOD = 6
A_HEADS = 16
A_HEAD_DIM = 64
A_LNX_EPS = 64e-5
B_Q_HEADS = 16
B_KV_HEADS = 4
B_HEAD_DIM = 64
WINDOW = 128
BLOCK = 128
C_HEADS = 8
C_NOPE_DIM = 128
C_ROPE_DIM = 64
C_V_DIM = 128
N_BRANCH = 3
CAPACITY_FACTOR = 2

LANES = 128
CHUNK = 64
SEQ_GROUP = 4
ROW_TILE = 256
MOE_TILE = 128
MOE_ALIGN = 16
MOE_WIN = MOE_TILE + MOE_ALIGN
MOE_SHORT_WIN = 3 * MOE_ALIGN
SEL_ROWS = 128
VMEM_MB = 56


def _cp(n_grid, vmem_mb=VMEM_MB):
    return pltpu.CompilerParams(dimension_semantics=("arbitrary",) * n_grid,
                                vmem_limit_bytes=vmem_mb * 1024 * 1024)


def _pick(n, cands):
    for c in cands:
        if n % c == 0:
            return c
    raise ValueError(f"no tile for {n} in {cands}")


def _dg(a, b, ca=1, cb=0):
    return lax.dot_general(a, b, (((ca,), (cb,)), ((), ())), preferred_element_type=F32)


def _split2(x):
    hi = x.astype(BF16)
    lo = (x - hi.astype(F32)).astype(BF16)
    return hi, lo


def _mm(a, b, passes=1, nt=False):
    cb = 1 if nt else 0
    if passes == 1:
        return _dg(a.astype(BF16), b.astype(BF16), 1, cb)
    ah, al = _split2(a)
    bh, bl = _split2(b)
    return _dg(ah, bh, 1, cb) + (_dg(ah, bl, 1, cb) + _dg(al, bh, 1, cb))


def _mm_exact_rhs(x, m_bf16):
    x1 = x.astype(BF16)
    r1 = x - x1.astype(F32)
    x2 = r1.astype(BF16)
    x3 = (r1 - x2.astype(F32)).astype(BF16)
    return _dg(x1, m_bf16) + (_dg(x2, m_bf16) + _dg(x3, m_bf16))


def _sigmoid(x):
    return 1.0 / (1.0 + jnp.exp(-x))


def _iota(shape, dim):
    return lax.broadcasted_iota(I32, shape, dim)


def _block_ones(n, blk):
    i = _iota((n, n), 0) // blk
    j = _iota((n, n), 1) // blk
    return jnp.where(i == j, 1.0, 0.0).astype(BF16)


def _seg_sum(x, ones_bf16):
    return _mm_exact_rhs(x, ones_bf16)


def _swap16(x):
    n = x.shape[-1]
    lane = _iota(x.shape, x.ndim - 1)
    fwd = pltpu.roll(x, n - 16, x.ndim - 1)
    bwd = pltpu.roll(x, 16, x.ndim - 1)
    return jnp.where((lane % 32) < 16, fwd, bwd)


def _adaln_kernel(c_ref, dn_ref, up_ref, b_ref, o_ref):
    c = c_ref[...]
    t = _mm(c * _sigmoid(c), dn_ref[...], 3)
    o_ref[...] = _mm(t, up_ref[...], 3) + b_ref[...]


def _adaln(cond8, down, up, bias):
    d, r = down.shape
    n = up.shape[1]
    tn = _pick(n, (4096, 2048, 1024, 512, 256, 128))
    return pl.pallas_call(
        _adaln_kernel,
        grid=(n // tn,),
        in_specs=[pl.BlockSpec((8, d), lambda j: (0, 0)),
                  pl.BlockSpec((d, r), lambda j: (0, 0)),
                  pl.BlockSpec((r, tn), lambda j: (0, j)),
                  pl.BlockSpec((1, tn), lambda j: (0, j))],
        out_specs=pl.BlockSpec((8, tn), lambda j: (0, j)),
        out_shape=jax.ShapeDtypeStruct((8, n), F32),
        compiler_params=_cp(1),
        name="adaln",
    )(cond8, down, up, bias.reshape(1, n))


def _moe_dcol(d):
    return min(d, 1024)


def _rows_are_ctx(tile_rows, row0, nu):
    return (row0 + _iota((tile_rows, 1), 0)) < nu


def _norm_mod_kernel(x_ref, g_ref, sh_ref, sc_ref, *rest, nu, tm, router):
    if router:
        wr_ref, h_ref, aff_ref = rest
    else:
        (h_ref,) = rest
    x = x_ref[...]
    is_u = pl.program_id(0) * tm < nu
    y = x * lax.rsqrt(jnp.mean(x * x, axis=-1, keepdims=True) + NORM_EPS) * g_ref[...]
    sh = jnp.where(is_u, sh_ref[0:1, :], sh_ref[1:2, :])
    sc = jnp.where(is_u, sc_ref[0:1, :], sc_ref[1:2, :])
    h = y * (1.0 + sc) + sh
    if router:
        dcol = h_ref.shape[2]
        for c in range(h_ref.shape[0]):
            h_ref[c] = h[:, c * dcol:(c + 1) * dcol].astype(h_ref.dtype)
    else:
        h_ref[...] = h.astype(h_ref.dtype)
    if router:
        logits = _mm(wr_ref[...], h, 3, nt=True)
        m = jnp.max(logits, axis=0, keepdims=True)
        p = jnp.exp(logits - m)
        aff_ref[...] = p / jnp.sum(p, axis=0, keepdims=True)


def _norm_mod(x_all, gain, shift2, scale2, nu, router_t=None):
    t, d = x_all.shape
    tm = ROW_TILE
    assert nu % tm == 0
    router = router_t is not None
    in_specs = [pl.BlockSpec((tm, d), lambda i: (i, 0)),
                pl.BlockSpec((1, d), lambda i: (0, 0)),
                pl.BlockSpec((2, d), lambda i: (0, 0)),
                pl.BlockSpec((2, d), lambda i: (0, 0))]
    args = [x_all, gain.reshape(1, d), shift2, scale2]
    out_specs = pl.BlockSpec((tm, d), lambda i: (i, 0))
    out_shape = jax.ShapeDtypeStruct((t, d), BF16)
    if router:
        e = router_t.shape[0]
        dcol = _moe_dcol(d)
        in_specs.append(pl.BlockSpec((e, d), lambda i: (0, 0)))
        args.append(router_t)
        out_specs = (pl.BlockSpec((d // dcol, tm, dcol), lambda i: (0, i, 0)), pl.BlockSpec((e, tm), lambda i: (0, i)))
        out_shape = (jax.ShapeDtypeStruct((d // dcol, t, dcol), BF16), jax.ShapeDtypeStruct((e, t), F32))
    return pl.pallas_call(
        functools.partial(_norm_mod_kernel, nu=nu, tm=tm, router=router),
        grid=(t // tm,), in_specs=in_specs, out_specs=out_specs, out_shape=out_shape,
        compiler_params=_cp(1), name="norm_mod_router" if router else "norm_mod",
    )(*args)


def _matmul_kernel(*refs, nu, tm, act, has_rms, has_resid):
    it = iter(refs)
    a_ref = next(it)
    w_ref = next(it)
    g_ref = next(it) if has_rms else None
    x_ref = next(it) if has_resid else None
    gate_ref = next(it) if has_resid else None
    o_ref = next(it)
    a = a_ref[...]
    if has_rms:
        af = a.astype(F32)
        a = af * lax.rsqrt(jnp.mean(af * af, axis=-1, keepdims=True) + NORM_EPS) * g_ref[...]
    acc = _dg(a.astype(BF16), w_ref[...])
    if act == "sigmoid":
        acc = _sigmoid(acc)
    if has_resid:
        is_u = _rows_are_ctx(tm, pl.program_id(1) * tm, nu)
        gate = jnp.where(is_u, gate_ref[0:1, :], gate_ref[1:2, :])
        acc = x_ref[...] + gate * acc
    o_ref[...] = acc.astype(o_ref.dtype)


def _matmul(a, w, out_dtype, *, a_col_block=0, w_row0=0, k=None, w_col0=0, n=None, act=None, rms_gain=None,
            resid=None, gate2=None, nu=0, name="matmul"):
    m = a.shape[0]
    k = w.shape[0] if k is None else k
    n = w.shape[1] if n is None else n
    tm = _pick(m, (640, 256, 128))
    tn = n if n <= 2048 else _pick(n, (1024, 768, 512, 384, 256, 128))
    has_rms = rms_gain is not None
    has_resid = resid is not None
    in_specs = [pl.BlockSpec((tm, k), lambda j, i: (i, a_col_block)),
                pl.BlockSpec((pl.Element(k), pl.Element(tn)),
                             lambda j, i: (w_row0, pl.multiple_of(w_col0 + j * tn, LANES)))]
    args = [a, w]
    if has_rms:
        in_specs.append(pl.BlockSpec((1, k), lambda j, i: (0, 0)))
        args.append(rms_gain.reshape(1, k))
    if has_resid:
        in_specs += [pl.BlockSpec((tm, tn), lambda j, i: (i, j)),
                     pl.BlockSpec((2, tn), lambda j, i: (0, j))]
        args += [resid, gate2]
    return pl.pallas_call(
        functools.partial(_matmul_kernel, nu=nu, tm=tm, act=act, has_rms=has_rms, has_resid=has_resid),
        grid=(n // tn, m // tm), in_specs=in_specs,
        out_specs=pl.BlockSpec((tm, tn), lambda j, i: (i, j)),
        out_shape=jax.ShapeDtypeStruct((m, n), out_dtype),
        compiler_params=_cp(2), name=name,
    )(*args)


def _rwkv_feat_kernel(z_ref, zp_ref, zn_ref, mu_ref, w0_ref, wup_ref, a0_ref, aup_ref, gup_ref,
                      r_ref, k_ref, v_ref, e_ref, a_ref, g_ref, *, nu, t_all, tm, aw, dr):
    z = z_ref[...]
    i = pl.program_id(0)
    ri = _iota((tm, 1), 0)
    first = (i == 0) | (i == nu // tm)
    last = (i == nu // tm - 1) | (i == t_all // tm - 1)
    prev_row = jnp.where(first, 0.0, zp_ref[7:8, :])
    next_row = jnp.where(last, 0.0, zn_ref[0:1, :])
    zp = jnp.where(ri == 0, prev_row, pltpu.roll(z, 1, 0))
    zn = jnp.where(ri == tm - 1, next_row, pltpu.roll(z, tm - 1, 0))
    zs = z + mu_ref[0:1, :] * (zp - z) + mu_ref[1:2, :] * (zn - z)
    r_ref[...] = zs[:, 0:aw]
    k_ref[...] = zs[:, aw:2 * aw]
    v_ref[...] = zs[:, 2 * aw:3 * aw]
    o = 3 * aw
    for n in range(2):
        wd = jnp.tanh(zs[:, o + n * dr:o + (n + 1) * dr])
        w = w0_ref[n:n + 1, :] + _mm(wd, wup_ref[n])
        sp = jnp.maximum(-w, 0.0) + jnp.log(1.0 + jnp.exp(-jnp.abs(w)))
        e_ref[n] = jnp.exp(-sp - 0.5)
    o += 2 * dr
    for n in range(2):
        ad = zs[:, o + n * dr:o + (n + 1) * dr]
        a_ref[n] = _sigmoid(a0_ref[n:n + 1, :] + _mm(ad, aup_ref[n]))
    o += 2 * dr
    g_ref[...] = _mm(_sigmoid(zs[:, o:]), gup_ref[...])


def _rwkv_feat(za, mu_p, w0, w_up, a0, a_up, g_up_p, nu):
    t, na = za.shape
    aw = w0.shape[1]
    dr = w_up.shape[1]
    tm = ROW_TILE
    assert nu % tm == 0
    nb8 = tm // 8
    last8 = t // 8 - 1
    row_spec = lambda w: pl.BlockSpec((tm, w), lambda i: (i, 0))
    full = lambda a: pl.BlockSpec(a.shape, lambda i: (0,) * a.ndim)
    out_rows = jax.ShapeDtypeStruct((t, aw), F32)
    out_dir = jax.ShapeDtypeStruct((2, t, aw), F32)
    dir_spec = pl.BlockSpec((2, tm, aw), lambda i: (0, i, 0))
    return pl.pallas_call(
        functools.partial(_rwkv_feat_kernel, nu=nu, t_all=t, tm=tm, aw=aw, dr=dr),
        grid=(t // tm,),
        in_specs=[row_spec(na),
                  pl.BlockSpec((8, na), lambda i: (jnp.maximum(i * nb8 - 1, 0), 0)),
                  pl.BlockSpec((8, na), lambda i: (jnp.minimum((i + 1) * nb8, last8), 0)),
                  full(mu_p), full(w0), full(w_up), full(a0), full(a_up), full(g_up_p)],
        out_specs=(row_spec(aw), row_spec(aw), row_spec(aw), dir_spec, dir_spec, row_spec(aw)),
        out_shape=(out_rows, out_rows, out_rows, out_dir, out_dir, out_rows),
        compiler_params=_cp(1), name="rwkv_feat",
    )(za, za, za, mu_p, w0, w_up, a0, a_up, g_up_p)


def _bmm(a, b, nt=False):
    cb = 2 if nt else 1
    return lax.dot_general(a.astype(BF16), b.astype(BF16), (((2,), (cb,)), ((0,), (0,))),
                           preferred_element_type=F32)


def _bmm_exact_lhs(m_bf16, x):
    dn = (((2,), (1,)), ((0,), (0,)))
    x1 = x.astype(BF16)
    r1 = x - x1.astype(F32)
    x2 = r1.astype(BF16)
    x3 = (r1 - x2.astype(F32)).astype(BF16)
    d = lambda y: lax.dot_general(m_bf16, y, dn, preferred_element_type=F32)
    return d(x1) + (d(x2) + d(x3))


def _bt(x):
    return jnp.stack([x[g].T for g in range(x.shape[0])], axis=0)


def _rwkv_chunk_math(r, k, v, e, a, kk_gain, ka_gain, reverse, groups):
    c = CHUNK
    hd = A_HEAD_DIM
    lane = _iota((1, LANES), 1)
    m_a = jnp.where(lane < hd, 1.0, 0.0)
    m_b = 1.0 - m_a
    ones_seg = _block_ones(LANES, hd)
    ti = _iota((groups, c, c), 1)
    tj = _iota((groups, c, c), 2)
    tri = jnp.where((tj >= ti) if reverse else (tj <= ti), 1.0, 0.0).astype(BF16)

    kk0 = k * kk_gain
    kk = kk0 / jnp.maximum(jnp.sqrt(_seg_sum(kk0 * kk0, ones_seg)), 1e-12)
    kt = k * (1.0 + (a - 1.0) * ka_gain)
    b = kk * a
    g3 = lambda x: x.reshape(groups, c, LANES)
    e3 = g3(e)
    cl = _bmm_exact_lhs(tri, e3)
    last = 0 if reverse else c - 1
    ctot = cl[:, last:last + 1, :]
    g_in = jnp.exp(-cl)
    g_ex = jnp.exp(e3 - cl)
    g_inv = jnp.exp(cl)
    g_end = jnp.exp(cl - ctot)
    st = lambda x: jnp.concatenate([x * m_a, x * m_b], axis=1)
    kk3, b3, kt3 = g3(kk), g3(b), g3(kt)
    kk2 = st(kk3 * g_ex)
    r2 = st(g3(r) * g_in)
    b2 = st(b3 * g_inv)
    k2 = st(kt3 * g_inv)
    v2 = st(g3(v))
    bg2 = st(b3 * g_end)
    kg2 = st(kt3 * g_end)

    s = _bmm(jnp.concatenate([kk2, r2], axis=1), jnp.concatenate([b2, k2], axis=1), nt=True)
    n2 = 2 * c
    i2 = _iota((n2, n2), 0)
    j2 = _iota((n2, n2), 1)
    il = i2 % c
    jl = j2 % c
    strict = (jl > il) if reverse else (jl < il)
    incl = (jl >= il) if reverse else (jl <= il)
    a_b = jnp.where(strict, s[:, :n2, :n2], 0.0)
    a_k = jnp.where(strict, s[:, :n2, n2:], 0.0)
    l_b = jnp.where(incl, s[:, n2:, :n2], 0.0)
    l_k = jnp.where(incl, s[:, n2:, n2:], 0.0)
    eye = jnp.where(i2 == j2, 1.0, 0.0)

    same = lambda m: (i2 // m) == (j2 // m)
    a_d = jnp.where(same(8), a_b, 0.0)
    a_d2 = _bmm(a_d, a_d)
    a_d4 = _bmm(a_d2, a_d2)
    t_inv = _bmm(_bmm(eye - a_d, eye + a_d2), eye + a_d4)
    m = 8
    while m < c:
        a_off = jnp.where(same(2 * m) & jnp.logical_not(same(m)), a_b, 0.0)
        t_inv = t_inv - _bmm(_bmm(t_inv, a_off), t_inv)
        m *= 2

    av = _bmm(a_k, v2)
    z12 = _bmm(t_inv, jnp.concatenate([kk2, av], axis=2))
    z1 = z12[:, :, :LANES]
    z2 = z12[:, :, LANES:]
    vz = jnp.concatenate([v2, z2], axis=1)
    p_bd = eye * jnp.exp(-ctot) - _bmm(_bt(bg2), z1)
    q_bd = _bmm(_bt(jnp.concatenate([kg2, -bg2], axis=1)), vz)
    r2s = r2 - _bmm(l_b, z1)
    y0s = _bmm(jnp.concatenate([l_k, -l_b], axis=2), vz)
    cp = lambda x: x[:, :x.shape[1] // 2] + x[:, x.shape[1] // 2:]
    return cp(p_bd), cp(q_bd), cp(r2s).reshape(groups * c, LANES), cp(y0s).reshape(groups * c, LANES)


def _rwkv_chunk_kernel(r_ref, k_ref, v_ref, e_ref, a_ref, kkg_ref, kag_ref,
                       p_ref, q_ref, r2_ref, y0_ref, *, reverse, groups):
    p, q, r2, y0 = _rwkv_chunk_math(r_ref[...], k_ref[...], v_ref[...], e_ref[0], a_ref[0],
                                    kkg_ref[...], kag_ref[...], reverse, groups)
    p_ref[0] = p
    q_ref[0] = q
    r2_ref[...] = r2
    y0_ref[...] = y0


def _rwkv_chunks(r, k, v, e, a, k_k, k_a, direction):
    t, aw = r.shape
    npair = aw // LANES
    nc = t // CHUNK
    groups = _pick(nc, (20, 10, 4))
    rows = groups * CHUNK
    row_spec = pl.BlockSpec((rows, LANES), lambda p, i: (i, p))
    dir_spec = pl.BlockSpec((1, rows, LANES), lambda p, i: (direction, i, p))
    par_spec = pl.BlockSpec((1, LANES), lambda p, i: (0, p))
    pq_spec = pl.BlockSpec((1, groups, A_HEAD_DIM, LANES), lambda p, i: (p, i, 0, 0))
    pq_shape = jax.ShapeDtypeStruct((npair, nc, A_HEAD_DIM, LANES), F32)
    ry_shape = jax.ShapeDtypeStruct((t, aw), F32)
    return pl.pallas_call(
        functools.partial(_rwkv_chunk_kernel, reverse=bool(direction), groups=groups),
        grid=(npair, t // rows),
        in_specs=[row_spec, row_spec, row_spec, dir_spec, dir_spec, par_spec, par_spec],
        out_specs=(pq_spec, pq_spec, row_spec, row_spec),
        out_shape=(pq_shape, pq_shape, ry_shape, ry_shape),
        compiler_params=_cp(2), name="rwkv_chunks_bwd" if direction else "rwkv_chunks_fwd",
    )(r, k, v, e, a, k_k.reshape(1, aw), k_a.reshape(1, aw))


def _pair_block_diag(x):
    lane = _iota(x.shape, 1)
    return jnp.concatenate([jnp.where(lane < A_HEAD_DIM, x, 0.0), jnp.where(lane >= A_HEAD_DIM, x, 0.0)], axis=0)


def _rwkv_seq_kernel(p_ref, q_ref, r2_ref, y0_ref, y_ref, h_ref, *, reverse, groups, npair):
    c = CHUNK

    @pl.when(pl.program_id(0) == 0)
    def _():
        h_ref[...] = jnp.zeros_like(h_ref)

    order = range(groups - 1, -1, -1) if reverse else range(groups)
    for g in order:
        sl = slice(g * c, (g + 1) * c)
        for p in range(npair):
            ls = slice(p * LANES, (p + 1) * LANES)
            h = h_ref[p]
            y_ref[sl, ls] = _mm(r2_ref[sl, ls], h) + y0_ref[sl, ls]
            h_ref[p] = _mm(_pair_block_diag(p_ref[p, g]), h) + _pair_block_diag(q_ref[p, g])


def _rwkv_seq(p, q, r2, y0, direction, nu):
    npair, nc = p.shape[:2]
    t, aw = r2.shape
    groups = SEQ_GROUP
    rows = groups * CHUNK
    nb = t // rows
    nbu = nu // rows
    assert nu % rows == 0
    if direction:
        blk = lambda s: jnp.where(s < nbu, nbu - 1 - s, nb + nbu - 1 - s)
    else:
        blk = lambda s: s
    pq_spec = pl.BlockSpec((npair, groups, A_HEAD_DIM, LANES), lambda s: (0, blk(s), 0, 0))
    row_spec = pl.BlockSpec((rows, aw), lambda s: (blk(s), 0))
    return pl.pallas_call(
        functools.partial(_rwkv_seq_kernel, reverse=bool(direction), groups=groups, npair=npair),
        grid=(nb,),
        in_specs=[pq_spec, pq_spec, row_spec, row_spec],
        out_specs=row_spec,
        out_shape=jax.ShapeDtypeStruct((t, aw), F32),
        scratch_shapes=[pltpu.VMEM((npair, LANES, LANES), F32)],
        compiler_params=_cp(1), name="rwkv_seq_bwd" if direction else "rwkv_seq_fwd",
    )(p, q, r2, y0)


def _rwkv_out_kernel(yf_ref, yb_ref, r_ref, k_ref, v_ref, a_ref, g_ref, lw_ref, lb_ref, rk_ref, ka_ref, o_ref, *, aw):
    hd = A_HEAD_DIM
    ones_seg = _block_ones(LANES, hd)
    for p in range(aw // LANES):
        ls = slice(p * LANES, (p + 1) * LANES)
        y = yf_ref[:, ls] + yb_ref[:, ls]
        mean = _seg_sum(y, ones_seg) * (1.0 / hd)
        yc = y - mean
        var = _seg_sum(yc * yc, ones_seg) * (1.0 / hd)
        yn = yc * lax.rsqrt(var + A_LNX_EPS) * lw_ref[:, ls] + lb_ref[:, ls]
        r = r_ref[:, ls]
        k = k_ref[:, ls]
        v = v_ref[:, ls]
        bonus = jnp.zeros_like(y)
        for n in range(2):
            kt = k * (1.0 + (a_ref[n, :, ls] - 1.0) * ka_ref[:, ls])
            bonus = bonus + _seg_sum(r * kt * rk_ref[:, ls], ones_seg) * v
        o_ref[:, ls] = ((yn + bonus) * g_ref[:, ls]).astype(o_ref.dtype)


def _rwkv_out(yf, yb, r, k, v, a, g, lnx_w, lnx_b, r_k, k_a):
    t, aw = r.shape
    tm = ROW_TILE
    row_spec = pl.BlockSpec((tm, aw), lambda i: (i, 0))
    par_spec = pl.BlockSpec((1, aw), lambda i: (0, 0))
    return pl.pallas_call(
        functools.partial(_rwkv_out_kernel, aw=aw),
        grid=(t // tm,),
        in_specs=[row_spec] * 5 + [pl.BlockSpec((2, tm, aw), lambda i: (0, i, 0)), row_spec] + [par_spec] * 4,
        out_specs=row_spec,
        out_shape=jax.ShapeDtypeStruct((t, aw), BF16),
        compiler_params=_cp(1), name="rwkv_out",
    )(yf, yb, r, k, v, a, g, lnx_w.reshape(1, aw), lnx_b.reshape(1, aw), r_k.reshape(1, aw), k_a.reshape(1, aw))


def _rms_rope_slab(x, gain, cos, sin, ones_seg):
    ms = _seg_sum(x * x, ones_seg) * (1.0 / B_HEAD_DIM)
    y = x * lax.rsqrt(ms + NORM_EPS) * gain
    return y * cos + _swap16(y) * sin


def _gqa_prep_kernel(z_ref, cos_ref, sin_ref, qg_ref, kg_ref, o_ref, *, qw, kw):
    ones_seg = _block_ones(LANES, B_HEAD_DIM)
    cos = cos_ref[...]
    sin = sin_ref[...]
    scale = B_HEAD_DIM ** -0.5 * math.log2(math.e)
    for s in range(qw // LANES):
        ls = slice(s * LANES, (s + 1) * LANES)
        o_ref[:, ls] = (_rms_rope_slab(z_ref[:, ls], qg_ref[...], cos, sin, ones_seg) * scale).astype(o_ref.dtype)
    for s in range(kw // LANES):
        ls = slice(qw + s * LANES, qw + (s + 1) * LANES)
        o_ref[:, ls] = _rms_rope_slab(z_ref[:, ls], kg_ref[...], cos, sin, ones_seg).astype(o_ref.dtype)
    o_ref[:, qw + kw:] = z_ref[:, qw + kw:].astype(o_ref.dtype)


def _gqa_prep(zb, cos, sin, q_norm, k_norm):
    t, nb = zb.shape
    tm = ROW_TILE
    qw = B_Q_HEADS * B_HEAD_DIM
    kw = B_KV_HEADS * B_HEAD_DIM
    tile2 = lambda g: jnp.tile(g.reshape(1, B_HEAD_DIM), (1, LANES // B_HEAD_DIM))
    return pl.pallas_call(
        functools.partial(_gqa_prep_kernel, qw=qw, kw=kw),
        grid=(t // tm,),
        in_specs=[pl.BlockSpec((tm, nb), lambda i: (i, 0)),
                  pl.BlockSpec((tm, LANES), lambda i: (i, 0)),
                  pl.BlockSpec((tm, LANES), lambda i: (i, 0)),
                  pl.BlockSpec((1, LANES), lambda i: (0, 0)),
                  pl.BlockSpec((1, LANES), lambda i: (0, 0))],
        out_specs=pl.BlockSpec((tm, nb), lambda i: (i, 0)),
        out_shape=jax.ShapeDtypeStruct((t, nb), BF16),
        compiler_params=_cp(1), name="gqa_prep",
    )(zb, cos, sin, tile2(q_norm), tile2(k_norm))


def _gqa_attn_kernel(sink_ref, q_ref, c_ref, kp_ref, ko_ref, kn_ref, o_ref, *, nu, seq, qw, kw):
    hd = B_HEAD_DIM
    grp = B_Q_HEADS // B_KV_HEADS
    blk = BLOCK
    j = pl.program_id(0)
    jb = j - nu // blk
    nkeys = nu + 3 * blk
    rows = grp * blk
    qi = _iota((rows, nkeys), 0) % blk
    kc = _iota((rows, nkeys), 1)
    q_pos = jb * blk + qi
    k_pos = (jb - 1) * blk + (kc - nu)
    band_ok = (jnp.abs(k_pos - q_pos) <= WINDOW) & (k_pos >= 0) & (k_pos < seq) & (jb >= 0)
    valid = (kc < nu) | band_ok
    rg = _iota((rows, 1), 0) // blk
    outs = []
    for h in range(B_KV_HEADS):
        ks = slice(qw + h * hd, qw + (h + 1) * hd)
        vs = slice(qw + kw + h * hd, qw + kw + (h + 1) * hd)
        k_all = jnp.concatenate([c_ref[:, ks], kp_ref[:, ks], ko_ref[:, ks], kn_ref[:, ks]], axis=0)
        v_all = jnp.concatenate([c_ref[:, vs], kp_ref[:, vs], ko_ref[:, vs], kn_ref[:, vs]], axis=0)
        q4 = jnp.concatenate([q_ref[:, (h * grp + g) * hd:(h * grp + g + 1) * hd] for g in range(grp)], axis=0)
        s = jnp.where(valid, _dg(q4, k_all, 1, 1), NEG_INF)
        sink = jnp.zeros((rows, 1), F32)
        for g in range(grp):
            sink = jnp.where(rg == g, sink_ref[h * grp + g] * math.log2(math.e), sink)
        m = jnp.maximum(jnp.max(s, axis=1, keepdims=True), sink)
        p = jnp.exp2((s - m).astype(BF16))
        pv = _dg(p, jnp.concatenate([v_all, jnp.ones_like(v_all)], axis=1))
        o = pv[:, :hd] / (pv[:, hd:hd + 1] + jnp.exp2(sink - m))
        outs += [o[g * blk:(g + 1) * blk, :] for g in range(grp)]
    o_ref[...] = jnp.concatenate(outs, axis=1).astype(o_ref.dtype)


def _gqa_attn(qkv, sink, nu, seq):
    t, nb = qkv.shape
    qw = B_Q_HEADS * B_HEAD_DIM
    kw = B_KV_HEADS * B_HEAD_DIM
    blk = BLOCK
    nblk = t // blk
    band = lambda off: pl.BlockSpec((blk, nb), lambda j: (jnp.clip(j + off, 0, nblk - 1), 0))
    return pl.pallas_call(
        functools.partial(_gqa_attn_kernel, nu=nu, seq=seq, qw=qw, kw=kw),
        grid=(nblk,),
        in_specs=[pl.BlockSpec(memory_space=pltpu.SMEM),
                  pl.BlockSpec((blk, nb), lambda j: (j, 0)),
                  pl.BlockSpec((nu, nb), lambda j: (0, 0)),
                  band(-1), band(0), band(1)],
        out_specs=pl.BlockSpec((blk, qw), lambda j: (j, 0)),
        out_shape=jax.ShapeDtypeStruct((t, qw), BF16),
        compiler_params=_cp(1), name="gqa_attn",
    )(sink, qkv, qkv, qkv, qkv, qkv)


def _mla_prep_kernel(q_ref, kv_ref, kr_ref, cos_ref, sin_ref, nn_ref, rn_ref, qo_ref, ko_ref, vo_ref):
    cos = cos_ref[...]
    sin = sin_ref[...]
    dn = C_NOPE_DIM
    hw = 2 * LANES
    scale = (C_NOPE_DIM + C_ROPE_DIM) ** -0.5 * math.log2(math.e)
    ones = jnp.ones((q_ref.shape[0], C_V_DIM), vo_ref.dtype)

    def rms_rope(x, gain):
        ms = jnp.sum(x * x, axis=-1, keepdims=True) * (1.0 / C_ROPE_DIM)
        y = x * lax.rsqrt(ms + NORM_EPS) * gain
        return y * cos + _swap16(y) * sin

    def rms(x, gain):
        return x * lax.rsqrt(jnp.mean(x * x, axis=-1, keepdims=True) + NORM_EPS) * gain

    kr = rms_rope(kr_ref[...], rn_ref[1:2, :]).astype(ko_ref.dtype)
    for h in range(C_HEADS):
        qn = rms(q_ref[:, h * hw:h * hw + dn], nn_ref[0:1, :])
        qr = rms_rope(q_ref[:, h * hw + dn:(h + 1) * hw], rn_ref[0:1, :])
        qo_ref[:, h * hw:h * hw + dn] = (qn * scale).astype(qo_ref.dtype)
        qo_ref[:, h * hw + dn:(h + 1) * hw] = (qr * scale).astype(qo_ref.dtype)
        ko_ref[:, h * hw:h * hw + dn] = rms(kv_ref[:, h * dn:(h + 1) * dn], nn_ref[1:2, :]).astype(ko_ref.dtype)
        ko_ref[:, h * hw + dn:(h + 1) * hw] = kr
        vo_ref[:, h * hw:h * hw + C_V_DIM] = kv_ref[:, (C_HEADS + h) * dn:(C_HEADS + h + 1) * dn].astype(vo_ref.dtype)
        vo_ref[:, h * hw + C_V_DIM:(h + 1) * hw] = ones


def _mla_prep(q, kv, zc, kr_col_block, cos, sin, nope_norm, rope_norm_p):
    t = q.shape[0]
    tm = ROW_TILE
    hw = 2 * LANES
    row = lambda w: pl.BlockSpec((tm, w), lambda i: (i, 0))
    full = lambda a: pl.BlockSpec(a.shape, lambda i: (0, 0))
    return pl.pallas_call(
        _mla_prep_kernel,
        grid=(t // tm,),
        in_specs=[row(q.shape[1]), row(kv.shape[1]),
                  pl.BlockSpec((tm, LANES), lambda i: (i, kr_col_block)),
                  row(LANES), row(LANES), full(nope_norm), full(rope_norm_p)],
        out_specs=(row(C_HEADS * hw),) * 3,
        out_shape=(jax.ShapeDtypeStruct((t, C_HEADS * hw), BF16),) * 3,
        compiler_params=_cp(1), name="mla_prep",
    )(q, kv, zc, cos, sin, nope_norm, rope_norm_p)


def _mla_flash_kernel(q_ref, k_ref, v_ref, o_ref, m_ref, acc_ref, *, nu, tq, tk, sub):
    qi = pl.program_id(1)
    ki = pl.program_id(2)

    @pl.when(ki == 0)
    def _():
        m_ref[...] = jnp.full_like(m_ref, NEG_INF)
        acc_ref[...] = jnp.zeros_like(acc_ref)

    def update(masked):
        for qs in range(tq // sub):
            rs = slice(qs * sub, (qs + 1) * sub)
            s = _dg(q_ref[rs, :], k_ref[...], 1, 1)
            if masked and qs * sub < nu:
                qrow = qi * tq + qs * sub + _iota((sub, tk), 0)
                kcol = ki * tk + _iota((sub, tk), 1)
                s = jnp.where((qrow < nu) & (kcol >= nu), NEG_INF, s)
            m_prev = m_ref[rs, :]
            m_next = jnp.maximum(m_prev, jnp.max(s, axis=1, keepdims=True))
            alpha = jnp.exp2(m_prev - m_next)
            p = jnp.exp2((s - m_next[:, 0:1]).astype(BF16))
            acc_ref[rs, :] = jnp.concatenate([alpha, alpha], axis=1) * acc_ref[rs, :] + _dg(p, v_ref[...])
            m_ref[rs, :] = m_next

    has_ctx_rows = qi * tq < nu

    @pl.when(has_ctx_rows)
    def _():
        update(True)

    @pl.when(jnp.logical_not(has_ctx_rows))
    def _():
        update(False)

    @pl.when(ki == pl.num_programs(2) - 1)
    def _():
        o_ref[...] = (acc_ref[:, :C_V_DIM] / acc_ref[:, C_V_DIM:]).astype(o_ref.dtype)


def _mla_flash(qf, kf, vf, nu):
    t = qf.shape[0]
    hw = 2 * LANES
    tq = _pick(t, (3328, 256))
    tk = _pick(t, (3328, 256))
    return pl.pallas_call(
        functools.partial(_mla_flash_kernel, nu=nu, tq=tq, tk=tk, sub=256),
        grid=(C_HEADS, t // tq, t // tk),
        in_specs=[pl.BlockSpec((tq, hw), lambda h, i, j: (i, h)),
                  pl.BlockSpec((tk, hw), lambda h, i, j: (j, h)),
                  pl.BlockSpec((tk, hw), lambda h, i, j: (j, h))],
        out_specs=pl.BlockSpec((tq, C_V_DIM), lambda h, i, j: (i, h)),
        out_shape=jax.ShapeDtypeStruct((t, C_HEADS * C_V_DIM), BF16),
        scratch_shapes=[pltpu.VMEM((tq, LANES), F32), pltpu.VMEM((tq, hw), F32)],
        compiler_params=_cp(3), name="mla_flash",
    )(qf, kf, vf)


def _merge_kernel(ya_ref, yb_ref, yc_ref, w_ref, g0_ref, g1_ref, g2_ref, o_ref):
    acc = g0_ref[...].astype(F32) * _dg(ya_ref[...], w_ref[0])
    acc = acc + g1_ref[...].astype(F32) * _dg(yb_ref[...], w_ref[1])
    acc = acc + g2_ref[...].astype(F32) * _dg(yc_ref[...], w_ref[2])
    o_ref[...] = acc.astype(o_ref.dtype)


def _merge(ya, yb, yc, w_branch, layer, gates):
    t, bw = ya.shape
    d = w_branch.shape[2]
    tm = _pick(t, (640, 256, 128))
    tn = _pick(d, (1024, 512, 256, 128))
    nj = d // tn
    y_spec = pl.BlockSpec((tm, bw), lambda j, i: (i, 0))
    gate = lambda b: pl.BlockSpec((tm, tn), lambda j, i: (i, b * nj + j))
    return pl.pallas_call(
        _merge_kernel,
        grid=(nj, t // tm),
        in_specs=[y_spec, y_spec, y_spec, pl.BlockSpec((N_BRANCH, bw, tn), lambda j, i: (layer, 0, j)),
                  gate(0), gate(1), gate(2)],
        out_specs=pl.BlockSpec((tm, tn), lambda j, i: (i, j)),
        out_shape=jax.ShapeDtypeStruct((t, d), BF16),
        compiler_params=_cp(2), name="merge",
    )(ya, yb, yc, w_branch, gates, gates, gates)


def _moe_select_kernel(aff_ref, loc_ref, base_ref, *, cap, n_exp):
    aff = aff_ref[...]
    bits = pltpu.bitcast(aff, I32)
    count = lambda m: jnp.sum(jnp.sum(jnp.where(m, 1.0, 0.0), axis=1, keepdims=True), axis=2, keepdims=True)
    theta = jnp.zeros((n_exp, 1, 1), I32)
    for bit in range(30, -1, -1):
        cand = theta | (1 << bit)
        theta = jnp.where(count(bits >= cand) >= cap, cand, theta)
    gt = bits > theta
    eq = bits == theta
    need = cap - count(gt)
    r = aff.shape[1]
    upper = jnp.where(_iota((LANES, LANES), 0) <= _iota((LANES, LANES), 1), 1.0, 0.0).astype(BF16)
    lower_strict = jnp.where(_iota((r, r), 1) < _iota((r, r), 0), 1.0, 0.0).astype(BF16)

    def prefix(x):
        incl = _dg(x.astype(BF16), upper)
        tot = jnp.broadcast_to(incl[:, LANES - 1:LANES], (r, LANES))
        return incl, _dg(lower_strict, tot.astype(BF16))

    for e in range(n_exp):
        xe = jnp.where(eq[e], 1.0, 0.0)
        incl, base = prefix(xe)
        take = eq[e] & ((incl - xe + base) < need[e])
        sel = jnp.where(gt[e] | take, 1.0, 0.0)
        incl, base = prefix(sel)
        loc_ref[e] = jnp.where(sel > 0.0, incl - sel, -4096.0)
        base_ref[e] = base


def _moe_select(aff3, cap):
    e, r, _ = aff3.shape
    spec = pl.BlockSpec((e, r, LANES), lambda i: (0, 0, 0))
    shape = jax.ShapeDtypeStruct((e, r, LANES), F32)
    return pl.pallas_call(
        functools.partial(_moe_select_kernel, cap=cap, n_exp=e),
        grid=(1,), in_specs=[spec], out_specs=(spec, spec), out_shape=(shape, shape),
        compiler_params=_cp(1), name="moe_select",
    )(aff3)


def _window_start(base_ref, e, i):
    b = base_ref[e, i]
    a = (b // MOE_ALIGN) * MOE_ALIGN
    return pl.multiple_of(a, MOE_ALIGN), b - a


def _moe_gather_kernel(base_ref, loc_ref, aff_ref, h_ref, xe_ref, ge_ref, *, tiles, group):
    g = pl.program_id(0)
    c = pl.program_id(1)
    i = pl.program_id(2)

    def onehot_window(k, r, nwin):
        a, off = _window_start(base_ref, g * group + k, i * tiles + r)
        slot = loc_ref[k, r] + off.astype(F32)
        return a, jnp.where(_iota((nwin, MOE_TILE), 0).astype(F32) == slot, 1.0, 0.0)

    @pl.when(i == 0)
    def _():
        xe_ref[...] = jnp.zeros_like(xe_ref)

    def scatter_rows(nwin):
        for k in range(group):
            for r in range(tiles):
                a, onehot = onehot_window(k, r, nwin)
                rows = _dg(onehot.astype(BF16), h_ref[r * MOE_TILE:(r + 1) * MOE_TILE, :])
                head = pl.ds(a, MOE_ALIGN)
                xe_ref[k, head, :] = (xe_ref[k, head, :].astype(F32) + rows[:MOE_ALIGN]).astype(xe_ref.dtype)
                xe_ref[k, pl.ds(a + MOE_ALIGN, nwin - MOE_ALIGN), :] = rows[MOE_ALIGN:].astype(xe_ref.dtype)

    fill = jnp.int32(0)
    for k in range(group):
        for r in range(tiles):
            t = i * tiles + r
            b0 = base_ref[g * group + k, t]
            fill = jnp.maximum(fill, base_ref[g * group + k, t + 1] - (b0 // MOE_ALIGN) * MOE_ALIGN)
    short = fill <= MOE_SHORT_WIN

    @pl.when(short)
    def _():
        scatter_rows(MOE_SHORT_WIN)

    @pl.when(jnp.logical_not(short))
    def _():
        scatter_rows(MOE_WIN)

    @pl.when(c == 0)
    def _():
        @pl.when(i == 0)
        def _():
            ge_ref[...] = jnp.zeros_like(ge_ref)

        for k in range(group):
            for r in range(tiles):
                a, onehot = onehot_window(k, r, MOE_WIN)
                win = pl.ds(a, MOE_WIN)
                gsel = jnp.sum(onehot * aff_ref[k, r], axis=1, keepdims=True)
                ge_ref[k, win, :] = ge_ref[k, win, :] + gsel


def _moe_gather(base_i, loc, aff3, h_chunks, t, row0, n, cap_p):
    e = loc.shape[0]
    nchunk, _, dcol = h_chunks.shape
    d = nchunk * dcol
    nt = n // MOE_TILE
    tiles = _pick(nt, (8, 4, 2, 1))
    rows = tiles * MOE_TILE
    group = _pick(e, (4, 2, 1))
    sel_spec = pl.BlockSpec((group, tiles, 1, LANES), lambda g, c, i, b: (g, i, 0, 0))
    grid_spec = pltpu.PrefetchScalarGridSpec(
        num_scalar_prefetch=1, grid=(e // group, d // dcol, nt // tiles),
        in_specs=[sel_spec, sel_spec,
                  pl.BlockSpec((pl.Element(rows), pl.Element(dcol)),
                               lambda g, c, i, b: (pl.multiple_of(c * t + row0 + i * rows, MOE_TILE), 0))],
        out_specs=(pl.BlockSpec((group, cap_p, dcol), lambda g, c, i, b: (g, 0, c)),
                   pl.BlockSpec((group, cap_p, LANES), lambda g, c, i, b: (g, 0, 0))))
    return pl.pallas_call(
        functools.partial(_moe_gather_kernel, tiles=tiles, group=group), grid_spec=grid_spec,
        out_shape=(jax.ShapeDtypeStruct((e, cap_p, d), BF16), jax.ShapeDtypeStruct((e, cap_p, LANES), F32)),
        compiler_params=_cp(3), name="moe_gather",
    )(base_i, loc, aff3, h_chunks.reshape(nchunk * t, dcol))


def _moe_ffn_kernel(x_ref, g_ref, wg_ref, wu_ref, wd_ref, y_ref):
    x = x_ref[0]
    hg = _dg(x, wg_ref[0])
    hid = hg * _sigmoid(hg) * _dg(x, wu_ref[0])
    y = _dg(hid.astype(BF16), wd_ref[0]) * g_ref[0][:, 0:1]
    y_ref[0] = y.astype(y_ref.dtype)


def _moe_ffn(xe, ge, w_gate, w_up, w_down, layer, tc):
    e, cap_p, d = xe.shape
    f = w_gate.shape[2]
    return pl.pallas_call(
        _moe_ffn_kernel,
        grid=(e, cap_p // tc),
        in_specs=[pl.BlockSpec((1, tc, d), lambda ee, i: (ee, i, 0)),
                  pl.BlockSpec((1, tc, LANES), lambda ee, i: (ee, i, 0)),
                  pl.BlockSpec((1, d, f), lambda ee, i: (layer * e + ee, 0, 0)),
                  pl.BlockSpec((1, d, f), lambda ee, i: (layer * e + ee, 0, 0)),
                  pl.BlockSpec((1, f, d), lambda ee, i: (layer * e + ee, 0, 0))],
        out_specs=pl.BlockSpec((1, tc, d), lambda ee, i: (ee, i, 0)),
        out_shape=jax.ShapeDtypeStruct((e, cap_p, d), BF16),
        compiler_params=_cp(2), name="moe_ffn",
    )(xe, ge, w_gate, w_up, w_down)


def _moe_combine_kernel(base_ref, loc_ref, *rest, n_exp):
    y_refs = rest[:n_exp]
    x_ref, gate_ref, o_ref = rest[n_exp:]
    i = pl.program_id(0)
    win_row = _iota((MOE_TILE, MOE_WIN), 1).astype(F32)
    acc = None
    for e in range(n_exp):
        _, off = _window_start(base_ref, e, i)
        slot_row = jnp.broadcast_to(loc_ref[e, 0] + off.astype(F32), (MOE_TILE, LANES))
        slot_col = slot_row.T
        slot_col = jnp.concatenate([slot_col, slot_col[:, :MOE_WIN - LANES]], axis=1)
        onehot = jnp.where(win_row == slot_col, 1.0, 0.0).astype(BF16)
        part = _dg(onehot, y_refs[e][...])
        acc = part if acc is None else acc + part
    o_ref[...] = x_ref[...] + gate_ref[...] * acc


def _moe_combine(base_i, loc, ye, x_all, gate_row, row0, n, rows_only):
    e, cap_p, d = ye.shape
    nt = n // MOE_TILE
    t0 = row0 // MOE_TILE
    o0 = 0 if rows_only else t0
    dcol = d

    def y_spec(ee):
        def index(i, j, b):
            start = ee * cap_p + (b[ee, i] // MOE_ALIGN) * MOE_ALIGN
            return pl.multiple_of(start, MOE_ALIGN), pl.multiple_of(j * dcol, LANES)
        return pl.BlockSpec((pl.Element(MOE_WIN), pl.Element(dcol)), index)

    grid_spec = pltpu.PrefetchScalarGridSpec(
        num_scalar_prefetch=1, grid=(nt, d // dcol),
        in_specs=[pl.BlockSpec((e, 1, 1, LANES), lambda i, j, b: (0, i, 0, 0))]
                 + [y_spec(ee) for ee in range(e)]
                 + [pl.BlockSpec((MOE_TILE, dcol), lambda i, j, b: (t0 + i, j)),
                    pl.BlockSpec((1, dcol), lambda i, j, b: (0, j))],
        out_specs=pl.BlockSpec((MOE_TILE, dcol), lambda i, j, b: (o0 + i, j)))
    x_index = 2 + e
    return pl.pallas_call(
        functools.partial(_moe_combine_kernel, n_exp=e), grid_spec=grid_spec,
        out_shape=jax.ShapeDtypeStruct((n, d) if rows_only else x_all.shape, F32),
        input_output_aliases={} if rows_only else {x_index: 0},
        compiler_params=_cp(2), name="moe_combine",
    )(base_i, loc, *([ye.reshape(e * cap_p, d)] * e), x_all, gate_row)


def _moe_stream(x_all, h_chunks, aff, row0, n, gate_row, w_gate, w_up, w_down, layer, rows_only=False):
    e = aff.shape[0]
    cap = (CAPACITY_FACTOR * n) // e
    tc = 256 if cap >= 256 else 64
    cap_p = -(-(cap + MOE_WIN) // tc) * tc
    n_sel = SEL_ROWS * LANES
    aff_s = lax.dynamic_slice_in_dim(aff, row0, n, axis=1)
    aff3 = jnp.pad(aff_s, ((0, 0), (0, n_sel - n))).reshape(e, SEL_ROWS, LANES)
    loc, base = _moe_select(aff3, cap)
    base_i = jnp.concatenate([base[:, :, 0].astype(I32), jnp.full((e, 1), cap, I32)], axis=1)
    loc4 = loc.reshape(e, SEL_ROWS, 1, LANES)
    aff4 = aff3.reshape(e, SEL_ROWS, 1, LANES)
    xe, ge = _moe_gather(base_i, loc4, aff4, h_chunks, x_all.shape[0], row0, n, cap_p)
    ye = _moe_ffn(xe, ge, w_gate, w_up, w_down, layer, tc)
    return _moe_combine(base_i, loc4, ye, x_all, gate_row, row0, n, rows_only)


def _rope_tables(nu, seq):
    n_freq = ROPE_DIM // 4
    pos = jnp.arange(seq)
    inv = jnp.power(ROPE_THETA, -jnp.arange(n_freq, dtype=F32) / n_freq)
    ang_r = (pos // GRID_W).astype(F32)[:, None] * inv[None]
    ang_c = (pos % GRID_W).astype(F32)[:, None] * inv[None]
    cos = jnp.concatenate([jnp.cos(ang_r)] * 2 + [jnp.cos(ang_c)] * 2, axis=1)
    sin = jnp.concatenate([-jnp.sin(ang_r), jnp.sin(ang_r), -jnp.sin(ang_c), jnp.sin(ang_c)], axis=1)
    cos = jnp.concatenate([jnp.ones((nu, ROPE_DIM), F32), cos], axis=0)
    sin = jnp.concatenate([jnp.zeros((nu, ROPE_DIM), F32), sin], axis=0)
    rep = LANES // ROPE_DIM
    return jnp.tile(cos, (1, rep)), jnp.tile(sin, (1, rep))


def _pad_cols(w, n):
    return jnp.pad(w, ((0, 0), (0, n - w.shape[1])))


def _round_up(x, m):
    return -(-x // m) * m


def kernel(x, c, ctx, c_ctx, norm_mix, norm_ffn, ada_down, ada_up, ada_bias, w_in, rwkv_mu, rwkv_w0, rwkv_w_up, rwkv_a0, rwkv_a_up, rwkv_g_up, rwkv_k_k, rwkv_k_a, rwkv_r_k, rwkv_lnx_w, rwkv_lnx_b, gqa_q_norm, gqa_k_norm, gqa_sink, mla_q_a_norm, mla_q_up, mla_kv_a_norm, mla_kv_up, mla_nope_norm, mla_rope_norm, w_branch, w_out, moe_router, moe_w_gate, moe_w_up, moe_w_down):
    bsz, seq, d = x.shape
    assert bsz == 1
    nu = ctx.shape[1]
    depth = w_in.shape[0]
    aw = A_HEADS * A_HEAD_DIM
    dr = rwkv_w_up.shape[2]
    gr = rwkv_g_up.shape[1]
    n_a = 3 * aw + 4 * dr + gr
    n_a_p = _round_up(n_a, LANES)
    n_b = (B_Q_HEADS + 2 * B_KV_HEADS) * B_HEAD_DIM
    cq = mla_q_up.shape[1]
    ckv = mla_kv_up.shape[1]
    n_c = cq + ckv + C_ROPE_DIM
    n_c_p = cq + ckv + LANES
    hw = 2 * LANES

    x_all = jnp.concatenate([ctx[0], x[0]], axis=0)
    cond8 = jnp.zeros((8, d), F32).at[0].set(c_ctx).at[1].set(c[0])
    cos, sin = _rope_tables(nu, seq)

    zcols = lambda w_: jnp.zeros((depth, d, w_), BF16)
    wb = w_in.astype(BF16)
    w_pad = jnp.concatenate([wb[:, :, :n_a], zcols(n_a_p - n_a), wb[:, :, n_a:n_a + n_b + n_c],
                             zcols(n_c_p - n_c), wb[:, :, n_a + n_b + n_c:]], axis=2).reshape(depth * d, -1)
    n_exp = moe_router.shape[2]
    w_branch_b = w_branch.astype(BF16).reshape(depth * N_BRANCH, aw, d)
    w_out_b = w_out.astype(BF16).reshape(depth * d, d)
    wg = moe_w_gate.astype(BF16).reshape(depth * n_exp, d, -1)
    wu = moe_w_up.astype(BF16).reshape(depth * n_exp, d, -1)
    wd = moe_w_down.astype(BF16).reshape(depth * n_exp, -1, d)

    for i in range(depth):
        mu_p = _pad_cols(rwkv_mu[i], n_a_p)
        g_up_p = jnp.pad(rwkv_g_up[i], ((0, n_a_p - n_a), (0, 0)))
        qu = mla_q_up[i].reshape(cq, C_HEADS, C_NOPE_DIM + C_ROPE_DIM)
        qu = jnp.pad(qu, ((0, 0), (0, 0), (0, hw - C_NOPE_DIM - C_ROPE_DIM))).reshape(cq, C_HEADS * hw).astype(BF16)
        kvu = mla_kv_up[i].reshape(ckv, C_HEADS, C_NOPE_DIM + C_V_DIM)
        kvu = jnp.concatenate([kvu[:, :, :C_NOPE_DIM].reshape(ckv, -1), kvu[:, :, C_NOPE_DIM:].reshape(ckv, -1)],
                              axis=1).astype(BF16)
        rope_norm_p = _pad_cols(mla_rope_norm[i], LANES)

        mod = _adaln(cond8, ada_down[i], ada_up[i], ada_bias[i])[:2].reshape(2, N_MOD, d)

        h = _norm_mod(x_all, norm_mix[i], mod[:, 0], mod[:, 1], nu)
        w_in_i = functools.partial(_matmul, h, w_pad, w_row0=i * d, k=d)
        za = w_in_i(F32, n=n_a_p, name="w_in_a")
        zb = w_in_i(F32, w_col0=n_a_p, n=n_b, name="w_in_b")
        zc = w_in_i(F32, w_col0=n_a_p + n_b, n=n_c_p, name="w_in_c")
        gates = w_in_i(BF16, w_col0=n_a_p + n_b + n_c_p, n=N_BRANCH * d, act="sigmoid", name="w_in_gate")

        r, k, v, e, a, g = _rwkv_feat(za, mu_p, rwkv_w0[i], rwkv_w_up[i], rwkv_a0[i], rwkv_a_up[i], g_up_p, nu)
        ys = []
        for direction in (0, 1):
            p_, q_, r2, y0 = _rwkv_chunks(r, k, v, e, a, rwkv_k_k[i], rwkv_k_a[i], direction)
            ys.append(_rwkv_seq(p_, q_, r2, y0, direction, nu))
        ya = _rwkv_out(ys[0], ys[1], r, k, v, a, g, rwkv_lnx_w[i], rwkv_lnx_b[i], rwkv_r_k[i], rwkv_k_a[i])

        qkv_b = _gqa_prep(zb, cos, sin, gqa_q_norm[i], gqa_k_norm[i])
        yb = _gqa_attn(qkv_b, gqa_sink[i], nu, seq)

        assert cq % ckv == 0
        q_c = _matmul(zc, qu, F32, rms_gain=mla_q_a_norm[i], name="mla_q_up")
        kv_c = _matmul(zc, kvu, F32, a_col_block=cq // ckv, rms_gain=mla_kv_a_norm[i], name="mla_kv_up")
        qf, kf, vf = _mla_prep(q_c, kv_c, zc, (cq + ckv) // LANES, cos, sin, mla_nope_norm[i], rope_norm_p)
        yc = _mla_flash(qf, kf, vf, nu)

        merged = _merge(ya, yb, yc, w_branch_b, i, gates)
        x_all = _matmul(merged, w_out_b, F32, w_row0=i * d, k=d, resid=x_all, gate2=mod[:, 2], nu=nu, name="w_out")

        hf, aff = _norm_mod(x_all, norm_ffn[i], mod[:, 3], mod[:, 4], nu, router_t=moe_router[i].T)
        last = i == depth - 1
        x_all = _moe_stream(x_all, hf, aff, nu, seq, mod[1:2, 5], wg, wu, wd, i, rows_only=last)
        if not last:
            x_all = _moe_stream(x_all, hf, aff, 0, nu, mod[0:1, 5], wg, wu, wd, i)
    return x_all.reshape(bsz, seq, d)
```
